```python
import jax, jax.numpy as jnp
from jax import lax
import numpy as np

D_MODEL = 2048
BATCH = 8
SEQ = 2048
DEPTH = 1

D_MIX = D_MODEL
LRU_WIDTH = D_MIX // 2
LRU_BLOCKS = 8
LRU_BLOCK_W = LRU_WIDTH // LRU_BLOCKS
CONV_W = 4
LRU_C = 8.0
ATTN_WIDTH = D_MIX - LRU_WIDTH
HEAD_DIM = 128
N_HEADS = ATTN_WIDTH // HEAD_DIM
ROT_DIM = HEAD_DIM // 4
ROPE_THETA = 500000.0
DILATED_PATTERNS = ((128, 1), (512, 4), (2048, 16))
ATTN_BLOCK = 128
N_EXPERTS = 64
N_GROUPS = 8
TOPK_GROUPS = 4
TOP_K = 8
D_EXPERT = D_MODEL // 4
ROUTED_SCALE = 2.5
EXPERT_BLOCK = 128
DN_ALPHA = (2 * DEPTH) ** 0.25
DN_BETA = (8 * DEPTH) ** -0.25
LN_EPS = 1e-5
COL_SPLITS = [LRU_WIDTH, 2 * LRU_WIDTH, 2 * LRU_WIDTH + ATTN_WIDTH, 2 * LRU_WIDTH + 2 * ATTN_WIDTH]

kernel_name = 'hymba_rglru_dilated_moe_deepnorm'

F32 = jnp.float32


def _layer_norm(x, g, b):
    xf = x.astype(F32)
    mu = jnp.mean(xf, -1, keepdims=True)
    xc = xf - mu
    var = jnp.mean(xc * xc, -1, keepdims=True)
    return (xc * lax.rsqrt(var + LN_EPS) * g.astype(F32) + b.astype(F32)).astype(x.dtype)


def _partial_rope(t):
    S = t.shape[1]
    half = ROT_DIM // 2
    pos = jnp.arange(S, dtype=F32)
    inv = ROPE_THETA ** (-jnp.arange(0, ROT_DIM, 2, dtype=F32) / ROT_DIM)
    ang = pos[:, None] * inv[None, :]
    cos = jnp.cos(ang)[None, :, None, :]
    sin = jnp.sin(ang)[None, :, None, :]
    t1, t2, tp = t[..., :half], t[..., half:ROT_DIM], t[..., ROT_DIM:]
    return jnp.concatenate([t1 * cos - t2 * sin, t2 * cos + t1 * sin, tp], -1)


def _rglru_branch(xb, gb, conv_w, conv_b, w_a, b_a, w_x, b_x, lam):
    B, S, C = xb.shape
    xc = lax.conv_general_dilated(xb, conv_w.reshape(CONV_W, 1, C), (1,), [(CONV_W - 1, 0)],
                                  dimension_numbers=('NWC', 'WIO', 'NWC'), feature_group_count=C)
    xc = (xc + conv_b).astype(F32)
    xr = xc.reshape(B, S, LRU_BLOCKS, LRU_BLOCK_W)
    r = jax.nn.sigmoid(jnp.einsum('bsnc,ncd->bsnd', xr, w_a.astype(F32)) + b_a.astype(F32)).reshape(B, S, C)
    i = jax.nn.sigmoid(jnp.einsum('bsnc,ncd->bsnd', xr, w_x.astype(F32)) + b_x.astype(F32)).reshape(B, S, C)
    log_a = -LRU_C * r * jax.nn.softplus(-lam.astype(F32))
    a = jnp.exp(log_a)
    u = jnp.sqrt(-jnp.expm1(2.0 * log_a)) * (i * xc)

    def comb(lhs, rhs):
        a1, b1 = lhs
        a2, b2 = rhs
        return a1 * a2, a2 * b1 + b2

    _, h = lax.associative_scan(comb, (a, u), axis=1)
    return h * jax.nn.gelu(gb.astype(F32), approximate=True)


def _dilated_window_attention(q, k, v, window, dilation):
    B, S, H, Dh = q.shape
    L = S // dilation
    W = window // dilation
    Q = min(ATTN_BLOCK, L)
    nb = -(-L // Q)
    Lp = nb * Q
    n = B * dilation

    def split(t):
        return t.reshape(B, L, dilation, H, Dh).transpose(0, 2, 1, 3, 4).reshape(n, L, H, Dh)

    qs = jnp.pad(split(q), ((0, 0), (0, Lp - L), (0, 0), (0, 0)))
    ks = jnp.pad(split(k), ((0, 0), (W, Lp - L), (0, 0), (0, 0)))
    vs = jnp.pad(split(v), ((0, 0), (W, Lp - L), (0, 0), (0, 0)))
    kidx = jnp.arange(nb)[:, None] * Q + jnp.arange(Q + W)[None, :]
    kb = ks[:, kidx]
    vb = vs[:, kidx]
    qb = qs.reshape(n, nb, Q, H, Dh)
    s = jnp.einsum('nbqhd,nbkhd->nbhqk', qb, kb) * (Dh ** -0.5)
    qpos = jnp.arange(Lp).reshape(nb, Q)
    kpos = kidx - W
    diff = qpos[:, :, None] - kpos[:, None, :]
    valid = (diff >= 0) & (diff <= W) & (kpos[:, None, :] >= 0)
    s = jnp.where(valid[None, :, None], s, -jnp.inf)
    m = jnp.max(s, -1)
    p = jnp.exp(s - m[..., None])
    den = jnp.sum(p, -1)
    o = jnp.einsum('nbhqk,nbkhd->nbqhd', p, vb) / jnp.swapaxes(den, -1, -2)[..., None]

    def merge(t):
        t = t[:, :L]
        rest = t.shape[2:]
        return t.reshape((B, dilation, L) + rest).swapaxes(1, 2).reshape((B, S) + rest)

    o = merge(o.reshape(n, Lp, H, Dh))
    m = merge(jnp.swapaxes(m, -1, -2).reshape(n, Lp, H))
    den = merge(jnp.swapaxes(den, -1, -2).reshape(n, Lp, H))
    return o, m, den


def _mixture_of_dilations(q, k, v):
    res = [_dilated_window_attention(q, k, v, w, d) for (w, d) in DILATED_PATTERNS]
    m_all = jnp.stack([r[1] for r in res])
    mx = jnp.max(m_all, 0)
    wts = jnp.stack([r[2] for r in res]) * jnp.exp(m_all - mx)
    o_all = jnp.stack([r[0] for r in res])
    return jnp.einsum('pbsh,pbshd->bshd', wts, o_all) / jnp.sum(wts, 0)[..., None]


def _route(xf, w_router, router_bias):
    N = xf.shape[0]
    scores = jax.nn.sigmoid(xf.astype(F32) @ w_router.astype(F32))
    sel = scores + router_bias.astype(F32)
    grp = sel.reshape(N, N_GROUPS, N_EXPERTS // N_GROUPS)
    gscore = jnp.sum(lax.top_k(grp, 2)[0], -1)
    _, gidx = lax.top_k(gscore, TOPK_GROUPS)
    gmask = jnp.sum(jax.nn.one_hot(gidx, N_GROUPS, dtype=F32), 1) > 0
    emask = jnp.repeat(gmask, N_EXPERTS // N_GROUPS, axis=1)
    _, idx = lax.top_k(jnp.where(emask, sel, -jnp.inf), TOP_K)
    g = jnp.take_along_axis(scores, idx, 1)
    g = g / jnp.sum(g, -1, keepdims=True) * ROUTED_SCALE
    return idx, g


def _routed_experts(xf, idx, gates, w_gate, w_up, w_down):
    N, D = xf.shape
    A = N * TOP_K
    e_flat = idx.reshape(A)
    tok_flat = jnp.repeat(jnp.arange(N, dtype=jnp.int32), TOP_K)
    g_flat = gates.reshape(A)
    order = jnp.argsort(e_flat)
    e_sorted = e_flat[order]
    counts = jnp.zeros((N_EXPERTS,), jnp.int32).at[e_flat].add(1)
    padded = (counts + EXPERT_BLOCK - 1) // EXPERT_BLOCK * EXPERT_BLOCK
    end_pad = jnp.cumsum(padded)
    start_pad = end_pad - padded
    start = jnp.cumsum(counts) - counts
    dest = start_pad[e_sorted] + jnp.arange(A, dtype=jnp.int32) - start[e_sorted]
    P = A + N_EXPERTS * EXPERT_BLOCK
    nblk = P // EXPERT_BLOCK
    slot_tok = jnp.full((P,), N, jnp.int32).at[dest].set(tok_flat[order])
    slot_gate = jnp.zeros((P,), F32).at[dest].set(g_flat[order])
    blk_start = jnp.arange(nblk, dtype=jnp.int32) * EXPERT_BLOCK
    blk_expert = jnp.minimum(jnp.searchsorted(end_pad, blk_start, side='right'), N_EXPERTS - 1)
    xpad = jnp.concatenate([xf, jnp.zeros((1, D), xf.dtype)], 0)

    def body(acc, blk):
        tok, g, e = blk
        xb = xpad[tok]
        h = jax.nn.silu(xb @ w_gate[e]) * (xb @ w_up[e])
        y = (h @ w_down[e]) * g[:, None].astype(h.dtype)
        return acc.at[tok].add(y.astype(acc.dtype)), None

    acc0 = jnp.zeros((N + 1, D), xf.dtype)
    acc, _ = lax.scan(body, acc0, (slot_tok.reshape(nblk, EXPERT_BLOCK),
                                   slot_gate.reshape(nblk, EXPERT_BLOCK), blk_expert))
    return acc[:N]


def _moe(x, w_router, router_bias, w_gate, w_up, w_down, ws_gate, ws_up, ws_down):
    B, S, D = x.shape
    xf = x.reshape(B * S, D)
    idx, g = _route(xf, w_router, router_bias)
    routed = _routed_experts(xf, idx, g, w_gate, w_up, w_down)
    shared = (jax.nn.silu(xf @ ws_gate) * (xf @ ws_up)) @ ws_down
    return (routed + shared).reshape(B, S, D)


def setup_inputs(seed: int = 0) -> dict:
    key = jax.random.key(seed)
    ks = jax.random.split(key, 26)
    L = DEPTH

    def nrm(k, shape, scale):
        return jax.random.normal(k, shape, F32) * scale

    x = nrm(ks[0], (BATCH, SEQ, D_MODEL), 1.0)
    w_in = jnp.concatenate([
        nrm(ks[1], (L, D_MODEL, 2 * LRU_WIDTH + 2 * ATTN_WIDTH), D_MODEL ** -0.5),
        nrm(ks[2], (L, D_MODEL, ATTN_WIDTH), DN_BETA * D_MODEL ** -0.5)], -1)
    conv_w = nrm(ks[3], (L, CONV_W, LRU_WIDTH), CONV_W ** -0.5)
    conv_b = nrm(ks[4], (L, LRU_WIDTH), 0.02)
    w_rg_a = nrm(ks[5], (L, LRU_BLOCKS, LRU_BLOCK_W, LRU_BLOCK_W), LRU_BLOCK_W ** -0.5)
    b_rg_a = nrm(ks[6], (L, LRU_BLOCKS, LRU_BLOCK_W), 0.02)
    w_rg_x = nrm(ks[7], (L, LRU_BLOCKS, LRU_BLOCK_W, LRU_BLOCK_W), LRU_BLOCK_W ** -0.5)
    b_rg_x = nrm(ks[8], (L, LRU_BLOCKS, LRU_BLOCK_W), 0.02)
    u = jax.random.uniform(ks[9], (L, LRU_WIDTH), dtype=F32, minval=0.9, maxval=0.999)
    sa = u ** (1.0 / LRU_C)
    lru_lambda = jnp.log(sa) - jnp.log1p(-sa)
    w_out = nrm(ks[10], (L, D_MIX, D_MODEL), DN_BETA * D_MIX ** -0.5)
    ln1_g = 1.0 + nrm(ks[11], (L, D_MODEL), 0.02)
    ln1_b = nrm(ks[12], (L, D_MODEL), 0.02)
    w_router = nrm(ks[13], (L, D_MODEL, N_EXPERTS), D_MODEL ** -0.5)
    router_bias = nrm(ks[14], (L, N_EXPERTS), 0.01)
    w_gate = nrm(ks[15], (L, N_EXPERTS, D_MODEL, D_EXPERT), D_MODEL ** -0.5)
    w_up = nrm(ks[16], (L, N_EXPERTS, D_MODEL, D_EXPERT), DN_BETA * D_MODEL ** -0.5)
    w_down = nrm(ks[17], (L, N_EXPERTS, D_EXPERT, D_MODEL), DN_BETA * D_EXPERT ** -0.5)
    ws_gate = nrm(ks[18], (L, D_MODEL, D_EXPERT), D_MODEL ** -0.5)
    ws_up = nrm(ks[19], (L, D_MODEL, D_EXPERT), DN_BETA * D_MODEL ** -0.5)
    ws_down = nrm(ks[20], (L, D_EXPERT, D_MODEL), DN_BETA * D_EXPERT ** -0.5)
    ln2_g = 1.0 + nrm(ks[21], (L, D_MODEL), 0.02)
    ln2_b = nrm(ks[22], (L, D_MODEL), 0.02)
    return {'x': x, 'w_in': w_in, 'conv_w': conv_w, 'conv_b': conv_b,
            'w_rg_a': w_rg_a, 'b_rg_a': b_rg_a, 'w_rg_x': w_rg_x, 'b_rg_x': b_rg_x,
            'lru_lambda': lru_lambda, 'w_out': w_out, 'ln1_g': ln1_g, 'ln1_b': ln1_b,
            'w_router': w_router, 'router_bias': router_bias, 'w_gate': w_gate,
            'w_up': w_up, 'w_down': w_down, 'ws_gate': ws_gate, 'ws_up': ws_up,
            'ws_down': ws_down, 'ln2_g': ln2_g, 'ln2_b': ln2_b}


def reference(x, w_in, conv_w, conv_b, w_rg_a, b_rg_a, w_rg_x, b_rg_x, lru_lambda, w_out,
              ln1_g, ln1_b, w_router, router_bias, w_gate, w_up, w_down, ws_gate, ws_up,
              ws_down, ln2_g, ln2_b):
    B, S, _ = x.shape
    for l in range(DEPTH):
        proj = jnp.einsum('bsd,de->bse', x, w_in[l])
        xl, gl, q, k, v = jnp.split(proj, COL_SPLITS, axis=-1)
        lru = _rglru_branch(xl, gl, conv_w[l], conv_b[l], w_rg_a[l], b_rg_a[l],
                            w_rg_x[l], b_rg_x[l], lru_lambda[l])
        qh = _partial_rope(q.astype(F32).reshape(B, S, N_HEADS, HEAD_DIM))
        kh = _partial_rope(k.astype(F32).reshape(B, S, N_HEADS, HEAD_DIM))
        vh = v.astype(F32).reshape(B, S, N_HEADS, HEAD_DIM)
        att = _mixture_of_dilations(qh, kh, vh).reshape(B, S, ATTN_WIDTH)
        mix = jnp.concatenate([lru, att], -1).astype(x.dtype) @ w_out[l]
        x = _layer_norm(DN_ALPHA * x + mix, ln1_g[l], ln1_b[l])
        ffn = _moe(x, w_router[l], router_bias[l], w_gate[l], w_up[l], w_down[l],
                   ws_gate[l], ws_up[l], ws_down[l])
        x = _layer_norm(DN_ALPHA * x + ffn, ln2_g[l], ln2_b[l])
    return x
```

```python
import functools

import jax
import jax.numpy as jnp
from jax import lax
from jax.experimental import pallas as pl
from jax.experimental.pallas import tpu as pltpu

F32 = jnp.float32
BF16 = jnp.bfloat16
I32 = jnp.int32

LRU_BLOCKS = 8
CONV_W = 4
LRU_C = 8.0
HEAD_DIM = 128
ROT_DIM = HEAD_DIM // 4
ROPE_THETA = 500000.0
DILATED_PATTERNS = ((128, 1), (512, 4), (2048, 16))
ATTN_BLOCK = 128
N_EXPERTS = 64
N_GROUPS = 8
GROUP_SIZE = N_EXPERTS // N_GROUPS
TOPK_GROUPS = 4
TOP_K = 8
ROUTED_SCALE = 2.5
DEPTH = 1
DN_ALPHA = (2 * DEPTH) ** 0.25
LN_EPS = 1e-5

PROJ_TM = 512
LRU_TC = 256
MIX_TM = 512
EXPERT_ROWS = 256
DISPATCH_TT = 256
COMBINE_TT = 128
SUBLANES = 8
VMEM_LIMIT = 56 * 1024 * 1024


def _cparams(sem):
    return pltpu.CompilerParams(dimension_semantics=sem, vmem_limit_bytes=VMEM_LIMIT)


def _proj_kernel(x_ref, w_ref, c_ref, sa_ref, sb_ref, o_ref, *, n_heads):
    j = pl.program_id(0)
    acc = jnp.dot(x_ref[...].astype(BF16), w_ref[...], preferred_element_type=F32)
    is_rope = (j == 2) | (j == 3)

    @pl.when(is_rope)
    def _():
        scale = jnp.where(j == 2, HEAD_DIM ** -0.5, 1.0).astype(F32)
        c = c_ref[...] * scale
        sa = sa_ref[...] * scale
        sb = sb_ref[...] * scale
        half = ROT_DIM // 2
        for h in range(n_heads):
            t = acc[:, h * HEAD_DIM:(h + 1) * HEAD_DIM]
            o_ref[:, h * HEAD_DIM:(h + 1) * HEAD_DIM] = (
                t * c + pltpu.roll(t, half, 1) * sa + pltpu.roll(t, HEAD_DIM - half, 1) * sb)

    @pl.when(jnp.logical_not(is_rope))
    def _():
        o_ref[...] = acc


def _rope_tables(seq):
    half = ROT_DIM // 2
    pos = jnp.arange(seq, dtype=F32)
    inv = ROPE_THETA ** (-jnp.arange(0, ROT_DIM, 2, dtype=F32) / ROT_DIM)
    ang = pos[:, None] * inv[None, :]
    cos, sin = jnp.cos(ang), jnp.sin(ang)
    ones = jnp.ones((seq, HEAD_DIM - ROT_DIM), F32)
    zeros = jnp.zeros((seq, HEAD_DIM - ROT_DIM), F32)
    zh = jnp.zeros((seq, half), F32)
    c = jnp.concatenate([cos, cos, ones], -1)
    sa = jnp.concatenate([zh, sin, zeros], -1)
    sb = jnp.concatenate([-sin, zh, zeros], -1)
    return c, sa, sb


def _proj(x2, w_in_bf, seq, width):
    n, d = x2.shape
    e = w_in_bf.shape[1]
    n_groups = e // width
    c, sa, sb = _rope_tables(seq)
    tiles_per_seq = seq // PROJ_TM
    tab_spec = pl.BlockSpec((PROJ_TM, HEAD_DIM), lambda j, i: (i % tiles_per_seq, 0))
    return pl.pallas_call(
        functools.partial(_proj_kernel, n_heads=width // HEAD_DIM),
        grid=(n_groups, n // PROJ_TM),
        in_specs=[pl.BlockSpec((PROJ_TM, d), lambda j, i: (i, 0)),
                  pl.BlockSpec((d, width), lambda j, i: (0, j)),
                  tab_spec, tab_spec, tab_spec],
        out_specs=pl.BlockSpec((PROJ_TM, width), lambda j, i: (i, j)),
        out_shape=jax.ShapeDtypeStruct((n, e), F32),
        compiler_params=_cparams(("arbitrary", "arbitrary")),
        name="proj_rope",
    )(x2, w_in_bf, c, sa, sb)


def _lru_kernel(xl_ref, gl_ref, cw_ref, cb_ref, wa_ref, ba_ref, wx_ref, bx_ref, lam_ref,
                o_ref, xext_ref, hprev_ref, a_ref, b_ref, h_ref, *, block_w):
    tc, width = xl_ref.shape
    groups = tc // SUBLANES

    @pl.when(pl.program_id(1) == 0)
    def _():
        xext_ref[0:SUBLANES, :] = jnp.zeros((SUBLANES, width), F32)
        hprev_ref[...] = jnp.zeros_like(hprev_ref)

    x = xl_ref[...]
    xext_ref[SUBLANES:SUBLANES + tc, :] = x
    xc = x * cw_ref[CONV_W - 1:CONV_W, :] + cb_ref[...]
    for j in range(CONV_W - 1):
        back = CONV_W - 1 - j
        xc = xc + xext_ref[pl.ds(SUBLANES - back, tc), :] * cw_ref[j:j + 1, :]
    xext_ref[0:SUBLANES, :] = x[tc - SUBLANES:, :]

    lam = lam_ref[...]
    softplus_neg_lam = jnp.maximum(-lam, 0.0) + jnp.log1p(jnp.exp(-jnp.abs(lam)))
    rs, is_ = [], []
    for nb in range(width // block_w):
        xb = xc[:, nb * block_w:(nb + 1) * block_w].astype(BF16)
        rs.append(jnp.dot(xb, wa_ref[nb], preferred_element_type=F32))
        is_.append(jnp.dot(xb, wx_ref[nb], preferred_element_type=F32))
    r = jax.nn.sigmoid(jnp.concatenate(rs, axis=1) + ba_ref[...])
    ig = jax.nn.sigmoid(jnp.concatenate(is_, axis=1) + bx_ref[...])
    a = jnp.exp(-LRU_C * r * softplus_neg_lam)
    b = jnp.sqrt(1.0 - a * a) * (ig * xc)

    row = lax.broadcasted_iota(I32, (tc, width), 0) & (SUBLANES - 1)
    k = 1
    while k < SUBLANES:
        keep = row >= k
        a_sh = pltpu.roll(a, k, 0)
        b_sh = pltpu.roll(b, k, 0)
        b = jnp.where(keep, a * b_sh + b, b)
        a = jnp.where(keep, a * a_sh, a)
        k *= 2
    a_ref[...] = a.reshape(groups, SUBLANES, width)
    b_ref[...] = b.reshape(groups, SUBLANES, width)

    def carry(g, hprev):
        h = a_ref[g] * hprev + b_ref[g]
        h_ref[g] = h
        return h[SUBLANES - 1:SUBLANES, :]

    hprev_ref[...] = lax.fori_loop(0, groups, carry, hprev_ref[...])
    h = h_ref[...].reshape(tc, width)
    o_ref[...] = (h * jax.nn.gelu(gl_ref[...], approximate=True)).astype(o_ref.dtype)


def _rglru(proj3, conv_w, conv_b, wa_bf, b_a, wx_bf, b_x, lam, width):
    b, s, _ = proj3.shape
    block_w = width // LRU_BLOCKS
    vec = pl.BlockSpec((1, width), lambda bi, ci: (0, 0))
    wspec = pl.BlockSpec((LRU_BLOCKS, block_w, block_w), lambda bi, ci: (0, 0, 0))
    groups = LRU_TC // SUBLANES
    return pl.pallas_call(
        functools.partial(_lru_kernel, block_w=block_w),
        grid=(b, s // LRU_TC),
        in_specs=[pl.BlockSpec((None, LRU_TC, width), lambda bi, ci: (bi, ci, 0)),
                  pl.BlockSpec((None, LRU_TC, width), lambda bi, ci: (bi, ci, 1)),
                  pl.BlockSpec((CONV_W, width), lambda bi, ci: (0, 0)),
                  vec, wspec, vec, wspec, vec, vec],
        out_specs=pl.BlockSpec((None, LRU_TC, width), lambda bi, ci: (bi, ci, 0)),
        out_shape=jax.ShapeDtypeStruct((b, s, width), BF16),
        scratch_shapes=[pltpu.VMEM((LRU_TC + SUBLANES, width), F32),
                        pltpu.VMEM((1, width), F32),
                        pltpu.VMEM((groups, SUBLANES, width), F32),
                        pltpu.VMEM((groups, SUBLANES, width), F32),
                        pltpu.VMEM((groups, SUBLANES, width), F32)],
        compiler_params=_cparams(("arbitrary", "arbitrary")),
        name="rglru",
    )(proj3, proj3, conv_w, conv_b.reshape(1, width), wa_bf, b_a.reshape(1, width),
      wx_bf, b_x.reshape(1, width), lam.reshape(1, width))


def _attn_kernel(q_ref, k_ref, v_ref, o_ref, qs, ks, vs, acc_s, m_s, l_s, acc_n, m_n, l_n):
    seq = q_ref.shape[0]
    blk = ATTN_BLOCK
    nblk = seq // blk
    n_pat = len(DILATED_PATTERNS)

    for p, (_, d) in enumerate(DILATED_PATTERNS):
        cls_len = seq // d
        ks[p, 0:blk, :] = jnp.zeros((blk, HEAD_DIM), BF16)
        vs[p, 0:blk, :] = jnp.zeros((blk, HEAD_DIM), BF16)
        for r in range(d):
            src = pl.ds(r, cls_len, stride=d) if d > 1 else pl.ds(0, seq)
            qs[p, r * cls_len:(r + 1) * cls_len, :] = q_ref[src, :].astype(BF16)
            ks[p, blk + r * cls_len:blk + (r + 1) * cls_len, :] = k_ref[src, :].astype(BF16)
            vs[p, blk + r * cls_len:blk + (r + 1) * cls_len, :] = v_ref[src, :].astype(BF16)

    qi = lax.broadcasted_iota(I32, (blk, 2 * blk), 0)
    kj = lax.broadcasted_iota(I32, (blk, 2 * blk), 1)
    band2 = (kj >= qi) & (kj <= qi + blk)
    in_block = kj >= blk
    qi1 = lax.broadcasted_iota(I32, (blk, blk), 0)
    kj1 = lax.broadcasted_iota(I32, (blk, blk), 1)
    causal1 = kj1 <= qi1
    nt = (((1,), (1,)), ((), ()))

    for p, (window, d) in enumerate(DILATED_PATTERNS):
        assert window // d == blk
        blocks_per_class = (seq // d) // blk

        def body(g, carry, p=p, blocks_per_class=blocks_per_class):
            row0 = pl.multiple_of(g * blk, blk)
            q = qs[p, pl.ds(row0, blk), :]
            if blocks_per_class == 1:
                kk = ks[p, pl.ds(row0 + blk, blk), :]
                vv = vs[p, pl.ds(row0 + blk, blk), :]
                mask = causal1
            else:
                kk = ks[p, pl.ds(row0, 2 * blk), :]
                vv = vs[p, pl.ds(row0, 2 * blk), :]
                first = (g % blocks_per_class) == 0
                mask = band2 & (in_block | jnp.logical_not(first))
            s = lax.dot_general(q, kk, nt, preferred_element_type=F32)
            s = jnp.where(mask, s, -jnp.inf)
            m = jnp.max(s, axis=1, keepdims=True)
            e = jnp.exp(s - m)
            l = jnp.sum(e, axis=1, keepdims=True)
            acc = jnp.dot(e.astype(BF16), vv, preferred_element_type=F32)
            acc_s[p, pl.ds(row0, blk), :] = acc
            m_s[p, pl.ds(row0, blk), :] = jnp.broadcast_to(m, (blk, HEAD_DIM))
            l_s[p, pl.ds(row0, blk), :] = jnp.broadcast_to(l, (blk, HEAD_DIM))
            return carry

        lax.fori_loop(0, nblk, body, 0)

    for p, (_, d) in enumerate(DILATED_PATTERNS):
        cls_len = seq // d
        for r in range(d):
            dst = pl.ds(r, cls_len, stride=d) if d > 1 else pl.ds(0, seq)
            acc_n[p, dst, :] = acc_s[p, r * cls_len:(r + 1) * cls_len, :]
            m_n[p, dst, :] = m_s[p, r * cls_len:(r + 1) * cls_len, :]
            l_n[p, dst, :] = l_s[p, r * cls_len:(r + 1) * cls_len, :]

    mx = m_n[0]
    for p in range(1, n_pat):
        mx = jnp.maximum(mx, m_n[p])
    num = jnp.zeros((seq, HEAD_DIM), F32)
    den = jnp.zeros((seq, HEAD_DIM), F32)
    for p in range(n_pat):
        w = jnp.exp(m_n[p] - mx)
        num = num + w * acc_n[p]
        den = den + w * l_n[p]
    o_ref[...] = (num / den).astype(o_ref.dtype)


def _attention(proj3, width):
    b, s, _ = proj3.shape
    n_heads = width // HEAD_DIM
    n_pat = len(DILATED_PATTERNS)
    col0 = 2 * width // HEAD_DIM

    def spec(group):
        return pl.BlockSpec((None, s, HEAD_DIM),
                            lambda bi, hi: (bi, 0, col0 + group * n_heads + hi))

    return pl.pallas_call(
        _attn_kernel,
        grid=(b, n_heads),
        in_specs=[spec(0), spec(1), spec(2)],
        out_specs=pl.BlockSpec((None, s, HEAD_DIM), lambda bi, hi: (bi, 0, hi)),
        out_shape=jax.ShapeDtypeStruct((b, s, width), BF16),
        scratch_shapes=[pltpu.VMEM((n_pat, s, HEAD_DIM), BF16),
                        pltpu.VMEM((n_pat, s + ATTN_BLOCK, HEAD_DIM), BF16),
                        pltpu.VMEM((n_pat, s + ATTN_BLOCK, HEAD_DIM), BF16)]
                       + [pltpu.VMEM((n_pat, s, HEAD_DIM), F32)] * 6,
        compiler_params=_cparams(("arbitrary", "arbitrary")),
        name="dilated_attention",
    )(proj3, proj3, proj3)


def _layer_norm(y, g, b):
    mu = jnp.mean(y, axis=-1, keepdims=True)
    yc = y - mu
    var = jnp.mean(yc * yc, axis=-1, keepdims=True)
    return yc * lax.rsqrt(var + LN_EPS) * g + b


def _first_index(hit, iota, size):
    return jnp.min(jnp.where(hit, iota, size), axis=0, keepdims=True)


def _mix_kernel(lru_ref, att_ref, x_ref, wo_ref, g_ref, b_ref, wrh_ref, wrl_ref, rb_ref, tri_ref,
                x1_ref, idx_ref, gate_ref, rank_ref, cnt_ref, base_ref):
    tm = x_ref.shape[0]
    half = lru_ref.shape[1]

    @pl.when(pl.program_id(0) == 0)
    def _():
        base_ref[...] = jnp.zeros_like(base_ref)

    mix = jnp.dot(lru_ref[...], wo_ref[0:half, :], preferred_element_type=F32)
    mix = mix + jnp.dot(att_ref[...], wo_ref[half:, :], preferred_element_type=F32)
    x1 = _layer_norm(DN_ALPHA * x_ref[...] + mix, g_ref[...], b_ref[...])
    x1_ref[...] = x1

    x_hi = x1.astype(BF16)
    x_lo = (x1 - x_hi.astype(F32)).astype(BF16)
    nt = (((1,), (1,)), ((), ()))
    logits = lax.dot_general(wrh_ref[...], x_hi, nt, preferred_element_type=F32)
    logits = logits + lax.dot_general(wrh_ref[...], x_lo, nt, preferred_element_type=F32)
    logits = logits + lax.dot_general(wrl_ref[...], x_hi, nt, preferred_element_type=F32)
    scores = jax.nn.sigmoid(logits)
    sel = scores + rb_ref[...]

    neg = -jnp.inf
    gi = lax.broadcasted_iota(I32, (GROUP_SIZE, tm), 0)
    grp_rows = []
    for g in range(N_GROUPS):
        blk = sel[g * GROUP_SIZE:(g + 1) * GROUP_SIZE, :]
        m1 = jnp.max(blk, axis=0, keepdims=True)
        f1 = _first_index(blk == m1, gi, GROUP_SIZE)
        m2 = jnp.max(jnp.where(gi == f1, neg, blk), axis=0, keepdims=True)
        grp_rows.append(m1 + m2)
    ni = lax.broadcasted_iota(I32, (N_GROUPS, tm), 0)
    gscore = jnp.zeros((N_GROUPS, tm), F32)
    for g in range(N_GROUPS):
        gscore = jnp.where(ni == g, grp_rows[g], gscore)
    gmask = jnp.zeros((N_GROUPS, tm), jnp.bool_)
    for _ in range(TOPK_GROUPS):
        m = jnp.max(gscore, axis=0, keepdims=True)
        f = _first_index(gscore == m, ni, N_GROUPS)
        hit = ni == f
        gmask = gmask | hit
        gscore = jnp.where(hit, neg, gscore)
    emask = jnp.concatenate(
        [jnp.broadcast_to(gmask[g:g + 1, :], (GROUP_SIZE, tm)) for g in range(N_GROUPS)], axis=0)
    masked = jnp.where(emask, sel, neg)

    ei = lax.broadcasted_iota(I32, (N_EXPERTS, tm), 0)
    chosen = jnp.zeros((N_EXPERTS, tm), jnp.bool_)
    picks, gates = [], []
    for _ in range(TOP_K):
        m = jnp.max(masked, axis=0, keepdims=True)
        f = _first_index(masked == m, ei, N_EXPERTS)
        hit = ei == f
        picks.append(f)
        gates.append(jnp.sum(jnp.where(hit, scores, 0.0), axis=0, keepdims=True))
        chosen = chosen | hit
        masked = jnp.where(hit, neg, masked)
    gsum = gates[0]
    for k in range(1, TOP_K):
        gsum = gsum + gates[k]

    chosen_f = jnp.where(chosen, 1.0, 0.0)
    incl = jnp.dot(chosen_f.astype(BF16), tri_ref[...], preferred_element_type=F32)
    rank_excl = base_ref[:, 0:1] + incl - chosen_f
    base_ref[...] = base_ref[...] + jnp.sum(chosen_f, axis=1, keepdims=True)
    cnt_ref[...] = base_ref[...]
    for k in range(TOP_K):
        idx_ref[k:k + 1, :] = picks[k]
        gate_ref[k:k + 1, :] = gates[k] / gsum * ROUTED_SCALE
        rk = jnp.sum(jnp.where(ei == picks[k], rank_excl, 0.0), axis=0, keepdims=True)
        rank_ref[k:k + 1, :] = rk.astype(I32)


def _mix_route(lru, att, x2, wo_bf, ln_g, ln_b, w_router, router_bias):
    n, d = x2.shape
    half = lru.shape[1]
    wr_t = w_router.T
    wr_hi = wr_t.astype(BF16)
    wr_lo = (wr_t - wr_hi.astype(F32)).astype(BF16)
    tri = jnp.triu(jnp.ones((MIX_TM, MIX_TM), BF16))
    const = lambda shape: pl.BlockSpec(shape, lambda i: (0,) * len(shape))
    tok = lambda dtype: jax.ShapeDtypeStruct((TOP_K, n), dtype)
    tok_spec = pl.BlockSpec((TOP_K, MIX_TM), lambda i: (0, i))
    return pl.pallas_call(
        _mix_kernel,
        grid=(n // MIX_TM,),
        in_specs=[pl.BlockSpec((MIX_TM, half), lambda i: (i, 0)),
                  pl.BlockSpec((MIX_TM, half), lambda i: (i, 0)),
                  pl.BlockSpec((MIX_TM, d), lambda i: (i, 0)),
                  const((d, d)), const((1, d)), const((1, d)),
                  const((N_EXPERTS, d)), const((N_EXPERTS, d)), const((N_EXPERTS, 1)),
                  const((MIX_TM, MIX_TM))],
        out_specs=[pl.BlockSpec((MIX_TM, d), lambda i: (i, 0)), tok_spec, tok_spec, tok_spec,
                   const((N_EXPERTS, HEAD_DIM))],
        out_shape=[jax.ShapeDtypeStruct((n, d), F32), tok(I32), tok(F32), tok(I32),
                   jax.ShapeDtypeStruct((N_EXPERTS, HEAD_DIM), F32)],
        scratch_shapes=[pltpu.VMEM((N_EXPERTS, HEAD_DIM), F32)],
        compiler_params=_cparams(("arbitrary",)),
        name="mix_ln_route",
    )(lru, att, x2, wo_bf, ln_g.reshape(1, d), ln_b.reshape(1, d), wr_hi, wr_lo,
      router_bias.reshape(N_EXPERTS, 1), tri)


def _dispatch_kernel(start_ref, cnt_ref, idx_ref, rank_ref, x1_hbm, zero_hbm, xs_hbm, sem):
    i = pl.program_id(0)
    tt = idx_ref.shape[1]

    def row_copy(t, k):
        dst = start_ref[idx_ref[k, t]] + rank_ref[k, t]
        return pltpu.make_async_copy(x1_hbm.at[pl.ds(i * tt + t, 1)], xs_hbm.at[pl.ds(dst, 1)], sem)

    def issue(t, c):
        for k in range(TOP_K):
            row_copy(t, k).start()
        return c

    def drain(t, c):
        for k in range(TOP_K):
            row_copy(t, k).wait()
        return c

    lax.fori_loop(0, tt, issue, 0)
    lax.fori_loop(0, tt, drain, 0)

    @pl.when(i == 0)
    def _():
        def pad_copy(r):
            return pltpu.make_async_copy(zero_hbm.at[pl.ds(0, 1)], xs_hbm.at[pl.ds(r, 1)], sem)

        def per_expert(e, c):
            lo = start_ref[e] + cnt_ref[e]
            hi = start_ref[e + 1]
            lax.fori_loop(lo, hi, lambda r, cc: (pad_copy(r).start(), cc)[1], 0)
            lax.fori_loop(lo, hi, lambda r, cc: (pad_copy(r).wait(), cc)[1], 0)
            return c

        lax.fori_loop(0, N_EXPERTS, per_expert, 0)


def _dispatch(x1, idx_t, rank_t, start, counts, total_rows):
    n, d = x1.shape
    smem_tok = pl.BlockSpec((TOP_K, DISPATCH_TT), lambda i, *_: (0, i), memory_space=pltpu.SMEM)
    grid_spec = pltpu.PrefetchScalarGridSpec(
        num_scalar_prefetch=2,
        grid=(n // DISPATCH_TT,),
        in_specs=[smem_tok, smem_tok,
                  pl.BlockSpec(memory_space=pl.ANY), pl.BlockSpec(memory_space=pl.ANY)],
        out_specs=pl.BlockSpec(memory_space=pl.ANY),
        scratch_shapes=[pltpu.SemaphoreType.DMA(())],
    )
    return pl.pallas_call(
        _dispatch_kernel,
        grid_spec=grid_spec,
        out_shape=jax.ShapeDtypeStruct((total_rows, d), x1.dtype),
        compiler_params=_cparams(("arbitrary",)),
        name="dispatch",
    )(start, counts, idx_t, rank_t, x1, jnp.zeros((SUBLANES, d), x1.dtype))


def _expert_kernel(be_ref, used_ref, xs_ref, wg_ref, wu_ref, wd_ref, ys_ref, wg_bf, wu_bf, wd_bf):
    i = pl.program_id(0)
    changed = (i == 0) | (be_ref[i] != be_ref[jnp.maximum(i - 1, 0)])

    @pl.when(changed)
    def _():
        wg_bf[...] = wg_ref[...].astype(BF16)
        wu_bf[...] = wu_ref[...].astype(BF16)
        wd_bf[...] = wd_ref[...].astype(BF16)

    @pl.when(i < used_ref[0])
    def _():
        x = xs_ref[...].astype(BF16)
        g = jnp.dot(x, wg_bf[...], preferred_element_type=F32)
        u = jnp.dot(x, wu_bf[...], preferred_element_type=F32)
        h = (g * jax.nn.sigmoid(g) * u).astype(BF16)
        ys_ref[...] = jnp.dot(h, wd_bf[...], preferred_element_type=F32)

    @pl.when(i >= used_ref[0])
    def _():
        ys_ref[...] = jnp.zeros_like(ys_ref)


def _experts(xs, blk_expert, used, w_gate, w_up, w_down):
    rows, d = xs.shape
    de = w_gate.shape[2]
    nblk = rows // EXPERT_ROWS
    last = lambda i, be, used: jnp.minimum(i, used[0] - 1)
    grid_spec = pltpu.PrefetchScalarGridSpec(
        num_scalar_prefetch=2,
        grid=(nblk,),
        in_specs=[pl.BlockSpec((EXPERT_ROWS, d), lambda i, be, used: (last(i, be, used), 0)),
                  pl.BlockSpec((None, d, de), lambda i, be, used: (be[i], 0, 0)),
                  pl.BlockSpec((None, d, de), lambda i, be, used: (be[i], 0, 0)),
                  pl.BlockSpec((None, de, d), lambda i, be, used: (be[i], 0, 0))],
        out_specs=pl.BlockSpec((EXPERT_ROWS, d), lambda i, be, used: (i, 0)),
        scratch_shapes=[pltpu.VMEM((d, de), BF16), pltpu.VMEM((d, de), BF16),
                        pltpu.VMEM((de, d), BF16)],
    )
    return pl.pallas_call(
        _expert_kernel,
        grid_spec=grid_spec,
        out_shape=jax.ShapeDtypeStruct((rows, d), F32),
        compiler_params=_cparams(("arbitrary",)),
        name="experts",
    )(blk_expert, used, xs, w_gate, w_up, w_down)


def _combine_kernel(start_ref, idx_ref, rank_ref, gate_ref, x1_ref, ys_hbm, wsg_ref, wsu_ref,
                    wsd_ref, g_ref, b_ref, o_ref, buf, sem):
    tt = x1_ref.shape[0]

    def row_copy(t, k):
        src = start_ref[idx_ref[k, t]] + rank_ref[k, t]
        return pltpu.make_async_copy(ys_hbm.at[pl.ds(src, 1)], buf.at[k, pl.ds(t, 1)], sem)

    def issue(t, c):
        for k in range(TOP_K):
            row_copy(t, k).start()
        return c

    def drain(t, c):
        for k in range(TOP_K):
            row_copy(t, k).wait()
        return c

    lax.fori_loop(0, tt, issue, 0)

    x1 = x1_ref[...]
    xb = x1.astype(BF16)
    g = jnp.dot(xb, wsg_ref[...], preferred_element_type=F32)
    u = jnp.dot(xb, wsu_ref[...], preferred_element_type=F32)
    h = (g * jax.nn.sigmoid(g) * u).astype(BF16)
    y = DN_ALPHA * x1 + jnp.dot(h, wsd_ref[...], preferred_element_type=F32)

    lax.fori_loop(0, tt, drain, 0)
    gate = gate_ref[...]
    for k in range(TOP_K):
        y = y + buf[k] * gate[:, k:k + 1]
    o_ref[...] = _layer_norm(y, g_ref[...], b_ref[...])


def _combine(x1, ys, idx_t, rank_t, gate_nk, start, wsg_bf, wsu_bf, wsd_bf, ln_g, ln_b):
    n, d = x1.shape
    de = wsg_bf.shape[1]
    smem_tok = pl.BlockSpec((TOP_K, COMBINE_TT), lambda i, *_: (0, i), memory_space=pltpu.SMEM)
    const = lambda shape: pl.BlockSpec(shape, lambda i, *_: (0,) * len(shape))
    grid_spec = pltpu.PrefetchScalarGridSpec(
        num_scalar_prefetch=1,
        grid=(n // COMBINE_TT,),
        in_specs=[smem_tok, smem_tok,
                  pl.BlockSpec((COMBINE_TT, TOP_K), lambda i, *_: (i, 0)),
                  pl.BlockSpec((COMBINE_TT, d), lambda i, *_: (i, 0)),
                  pl.BlockSpec(memory_space=pl.ANY),
                  const((d, de)), const((d, de)), const((de, d)), const((1, d)), const((1, d))],
        out_specs=pl.BlockSpec((COMBINE_TT, d), lambda i, *_: (i, 0)),
        scratch_shapes=[pltpu.VMEM((TOP_K, COMBINE_TT, d), F32), pltpu.SemaphoreType.DMA(())],
    )
    return pl.pallas_call(
        _combine_kernel,
        grid_spec=grid_spec,
        out_shape=jax.ShapeDtypeStruct((n, d), F32),
        compiler_params=_cparams(("arbitrary",)),
        name="combine_ln",
    )(start, idx_t, rank_t, gate_nk, x1, ys, wsg_bf, wsu_bf, wsd_bf,
      ln_g.reshape(1, d), ln_b.reshape(1, d))


def _block_layout(counts_f, n_assign):
    counts = counts_f.astype(I32)
    padded = (counts + EXPERT_ROWS - 1) // EXPERT_ROWS * EXPERT_ROWS
    end = jnp.cumsum(padded)
    total_rows = n_assign + N_EXPERTS * EXPERT_ROWS
    start = jnp.concatenate([end - padded, jnp.full((1,), total_rows, I32)])
    nblk = total_rows // EXPERT_ROWS
    used = end[-1:] // EXPERT_ROWS
    blk_start = jnp.minimum(jnp.arange(nblk, dtype=I32), used[0] - 1) * EXPERT_ROWS
    blk_expert = jnp.minimum(jnp.searchsorted(end, blk_start, side="right"), N_EXPERTS - 1)
    return start, counts, blk_expert.astype(I32), used.astype(I32), total_rows


def kernel(x, w_in, conv_w, conv_b, w_rg_a, b_rg_a, w_rg_x, b_rg_x, lru_lambda, w_out, ln1_g, ln1_b,
           w_router, router_bias, w_gate, w_up, w_down, ws_gate, ws_up, ws_down, ln2_g, ln2_b):
    b, s, d = x.shape
    n = b * s
    width = d // 2
    for l in range(DEPTH):
        x2 = x.reshape(n, d)
        proj = _proj(x2, w_in[l].astype(BF16), s, width)
        proj3 = proj.reshape(b, s, proj.shape[1])
        lru = _rglru(proj3, conv_w[l], conv_b[l], w_rg_a[l].astype(BF16), b_rg_a[l],
                     w_rg_x[l].astype(BF16), b_rg_x[l], lru_lambda[l], width)
        att = _attention(proj3, width)
        x1, idx_t, gate_t, rank_t, cnt = _mix_route(
            lru.reshape(n, width), att.reshape(n, width), x2, w_out[l].astype(BF16),
            ln1_g[l], ln1_b[l], w_router[l], router_bias[l])
        start, counts, blk_expert, used, total_rows = _block_layout(cnt[:, 0], n * TOP_K)
        xs = _dispatch(x1, idx_t, rank_t, start, counts, total_rows)
        ys = _experts(xs, blk_expert, used, w_gate[l], w_up[l], w_down[l])
        out = _combine(x1, ys, idx_t, rank_t, gate_t.T, start, ws_gate[l].astype(BF16),
                       ws_up[l].astype(BF16), ws_down[l].astype(BF16), ln2_g[l], ln2_b[l])
        x = out.reshape(b, s, d)
    return x
```

```python
import functools

import jax
import jax.numpy as jnp
from jax import lax
from jax.experimental import pallas as pl
from jax.experimental.pallas import tpu as pltpu

F32 = jnp.float32
BF16 = jnp.bfloat16
I32 = jnp.int32

LRU_BLOCKS = 8
CONV_W = 4
LRU_C = 8.0
HEAD_DIM = 128
ROT_DIM = HEAD_DIM // 4
ROPE_THETA = 500000.0
DILATED_PATTERNS = ((128, 1), (512, 4), (2048, 16))
ATTN_BLOCK = 128
N_EXPERTS = 64
N_GROUPS = 8
GROUP_SIZE = N_EXPERTS // N_GROUPS
TOPK_GROUPS = 4
TOP_K = 8
ROUTED_SCALE = 2.5
DEPTH = 1
DN_ALPHA = (2 * DEPTH) ** 0.25
LN_EPS = 1e-5

PROJ_TM = 512
LRU_TC = 256
MIX_TM = 512
ATTN_UNROLL = 4
EXPERT_ROWS = 256
DISPATCH_TT = 256
COMBINE_TT = 128
SUBLANES = 8
VMEM_LIMIT = 56 * 1024 * 1024


def _cparams(sem):
    return pltpu.CompilerParams(dimension_semantics=sem, vmem_limit_bytes=VMEM_LIMIT)


def _proj_kernel(x_ref, w_ref, c_ref, sa_ref, sb_ref, o_ref, *, n_heads):
    j = pl.program_id(0)
    acc = jnp.dot(x_ref[...].astype(BF16), w_ref[...], preferred_element_type=F32)
    is_rope = (j == 2) | (j == 3)

    @pl.when(is_rope)
    def _():
        scale = jnp.where(j == 2, HEAD_DIM ** -0.5, 1.0).astype(F32)
        c = c_ref[...] * scale
        sa = sa_ref[...] * scale
        sb = sb_ref[...] * scale
        half = ROT_DIM // 2
        for h in range(n_heads):
            t = acc[:, h * HEAD_DIM:(h + 1) * HEAD_DIM]
            o_ref[:, h * HEAD_DIM:(h + 1) * HEAD_DIM] = (
                t * c + pltpu.roll(t, half, 1) * sa + pltpu.roll(t, HEAD_DIM - half, 1) * sb)

    @pl.when(jnp.logical_not(is_rope))
    def _():
        o_ref[...] = acc


def _rope_tables(seq):
    half = ROT_DIM // 2
    pos = jnp.arange(seq, dtype=F32)
    inv = ROPE_THETA ** (-jnp.arange(0, ROT_DIM, 2, dtype=F32) / ROT_DIM)
    ang = pos[:, None] * inv[None, :]
    cos, sin = jnp.cos(ang), jnp.sin(ang)
    ones = jnp.ones((seq, HEAD_DIM - ROT_DIM), F32)
    zeros = jnp.zeros((seq, HEAD_DIM - ROT_DIM), F32)
    zh = jnp.zeros((seq, half), F32)
    c = jnp.concatenate([cos, cos, ones], -1)
    sa = jnp.concatenate([zh, sin, zeros], -1)
    sb = jnp.concatenate([-sin, zh, zeros], -1)
    return c, sa, sb


def _proj(x2, w_in_bf, seq, width):
    n, d = x2.shape
    e = w_in_bf.shape[1]
    n_groups = e // width
    c, sa, sb = _rope_tables(seq)
    tiles_per_seq = seq // PROJ_TM
    tab_spec = pl.BlockSpec((PROJ_TM, HEAD_DIM), lambda j, i: (i % tiles_per_seq, 0))
    return pl.pallas_call(
        functools.partial(_proj_kernel, n_heads=width // HEAD_DIM),
        grid=(n_groups, n // PROJ_TM),
        in_specs=[pl.BlockSpec((PROJ_TM, d), lambda j, i: (i, 0)),
                  pl.BlockSpec((d, width), lambda j, i: (0, j)),
                  tab_spec, tab_spec, tab_spec],
        out_specs=pl.BlockSpec((PROJ_TM, width), lambda j, i: (i, j)),
        out_shape=jax.ShapeDtypeStruct((n, e), F32),
        compiler_params=_cparams(("arbitrary", "arbitrary")),
        name="proj_rope",
    )(x2, w_in_bf, c, sa, sb)


def _lru_kernel(xl_ref, gl_ref, cw_ref, cb_ref, wa_ref, ba_ref, wx_ref, bx_ref, lam_ref,
                o_ref, xext_ref, hprev_ref, a_ref, b_ref, h_ref, *, block_w):
    tc, width = xl_ref.shape
    groups = tc // SUBLANES

    @pl.when(pl.program_id(1) == 0)
    def _():
        xext_ref[0:SUBLANES, :] = jnp.zeros((SUBLANES, width), F32)
        hprev_ref[...] = jnp.zeros_like(hprev_ref)

    x = xl_ref[...]
    xext_ref[SUBLANES:SUBLANES + tc, :] = x
    xc = x * cw_ref[CONV_W - 1:CONV_W, :] + cb_ref[...]
    for j in range(CONV_W - 1):
        back = CONV_W - 1 - j
        xc = xc + xext_ref[pl.ds(SUBLANES - back, tc), :] * cw_ref[j:j + 1, :]
    xext_ref[0:SUBLANES, :] = x[tc - SUBLANES:, :]

    lam = lam_ref[...]
    softplus_neg_lam = jnp.maximum(-lam, 0.0) + jnp.log1p(jnp.exp(-jnp.abs(lam)))
    rs, is_ = [], []
    for nb in range(width // block_w):
        xb = xc[:, nb * block_w:(nb + 1) * block_w].astype(BF16)
        rs.append(jnp.dot(xb, wa_ref[nb], preferred_element_type=F32))
        is_.append(jnp.dot(xb, wx_ref[nb], preferred_element_type=F32))
    r = jax.nn.sigmoid(jnp.concatenate(rs, axis=1) + ba_ref[...])
    ig = jax.nn.sigmoid(jnp.concatenate(is_, axis=1) + bx_ref[...])
    a = jnp.exp(-LRU_C * r * softplus_neg_lam)
    b = jnp.sqrt(1.0 - a * a) * (ig * xc)

    row = lax.broadcasted_iota(I32, (tc, width), 0) & (SUBLANES - 1)
    k = 1
    while k < SUBLANES:
        keep = row >= k
        a_sh = pltpu.roll(a, k, 0)
        b_sh = pltpu.roll(b, k, 0)
        b = jnp.where(keep, a * b_sh + b, b)
        a = jnp.where(keep, a * a_sh, a)
        k *= 2
    a_ref[...] = a.reshape(groups, SUBLANES, width)
    b_ref[...] = b.reshape(groups, SUBLANES, width)

    def carry(g, hprev):
        h = a_ref[g] * hprev + b_ref[g]
        h_ref[g] = h
        return h[SUBLANES - 1:SUBLANES, :]

    hprev_ref[...] = lax.fori_loop(0, groups, carry, hprev_ref[...])
    h = h_ref[...].reshape(tc, width)
    o_ref[...] = (h * jax.nn.gelu(gl_ref[...], approximate=True)).astype(o_ref.dtype)


def _rglru(proj3, conv_w, conv_b, wa_bf, b_a, wx_bf, b_x, lam, width):
    b, s, _ = proj3.shape
    block_w = width // LRU_BLOCKS
    vec = pl.BlockSpec((1, width), lambda bi, ci: (0, 0))
    wspec = pl.BlockSpec((LRU_BLOCKS, block_w, block_w), lambda bi, ci: (0, 0, 0))
    groups = LRU_TC // SUBLANES
    return pl.pallas_call(
        functools.partial(_lru_kernel, block_w=block_w),
        grid=(b, s // LRU_TC),
        in_specs=[pl.BlockSpec((None, LRU_TC, width), lambda bi, ci: (bi, ci, 0)),
                  pl.BlockSpec((None, LRU_TC, width), lambda bi, ci: (bi, ci, 1)),
                  pl.BlockSpec((CONV_W, width), lambda bi, ci: (0, 0)),
                  vec, wspec, vec, wspec, vec, vec],
        out_specs=pl.BlockSpec((None, LRU_TC, width), lambda bi, ci: (bi, ci, 0)),
        out_shape=jax.ShapeDtypeStruct((b, s, width), BF16),
        scratch_shapes=[pltpu.VMEM((LRU_TC + SUBLANES, width), F32),
                        pltpu.VMEM((1, width), F32),
                        pltpu.VMEM((groups, SUBLANES, width), F32),
                        pltpu.VMEM((groups, SUBLANES, width), F32),
                        pltpu.VMEM((groups, SUBLANES, width), F32)],
        compiler_params=_cparams(("arbitrary", "arbitrary")),
        name="rglru",
    )(proj3, proj3, conv_w, conv_b.reshape(1, width), wa_bf, b_a.reshape(1, width),
      wx_bf, b_x.reshape(1, width), lam.reshape(1, width))


def _attn_kernel(q_ref, k_ref, v_ref, o_ref, qs, ks, vs, acc_s, m_s, l_s, acc_n, m_n, l_n):
    seq = q_ref.shape[0]
    blk = ATTN_BLOCK
    nblk = seq // blk
    n_pat = len(DILATED_PATTERNS)

    for p, (_, d) in enumerate(DILATED_PATTERNS):
        cls_len = seq // d
        ks[p, 0:blk, :] = jnp.zeros((blk, HEAD_DIM), BF16)
        vs[p, 0:blk, :] = jnp.zeros((blk, HEAD_DIM), BF16)
        for r in range(d):
            src = pl.ds(r, cls_len, stride=d) if d > 1 else pl.ds(0, seq)
            qs[p, r * cls_len:(r + 1) * cls_len, :] = q_ref[src, :].astype(BF16)
            ks[p, blk + r * cls_len:blk + (r + 1) * cls_len, :] = k_ref[src, :].astype(BF16)
            vs[p, blk + r * cls_len:blk + (r + 1) * cls_len, :] = v_ref[src, :].astype(BF16)

    qi = lax.broadcasted_iota(I32, (blk, 2 * blk), 0)
    kj = lax.broadcasted_iota(I32, (blk, 2 * blk), 1)
    band2 = (kj >= qi) & (kj <= qi + blk)
    in_block = kj >= blk
    qi1 = lax.broadcasted_iota(I32, (blk, blk), 0)
    kj1 = lax.broadcasted_iota(I32, (blk, blk), 1)
    causal1 = kj1 <= qi1
    nt = (((1,), (1,)), ((), ()))

    def one_block(p, row0, first):
        q = qs[p, pl.ds(row0, blk), :]
        if first is True:
            kk = ks[p, pl.ds(row0 + blk, blk), :]
            vv = vs[p, pl.ds(row0 + blk, blk), :]
            mask = causal1
        else:
            kk = ks[p, pl.ds(row0, 2 * blk), :]
            vv = vs[p, pl.ds(row0, 2 * blk), :]
            mask = band2 if first is False else band2 & (in_block | jnp.logical_not(first))
        s = lax.dot_general(q, kk, nt, preferred_element_type=F32)
        s = jnp.where(mask, s, -jnp.inf)
        m = jnp.max(s, axis=1, keepdims=True)
        e = jnp.exp(s - m)
        l = jnp.sum(e, axis=1, keepdims=True)
        acc = jnp.dot(e.astype(BF16), vv, preferred_element_type=F32)
        acc_s[p, pl.ds(row0, blk), :] = acc
        m_s[p, pl.ds(row0, blk), :] = jnp.broadcast_to(m, (blk, HEAD_DIM))
        l_s[p, pl.ds(row0, blk), :] = jnp.broadcast_to(l, (blk, HEAD_DIM))

    def body(it, carry):
        for p, (window, d) in enumerate(DILATED_PATTERNS):
            assert window // d == blk
            blocks_per_class = (seq // d) // blk
            for j in range(ATTN_UNROLL):
                row0 = pl.multiple_of((it * ATTN_UNROLL + j) * blk, blk)
                if ATTN_UNROLL % blocks_per_class == 0:
                    first = j % blocks_per_class == 0
                else:
                    assert blocks_per_class % ATTN_UNROLL == 0
                    first = (it % (blocks_per_class // ATTN_UNROLL) == 0) if j == 0 else False
                one_block(p, row0, first)
        return carry

    lax.fori_loop(0, nblk // ATTN_UNROLL, body, 0)

    for p, (_, d) in enumerate(DILATED_PATTERNS):
        cls_len = seq // d
        for r in range(d):
            dst = pl.ds(r, cls_len, stride=d) if d > 1 else pl.ds(0, seq)
            acc_n[p, dst, :] = acc_s[p, r * cls_len:(r + 1) * cls_len, :]
            m_n[p, dst, :] = m_s[p, r * cls_len:(r + 1) * cls_len, :]
            l_n[p, dst, :] = l_s[p, r * cls_len:(r + 1) * cls_len, :]

    mx = m_n[0]
    for p in range(1, n_pat):
        mx = jnp.maximum(mx, m_n[p])
    num = jnp.zeros((seq, HEAD_DIM), F32)
    den = jnp.zeros((seq, HEAD_DIM), F32)
    for p in range(n_pat):
        w = jnp.exp(m_n[p] - mx)
        num = num + w * acc_n[p]
        den = den + w * l_n[p]
    o_ref[...] = (num / den).astype(o_ref.dtype)


def _attention(proj3, width):
    b, s, _ = proj3.shape
    n_heads = width // HEAD_DIM
    n_pat = len(DILATED_PATTERNS)
    col0 = 2 * width // HEAD_DIM

    def spec(group):
        return pl.BlockSpec((None, s, HEAD_DIM),
                            lambda bi, hi: (bi, 0, col0 + group * n_heads + hi))

    return pl.pallas_call(
        _attn_kernel,
        grid=(b, n_heads),
        in_specs=[spec(0), spec(1), spec(2)],
        out_specs=pl.BlockSpec((None, s, HEAD_DIM), lambda bi, hi: (bi, 0, hi)),
        out_shape=jax.ShapeDtypeStruct((b, s, width), BF16),
        scratch_shapes=[pltpu.VMEM((n_pat, s, HEAD_DIM), BF16),
                        pltpu.VMEM((n_pat, s + ATTN_BLOCK, HEAD_DIM), BF16),
                        pltpu.VMEM((n_pat, s + ATTN_BLOCK, HEAD_DIM), BF16)]
                       + [pltpu.VMEM((n_pat, s, HEAD_DIM), F32)] * 6,
        compiler_params=_cparams(("arbitrary", "arbitrary")),
        name="dilated_attention",
    )(proj3, proj3, proj3)


def _layer_norm(y, g, b):
    mu = jnp.mean(y, axis=-1, keepdims=True)
    yc = y - mu
    var = jnp.mean(yc * yc, axis=-1, keepdims=True)
    return yc * lax.rsqrt(var + LN_EPS) * g + b


def _first_index(hit, iota, size):
    return jnp.min(jnp.where(hit, iota, size), axis=0, keepdims=True)


def _mix_kernel(lru_ref, att_ref, x_ref, wo_ref, g_ref, b_ref, wrh_ref, wrl_ref, rb_ref, tri_ref,
                x1_ref, idx_ref, gate_ref, rank_ref, cnt_ref, base_ref):
    tm = x_ref.shape[0]
    half = lru_ref.shape[1]

    @pl.when(pl.program_id(0) == 0)
    def _():
        base_ref[...] = jnp.zeros_like(base_ref)

    mix = jnp.dot(lru_ref[...], wo_ref[0:half, :], preferred_element_type=F32)
    mix = mix + jnp.dot(att_ref[...], wo_ref[half:, :], preferred_element_type=F32)
    x1 = _layer_norm(DN_ALPHA * x_ref[...] + mix, g_ref[...], b_ref[...])
    x1_ref[...] = x1

    x_hi = x1.astype(BF16)
    x_lo = (x1 - x_hi.astype(F32)).astype(BF16)
    nt = (((1,), (1,)), ((), ()))
    logits = lax.dot_general(wrh_ref[...], x_hi, nt, preferred_element_type=F32)
    logits = logits + lax.dot_general(wrh_ref[...], x_lo, nt, preferred_element_type=F32)
    logits = logits + lax.dot_general(wrl_ref[...], x_hi, nt, preferred_element_type=F32)
    scores = jax.nn.sigmoid(logits)
    sel = scores + rb_ref[...]

    neg = -jnp.inf
    gi = lax.broadcasted_iota(I32, (GROUP_SIZE, tm), 0)
    grp_rows = []
    for g in range(N_GROUPS):
        blk = sel[g * GROUP_SIZE:(g + 1) * GROUP_SIZE, :]
        m1 = jnp.max(blk, axis=0, keepdims=True)
        f1 = _first_index(blk == m1, gi, GROUP_SIZE)
        m2 = jnp.max(jnp.where(gi == f1, neg, blk), axis=0, keepdims=True)
        grp_rows.append(m1 + m2)
    ni = lax.broadcasted_iota(I32, (N_GROUPS, tm), 0)
    gscore = jnp.zeros((N_GROUPS, tm), F32)
    for g in range(N_GROUPS):
        gscore = jnp.where(ni == g, grp_rows[g], gscore)
    gmask = jnp.zeros((N_GROUPS, tm), jnp.bool_)
    for _ in range(TOPK_GROUPS):
        m = jnp.max(gscore, axis=0, keepdims=True)
        f = _first_index(gscore == m, ni, N_GROUPS)
        hit = ni == f
        gmask = gmask | hit
        gscore = jnp.where(hit, neg, gscore)
    emask = jnp.concatenate(
        [jnp.broadcast_to(gmask[g:g + 1, :], (GROUP_SIZE, tm)) for g in range(N_GROUPS)], axis=0)
    masked = jnp.where(emask, sel, neg)

    ei = lax.broadcasted_iota(I32, (N_EXPERTS, tm), 0)
    chosen = jnp.zeros((N_EXPERTS, tm), jnp.bool_)
    picks, gates = [], []
    for _ in range(TOP_K):
        m = jnp.max(masked, axis=0, keepdims=True)
        f = _first_index(masked == m, ei, N_EXPERTS)
        hit = ei == f
        picks.append(f)
        gates.append(jnp.sum(jnp.where(hit, scores, 0.0), axis=0, keepdims=True))
        chosen = chosen | hit
        masked = jnp.where(hit, neg, masked)
    gsum = gates[0]
    for k in range(1, TOP_K):
        gsum = gsum + gates[k]

    chosen_f = jnp.where(chosen, 1.0, 0.0)
    incl = jnp.dot(chosen_f.astype(BF16), tri_ref[...], preferred_element_type=F32)
    rank_excl = base_ref[:, 0:1] + incl - chosen_f
    base_ref[...] = base_ref[...] + jnp.sum(chosen_f, axis=1, keepdims=True)
    cnt_ref[...] = base_ref[...]
    for k in range(TOP_K):
        idx_ref[k:k + 1, :] = picks[k]
        gate_ref[k:k + 1, :] = gates[k] / gsum * ROUTED_SCALE
        rk = jnp.sum(jnp.where(ei == picks[k], rank_excl, 0.0), axis=0, keepdims=True)
        rank_ref[k:k + 1, :] = rk.astype(I32)


def _mix_route(lru, att, x2, wo_bf, ln_g, ln_b, w_router, router_bias):
    n, d = x2.shape
    half = lru.shape[1]
    wr_t = w_router.T
    wr_hi = wr_t.astype(BF16)
    wr_lo = (wr_t - wr_hi.astype(F32)).astype(BF16)
    tri = jnp.triu(jnp.ones((MIX_TM, MIX_TM), BF16))
    const = lambda shape: pl.BlockSpec(shape, lambda i: (0,) * len(shape))
    tok = lambda dtype: jax.ShapeDtypeStruct((TOP_K, n), dtype)
    tok_spec = pl.BlockSpec((TOP_K, MIX_TM), lambda i: (0, i))
    return pl.pallas_call(
        _mix_kernel,
        grid=(n // MIX_TM,),
        in_specs=[pl.BlockSpec((MIX_TM, half), lambda i: (i, 0)),
                  pl.BlockSpec((MIX_TM, half), lambda i: (i, 0)),
                  pl.BlockSpec((MIX_TM, d), lambda i: (i, 0)),
                  const((d, d)), const((1, d)), const((1, d)),
                  const((N_EXPERTS, d)), const((N_EXPERTS, d)), const((N_EXPERTS, 1)),
                  const((MIX_TM, MIX_TM))],
        out_specs=[pl.BlockSpec((MIX_TM, d), lambda i: (i, 0)), tok_spec, tok_spec, tok_spec,
                   const((N_EXPERTS, HEAD_DIM))],
        out_shape=[jax.ShapeDtypeStruct((n, d), F32), tok(I32), tok(F32), tok(I32),
                   jax.ShapeDtypeStruct((N_EXPERTS, HEAD_DIM), F32)],
        scratch_shapes=[pltpu.VMEM((N_EXPERTS, HEAD_DIM), F32)],
        compiler_params=_cparams(("arbitrary",)),
        name="mix_ln_route",
    )(lru, att, x2, wo_bf, ln_g.reshape(1, d), ln_b.reshape(1, d), wr_hi, wr_lo,
      router_bias.reshape(N_EXPERTS, 1), tri)


def _dispatch_kernel(start_ref, cnt_ref, idx_ref, rank_ref, x1_ref, zero_ref, xs_hbm, sem):
    i = pl.program_id(0)
    tt = idx_ref.shape[1]

    def row_copy(t, k):
        dst = start_ref[idx_ref[k, t]] + rank_ref[k, t]
        return pltpu.make_async_copy(x1_ref.at[pl.ds(t, 1)], xs_hbm.at[pl.ds(dst, 1)], sem)

    def issue(t, c):
        for k in range(TOP_K):
            row_copy(t, k).start()
        return c

    def drain(t, c):
        for k in range(TOP_K):
            row_copy(t, k).wait()
        return c

    lax.fori_loop(0, tt, issue, 0)
    lax.fori_loop(0, tt, drain, 0)

    @pl.when(i == 0)
    def _():
        def pad_copy(r):
            return pltpu.make_async_copy(zero_ref.at[pl.ds(0, 1)], xs_hbm.at[pl.ds(r, 1)], sem)

        def per_expert(e, c):
            lo = start_ref[e] + cnt_ref[e]
            hi = start_ref[e + 1]
            lax.fori_loop(lo, hi, lambda r, cc: (pad_copy(r).start(), cc)[1], 0)
            lax.fori_loop(lo, hi, lambda r, cc: (pad_copy(r).wait(), cc)[1], 0)
            return c

        lax.fori_loop(0, N_EXPERTS, per_expert, 0)


def _dispatch(x1, idx_t, rank_t, start, counts, total_rows):
    n, d = x1.shape
    smem_tok = pl.BlockSpec((TOP_K, DISPATCH_TT), lambda i, *_: (0, i), memory_space=pltpu.SMEM)
    grid_spec = pltpu.PrefetchScalarGridSpec(
        num_scalar_prefetch=2,
        grid=(n // DISPATCH_TT,),
        in_specs=[smem_tok, smem_tok,
                  pl.BlockSpec((DISPATCH_TT, d), lambda i, *_: (i, 0)),
                  pl.BlockSpec((SUBLANES, d), lambda i, *_: (0, 0))],
        out_specs=pl.BlockSpec(memory_space=pl.ANY),
        scratch_shapes=[pltpu.SemaphoreType.DMA(())],
    )
    return pl.pallas_call(
        _dispatch_kernel,
        grid_spec=grid_spec,
        out_shape=jax.ShapeDtypeStruct((total_rows, d), x1.dtype),
        compiler_params=_cparams(("arbitrary",)),
        name="dispatch",
    )(start, counts, idx_t, rank_t, x1, jnp.zeros((SUBLANES, d), x1.dtype))


def _expert_kernel(be_ref, used_ref, xs_ref, wg_ref, wu_ref, wd_ref, ys_ref, wg_bf, wu_bf, wd_bf):
    i = pl.program_id(0)
    changed = (i == 0) | (be_ref[i] != be_ref[jnp.maximum(i - 1, 0)])

    @pl.when(changed)
    def _():
        wg_bf[...] = wg_ref[...].astype(BF16)
        wu_bf[...] = wu_ref[...].astype(BF16)
        wd_bf[...] = wd_ref[...].astype(BF16)

    @pl.when(i < used_ref[0])
    def _():
        x = xs_ref[...].astype(BF16)
        g = jnp.dot(x, wg_bf[...], preferred_element_type=F32)
        u = jnp.dot(x, wu_bf[...], preferred_element_type=F32)
        h = (g * jax.nn.sigmoid(g) * u).astype(BF16)
        ys_ref[...] = jnp.dot(h, wd_bf[...], preferred_element_type=F32)

    @pl.when(i >= used_ref[0])
    def _():
        ys_ref[...] = jnp.zeros_like(ys_ref)


def _experts(xs, blk_expert, used, w_gate, w_up, w_down):
    rows, d = xs.shape
    de = w_gate.shape[2]
    nblk = rows // EXPERT_ROWS
    last = lambda i, be, used: jnp.minimum(i, used[0] - 1)
    grid_spec = pltpu.PrefetchScalarGridSpec(
        num_scalar_prefetch=2,
        grid=(nblk,),
        in_specs=[pl.BlockSpec((EXPERT_ROWS, d), lambda i, be, used: (last(i, be, used), 0)),
                  pl.BlockSpec((None, d, de), lambda i, be, used: (be[i], 0, 0)),
                  pl.BlockSpec((None, d, de), lambda i, be, used: (be[i], 0, 0)),
                  pl.BlockSpec((None, de, d), lambda i, be, used: (be[i], 0, 0))],
        out_specs=pl.BlockSpec((EXPERT_ROWS, d), lambda i, be, used: (i, 0)),
        scratch_shapes=[pltpu.VMEM((d, de), BF16), pltpu.VMEM((d, de), BF16),
                        pltpu.VMEM((de, d), BF16)],
    )
    return pl.pallas_call(
        _expert_kernel,
        grid_spec=grid_spec,
        out_shape=jax.ShapeDtypeStruct((rows, d), F32),
        compiler_params=_cparams(("arbitrary",)),
        name="experts",
    )(blk_expert, used, xs, w_gate, w_up, w_down)


def _combine_kernel(start_ref, idx_ref, rank_ref, gate_ref, x1_ref, ys_hbm, wsg_ref, wsu_ref,
                    wsd_ref, g_ref, b_ref, o_ref, buf, sem):
    tt = x1_ref.shape[0]

    def row_copy(t, k):
        src = start_ref[idx_ref[k, t]] + rank_ref[k, t]
        return pltpu.make_async_copy(ys_hbm.at[pl.ds(src, 1)], buf.at[k, pl.ds(t, 1)], sem)

    def issue(t, c):
        for k in range(TOP_K):
            row_copy(t, k).start()
        return c

    def drain(t, c):
        for k in range(TOP_K):
            row_copy(t, k).wait()
        return c

    lax.fori_loop(0, tt, issue, 0)

    x1 = x1_ref[...]
    xb = x1.astype(BF16)
    g = jnp.dot(xb, wsg_ref[...], preferred_element_type=F32)
    u = jnp.dot(xb, wsu_ref[...], preferred_element_type=F32)
    h = (g * jax.nn.sigmoid(g) * u).astype(BF16)
    y = DN_ALPHA * x1 + jnp.dot(h, wsd_ref[...], preferred_element_type=F32)

    lax.fori_loop(0, tt, drain, 0)
    gate = gate_ref[...]
    for k in range(TOP_K):
        y = y + buf[k] * gate[:, k:k + 1]
    o_ref[...] = _layer_norm(y, g_ref[...], b_ref[...])


def _combine(x1, ys, idx_t, rank_t, gate_nk, start, wsg_bf, wsu_bf, wsd_bf, ln_g, ln_b):
    n, d = x1.shape
    de = wsg_bf.shape[1]
    smem_tok = pl.BlockSpec((TOP_K, COMBINE_TT), lambda i, *_: (0, i), memory_space=pltpu.SMEM)
    const = lambda shape: pl.BlockSpec(shape, lambda i, *_: (0,) * len(shape))
    grid_spec = pltpu.PrefetchScalarGridSpec(
        num_scalar_prefetch=1,
        grid=(n // COMBINE_TT,),
        in_specs=[smem_tok, smem_tok,
                  pl.BlockSpec((COMBINE_TT, TOP_K), lambda i, *_: (i, 0)),
                  pl.BlockSpec((COMBINE_TT, d), lambda i, *_: (i, 0)),
                  pl.BlockSpec(memory_space=pl.ANY),
                  const((d, de)), const((d, de)), const((de, d)), const((1, d)), const((1, d))],
        out_specs=pl.BlockSpec((COMBINE_TT, d), lambda i, *_: (i, 0)),
        scratch_shapes=[pltpu.VMEM((TOP_K, COMBINE_TT, d), F32), pltpu.SemaphoreType.DMA(())],
    )
    return pl.pallas_call(
        _combine_kernel,
        grid_spec=grid_spec,
        out_shape=jax.ShapeDtypeStruct((n, d), F32),
        compiler_params=_cparams(("arbitrary",)),
        name="combine_ln",
    )(start, idx_t, rank_t, gate_nk, x1, ys, wsg_bf, wsu_bf, wsd_bf,
      ln_g.reshape(1, d), ln_b.reshape(1, d))


def _block_layout(counts_f, n_assign):
    counts = counts_f.astype(I32)
    padded = (counts + EXPERT_ROWS - 1) // EXPERT_ROWS * EXPERT_ROWS
    end = jnp.cumsum(padded)
    total_rows = n_assign + N_EXPERTS * EXPERT_ROWS
    start = jnp.concatenate([end - padded, jnp.full((1,), total_rows, I32)])
    nblk = total_rows // EXPERT_ROWS
    used = end[-1:] // EXPERT_ROWS
    blk_start = jnp.minimum(jnp.arange(nblk, dtype=I32), used[0] - 1) * EXPERT_ROWS
    blk_expert = jnp.sum((end[None, :] <= blk_start[:, None]).astype(I32), axis=1)
    blk_expert = jnp.minimum(blk_expert, N_EXPERTS - 1)
    return start, counts, blk_expert, used.astype(I32), total_rows


def kernel(x, w_in, conv_w, conv_b, w_rg_a, b_rg_a, w_rg_x, b_rg_x, lru_lambda, w_out, ln1_g, ln1_b,
           w_router, router_bias, w_gate, w_up, w_down, ws_gate, ws_up, ws_down, ln2_g, ln2_b):
    b, s, d = x.shape
    n = b * s
    width = d // 2
    for l in range(DEPTH):
        x2 = x.reshape(n, d)
        proj = _proj(x2, w_in[l].astype(BF16), s, width)
        proj3 = proj.reshape(b, s, proj.shape[1])
        lru = _rglru(proj3, conv_w[l], conv_b[l], w_rg_a[l].astype(BF16), b_rg_a[l],
                     w_rg_x[l].astype(BF16), b_rg_x[l], lru_lambda[l], width)
        att = _attention(proj3, width)
        x1, idx_t, gate_t, rank_t, cnt = _mix_route(
            lru.reshape(n, width), att.reshape(n, width), x2, w_out[l].astype(BF16),
            ln1_g[l], ln1_b[l], w_router[l], router_bias[l])
        start, counts, blk_expert, used, total_rows = _block_layout(cnt[:, 0], n * TOP_K)
        xs = _dispatch(x1, idx_t, rank_t, start, counts, total_rows)
        ys = _experts(xs, blk_expert, used, w_gate[l], w_up[l], w_down[l])
        out = _combine(x1, ys, idx_t, rank_t, gate_t.T, start, ws_gate[l].astype(BF16),
                       ws_up[l].astype(BF16), ws_down[l].astype(BF16), ln2_g[l], ln2_b[l])
        x = out.reshape(b, s, d)
    return x
```

```python
import functools

import jax
import jax.numpy as jnp
from jax import lax
from jax.experimental import pallas as pl
from jax.experimental.pallas import tpu as pltpu

F32 = jnp.float32
BF16 = jnp.bfloat16
I32 = jnp.int32
U32 = jnp.uint32

LRU_BLOCKS = 8
CONV_W = 4
LRU_C = 8.0
HEAD_DIM = 128
ROT_DIM = HEAD_DIM // 4
ROPE_THETA = 500000.0
DILATED_PATTERNS = ((128, 1), (512, 4), (2048, 16))
ATTN_BLOCK = 128
N_EXPERTS = 64
N_GROUPS = 8
GROUP_SIZE = N_EXPERTS // N_GROUPS
TOPK_GROUPS = 4
TOP_K = 8
ROUTED_SCALE = 2.5
DEPTH = 1
DN_ALPHA = (2 * DEPTH) ** 0.25
LN_EPS = 1e-5

PROJ_TM = 512
LRU_TC = 256
MIX_TM = 512
ATTN_UNROLL = 4
EXPERT_ROWS = 256
DISPATCH_TT = 256
COMBINE_TT = 128
SUBLANES = 8
LANES = 128
TILE_WORDS = SUBLANES * LANES
VMEM_LIMIT = 56 * 1024 * 1024


def _pack_bf16_pair(lo, hi):
    lo_bits = lax.bitcast_convert_type(lo.astype(BF16).astype(F32), U32)
    hi_bits = lax.bitcast_convert_type(hi.astype(BF16).astype(F32), U32)
    return (lo_bits >> 16) | (hi_bits & jnp.uint32(0xFFFF0000))


def _unpack_bf16_pair(packed):
    lo = lax.bitcast_convert_type(packed << 16, F32)
    hi = lax.bitcast_convert_type(packed & jnp.uint32(0xFFFF0000), F32)
    return lo, hi


def _store_rows_as_tiles(ref, words):
    rows = words.shape[0]
    for s in range(SUBLANES):
        ref[pl.ds(s, rows, stride=SUBLANES), :] = words[:, s * LANES:(s + 1) * LANES]


def _load_rows_from_tiles(ref, row0, rows):
    return jnp.concatenate(
        [ref[pl.ds(row0 * SUBLANES + s, rows, stride=SUBLANES), :] for s in range(SUBLANES)], axis=1)


def _cparams(sem):
    return pltpu.CompilerParams(dimension_semantics=sem, vmem_limit_bytes=VMEM_LIMIT)


def _proj_kernel(x_ref, w_ref, c_ref, sa_ref, sb_ref, o_ref, *, n_heads):
    j = pl.program_id(0)
    acc = jnp.dot(x_ref[...].astype(BF16), w_ref[...], preferred_element_type=F32)
    is_rope = (j == 2) | (j == 3)

    @pl.when(is_rope)
    def _():
        scale = jnp.where(j == 2, HEAD_DIM ** -0.5, 1.0).astype(F32)
        c = c_ref[...] * scale
        sa = sa_ref[...] * scale
        sb = sb_ref[...] * scale
        half = ROT_DIM // 2
        for h in range(n_heads):
            t = acc[:, h * HEAD_DIM:(h + 1) * HEAD_DIM]
            o_ref[:, h * HEAD_DIM:(h + 1) * HEAD_DIM] = (
                t * c + pltpu.roll(t, half, 1) * sa + pltpu.roll(t, HEAD_DIM - half, 1) * sb)

    @pl.when(jnp.logical_not(is_rope))
    def _():
        o_ref[...] = acc


def _rope_tables(seq):
    half = ROT_DIM // 2
    pos = jnp.arange(seq, dtype=F32)
    inv = ROPE_THETA ** (-jnp.arange(0, ROT_DIM, 2, dtype=F32) / ROT_DIM)
    ang = pos[:, None] * inv[None, :]
    cos, sin = jnp.cos(ang), jnp.sin(ang)
    ones = jnp.ones((seq, HEAD_DIM - ROT_DIM), F32)
    zeros = jnp.zeros((seq, HEAD_DIM - ROT_DIM), F32)
    zh = jnp.zeros((seq, half), F32)
    c = jnp.concatenate([cos, cos, ones], -1)
    sa = jnp.concatenate([zh, sin, zeros], -1)
    sb = jnp.concatenate([-sin, zh, zeros], -1)
    return c, sa, sb


def _proj(x2, w_in_bf, seq, width):
    n, d = x2.shape
    e = w_in_bf.shape[1]
    n_groups = e // width
    c, sa, sb = _rope_tables(seq)
    tiles_per_seq = seq // PROJ_TM
    tab_spec = pl.BlockSpec((PROJ_TM, HEAD_DIM), lambda j, i: (i % tiles_per_seq, 0))
    return pl.pallas_call(
        functools.partial(_proj_kernel, n_heads=width // HEAD_DIM),
        grid=(n_groups, n // PROJ_TM),
        in_specs=[pl.BlockSpec((PROJ_TM, d), lambda j, i: (i, 0)),
                  pl.BlockSpec((d, width), lambda j, i: (0, j)),
                  tab_spec, tab_spec, tab_spec],
        out_specs=pl.BlockSpec((PROJ_TM, width), lambda j, i: (i, j)),
        out_shape=jax.ShapeDtypeStruct((n, e), F32),
        compiler_params=_cparams(("arbitrary", "arbitrary")),
        name="proj_rope",
    )(x2, w_in_bf, c, sa, sb)


def _lru_kernel(xl_ref, gl_ref, cw_ref, cb_ref, wa_ref, ba_ref, wx_ref, bx_ref, lam_ref,
                o_ref, xext_ref, hprev_ref, a_ref, b_ref, h_ref, *, block_w):
    tc, width = xl_ref.shape
    groups = tc // SUBLANES

    @pl.when(pl.program_id(1) == 0)
    def _():
        xext_ref[0:SUBLANES, :] = jnp.zeros((SUBLANES, width), F32)
        hprev_ref[...] = jnp.zeros_like(hprev_ref)

    x = xl_ref[...]
    xext_ref[SUBLANES:SUBLANES + tc, :] = x
    xc = x * cw_ref[CONV_W - 1:CONV_W, :] + cb_ref[...]
    for j in range(CONV_W - 1):
        back = CONV_W - 1 - j
        xc = xc + xext_ref[pl.ds(SUBLANES - back, tc), :] * cw_ref[j:j + 1, :]
    xext_ref[0:SUBLANES, :] = x[tc - SUBLANES:, :]

    lam = lam_ref[...]
    softplus_neg_lam = jnp.maximum(-lam, 0.0) + jnp.log1p(jnp.exp(-jnp.abs(lam)))
    rs, is_ = [], []
    for nb in range(width // block_w):
        xb = xc[:, nb * block_w:(nb + 1) * block_w].astype(BF16)
        rs.append(jnp.dot(xb, wa_ref[nb], preferred_element_type=F32))
        is_.append(jnp.dot(xb, wx_ref[nb], preferred_element_type=F32))
    r = jax.nn.sigmoid(jnp.concatenate(rs, axis=1) + ba_ref[...])
    ig = jax.nn.sigmoid(jnp.concatenate(is_, axis=1) + bx_ref[...])
    a = jnp.exp(-LRU_C * r * softplus_neg_lam)
    b = jnp.sqrt(1.0 - a * a) * (ig * xc)

    row = lax.broadcasted_iota(I32, (tc, width), 0) & (SUBLANES - 1)
    k = 1
    while k < SUBLANES:
        keep = row >= k
        a_sh = pltpu.roll(a, k, 0)
        b_sh = pltpu.roll(b, k, 0)
        b = jnp.where(keep, a * b_sh + b, b)
        a = jnp.where(keep, a * a_sh, a)
        k *= 2
    a_ref[...] = a.reshape(groups, SUBLANES, width)
    b_ref[...] = b.reshape(groups, SUBLANES, width)

    def carry(g, hprev):
        h = a_ref[g] * hprev + b_ref[g]
        h_ref[g] = h
        return h[SUBLANES - 1:SUBLANES, :]

    hprev_ref[...] = lax.fori_loop(0, groups, carry, hprev_ref[...])
    h = h_ref[...].reshape(tc, width)
    o_ref[...] = (h * jax.nn.gelu(gl_ref[...], approximate=True)).astype(o_ref.dtype)


def _rglru(proj3, conv_w, conv_b, wa_bf, b_a, wx_bf, b_x, lam, width):
    b, s, _ = proj3.shape
    block_w = width // LRU_BLOCKS
    vec = pl.BlockSpec((1, width), lambda bi, ci: (0, 0))
    wspec = pl.BlockSpec((LRU_BLOCKS, block_w, block_w), lambda bi, ci: (0, 0, 0))
    groups = LRU_TC // SUBLANES
    return pl.pallas_call(
        functools.partial(_lru_kernel, block_w=block_w),
        grid=(b, s // LRU_TC),
        in_specs=[pl.BlockSpec((None, LRU_TC, width), lambda bi, ci: (bi, ci, 0)),
                  pl.BlockSpec((None, LRU_TC, width), lambda bi, ci: (bi, ci, 1)),
                  pl.BlockSpec((CONV_W, width), lambda bi, ci: (0, 0)),
                  vec, wspec, vec, wspec, vec, vec],
        out_specs=pl.BlockSpec((None, LRU_TC, width), lambda bi, ci: (bi, ci, 0)),
        out_shape=jax.ShapeDtypeStruct((b, s, width), BF16),
        scratch_shapes=[pltpu.VMEM((LRU_TC + SUBLANES, width), F32),
                        pltpu.VMEM((1, width), F32),
                        pltpu.VMEM((groups, SUBLANES, width), F32),
                        pltpu.VMEM((groups, SUBLANES, width), F32),
                        pltpu.VMEM((groups, SUBLANES, width), F32)],
        compiler_params=_cparams(("arbitrary", "arbitrary")),
        name="rglru",
    )(proj3, proj3, conv_w, conv_b.reshape(1, width), wa_bf, b_a.reshape(1, width),
      wx_bf, b_x.reshape(1, width), lam.reshape(1, width))


def _attn_kernel(q_ref, k_ref, v_ref, o_ref, qs, ks, vs, acc_s, m_s, l_s, acc_n, m_n, l_n):
    seq = q_ref.shape[0]
    blk = ATTN_BLOCK
    nblk = seq // blk
    n_pat = len(DILATED_PATTERNS)

    for p, (_, d) in enumerate(DILATED_PATTERNS):
        cls_len = seq // d
        ks[p, 0:blk, :] = jnp.zeros((blk, HEAD_DIM), BF16)
        vs[p, 0:blk, :] = jnp.zeros((blk, HEAD_DIM), BF16)
        for r in range(d):
            src = pl.ds(r, cls_len, stride=d) if d > 1 else pl.ds(0, seq)
            qs[p, r * cls_len:(r + 1) * cls_len, :] = q_ref[src, :].astype(BF16)
            ks[p, blk + r * cls_len:blk + (r + 1) * cls_len, :] = k_ref[src, :].astype(BF16)
            vs[p, blk + r * cls_len:blk + (r + 1) * cls_len, :] = v_ref[src, :].astype(BF16)

    qi = lax.broadcasted_iota(I32, (blk, 2 * blk), 0)
    kj = lax.broadcasted_iota(I32, (blk, 2 * blk), 1)
    band2 = (kj >= qi) & (kj <= qi + blk)
    in_block = kj >= blk
    qi1 = lax.broadcasted_iota(I32, (blk, blk), 0)
    kj1 = lax.broadcasted_iota(I32, (blk, blk), 1)
    causal1 = kj1 <= qi1
    nt = (((1,), (1,)), ((), ()))

    def one_block(p, row0, first):
        q = qs[p, pl.ds(row0, blk), :]
        if first is True:
            kk = ks[p, pl.ds(row0 + blk, blk), :]
            vv = vs[p, pl.ds(row0 + blk, blk), :]
            mask = causal1
        else:
            kk = ks[p, pl.ds(row0, 2 * blk), :]
            vv = vs[p, pl.ds(row0, 2 * blk), :]
            mask = band2 if first is False else band2 & (in_block | jnp.logical_not(first))
        s = lax.dot_general(q, kk, nt, preferred_element_type=F32)
        s = jnp.where(mask, s, -jnp.inf)
        m = jnp.max(s, axis=1, keepdims=True)
        e = jnp.exp(s - m)
        l = jnp.sum(e, axis=1, keepdims=True)
        acc = jnp.dot(e.astype(BF16), vv, preferred_element_type=F32)
        acc_s[p, pl.ds(row0, blk), :] = acc
        m_s[p, pl.ds(row0, blk), :] = jnp.broadcast_to(m, (blk, HEAD_DIM))
        l_s[p, pl.ds(row0, blk), :] = jnp.broadcast_to(l, (blk, HEAD_DIM))

    def body(it, carry):
        for p, (window, d) in enumerate(DILATED_PATTERNS):
            assert window // d == blk
            blocks_per_class = (seq // d) // blk
            for j in range(ATTN_UNROLL):
                row0 = pl.multiple_of((it * ATTN_UNROLL + j) * blk, blk)
                if ATTN_UNROLL % blocks_per_class == 0:
                    first = j % blocks_per_class == 0
                else:
                    assert blocks_per_class % ATTN_UNROLL == 0
                    first = (it % (blocks_per_class // ATTN_UNROLL) == 0) if j == 0 else False
                one_block(p, row0, first)
        return carry

    lax.fori_loop(0, nblk // ATTN_UNROLL, body, 0)

    for p, (_, d) in enumerate(DILATED_PATTERNS):
        cls_len = seq // d
        for r in range(d):
            dst = pl.ds(r, cls_len, stride=d) if d > 1 else pl.ds(0, seq)
            acc_n[p, dst, :] = acc_s[p, r * cls_len:(r + 1) * cls_len, :]
            m_n[p, dst, :] = m_s[p, r * cls_len:(r + 1) * cls_len, :]
            l_n[p, dst, :] = l_s[p, r * cls_len:(r + 1) * cls_len, :]

    mx = m_n[0]
    for p in range(1, n_pat):
        mx = jnp.maximum(mx, m_n[p])
    num = jnp.zeros((seq, HEAD_DIM), F32)
    den = jnp.zeros((seq, HEAD_DIM), F32)
    for p in range(n_pat):
        w = jnp.exp(m_n[p] - mx)
        num = num + w * acc_n[p]
        den = den + w * l_n[p]
    o_ref[...] = (num / den).astype(o_ref.dtype)


def _attention(proj3, width):
    b, s, _ = proj3.shape
    n_heads = width // HEAD_DIM
    n_pat = len(DILATED_PATTERNS)
    col0 = 2 * width // HEAD_DIM

    def spec(group):
        return pl.BlockSpec((None, s, HEAD_DIM),
                            lambda bi, hi: (bi, 0, col0 + group * n_heads + hi))

    return pl.pallas_call(
        _attn_kernel,
        grid=(b, n_heads),
        in_specs=[spec(0), spec(1), spec(2)],
        out_specs=pl.BlockSpec((None, s, HEAD_DIM), lambda bi, hi: (bi, 0, hi)),
        out_shape=jax.ShapeDtypeStruct((b, s, width), BF16),
        scratch_shapes=[pltpu.VMEM((n_pat, s, HEAD_DIM), BF16),
                        pltpu.VMEM((n_pat, s + ATTN_BLOCK, HEAD_DIM), BF16),
                        pltpu.VMEM((n_pat, s + ATTN_BLOCK, HEAD_DIM), BF16)]
                       + [pltpu.VMEM((n_pat, s, HEAD_DIM), F32)] * 6,
        compiler_params=_cparams(("arbitrary", "arbitrary")),
        name="dilated_attention",
    )(proj3, proj3, proj3)


def _layer_norm(y, g, b):
    mu = jnp.mean(y, axis=-1, keepdims=True)
    yc = y - mu
    var = jnp.mean(yc * yc, axis=-1, keepdims=True)
    return yc * lax.rsqrt(var + LN_EPS) * g + b


def _first_index(hit, iota, size):
    return jnp.min(jnp.where(hit, iota, size), axis=0, keepdims=True)


def _mix_kernel(lru_ref, att_ref, x_ref, wo_ref, g_ref, b_ref, wrh_ref, wrl_ref, rb_ref, tri_ref,
                x1_ref, x1p_ref, idx_ref, gate_ref, rank_ref, cnt_ref, base_ref):
    tm = x_ref.shape[0]
    half = lru_ref.shape[1]

    @pl.when(pl.program_id(0) == 0)
    def _():
        base_ref[...] = jnp.zeros_like(base_ref)

    mix = jnp.dot(lru_ref[...], wo_ref[0:half, :], preferred_element_type=F32)
    mix = mix + jnp.dot(att_ref[...], wo_ref[half:, :], preferred_element_type=F32)
    x1 = _layer_norm(DN_ALPHA * x_ref[...] + mix, g_ref[...], b_ref[...])
    x1_ref[...] = x1
    _store_rows_as_tiles(x1p_ref, _pack_bf16_pair(x1[:, 0:half], x1[:, half:]))

    x_hi = x1.astype(BF16)
    x_lo = (x1 - x_hi.astype(F32)).astype(BF16)
    nt = (((1,), (1,)), ((), ()))
    logits = lax.dot_general(wrh_ref[...], x_hi, nt, preferred_element_type=F32)
    logits = logits + lax.dot_general(wrh_ref[...], x_lo, nt, preferred_element_type=F32)
    logits = logits + lax.dot_general(wrl_ref[...], x_hi, nt, preferred_element_type=F32)
    scores = jax.nn.sigmoid(logits)
    sel = scores + rb_ref[...]

    neg = -jnp.inf
    gi = lax.broadcasted_iota(I32, (GROUP_SIZE, tm), 0)
    grp_rows = []
    for g in range(N_GROUPS):
        blk = sel[g * GROUP_SIZE:(g + 1) * GROUP_SIZE, :]
        m1 = jnp.max(blk, axis=0, keepdims=True)
        f1 = _first_index(blk == m1, gi, GROUP_SIZE)
        m2 = jnp.max(jnp.where(gi == f1, neg, blk), axis=0, keepdims=True)
        grp_rows.append(m1 + m2)
    ni = lax.broadcasted_iota(I32, (N_GROUPS, tm), 0)
    gscore = jnp.zeros((N_GROUPS, tm), F32)
    for g in range(N_GROUPS):
        gscore = jnp.where(ni == g, grp_rows[g], gscore)
    gmask = jnp.zeros((N_GROUPS, tm), jnp.bool_)
    for _ in range(TOPK_GROUPS):
        m = jnp.max(gscore, axis=0, keepdims=True)
        f = _first_index(gscore == m, ni, N_GROUPS)
        hit = ni == f
        gmask = gmask | hit
        gscore = jnp.where(hit, neg, gscore)
    emask = jnp.concatenate(
        [jnp.broadcast_to(gmask[g:g + 1, :], (GROUP_SIZE, tm)) for g in range(N_GROUPS)], axis=0)
    masked = jnp.where(emask, sel, neg)

    ei = lax.broadcasted_iota(I32, (N_EXPERTS, tm), 0)
    chosen = jnp.zeros((N_EXPERTS, tm), jnp.bool_)
    picks, gates = [], []
    for _ in range(TOP_K):
        m = jnp.max(masked, axis=0, keepdims=True)
        f = _first_index(masked == m, ei, N_EXPERTS)
        hit = ei == f
        picks.append(f)
        gates.append(jnp.sum(jnp.where(hit, scores, 0.0), axis=0, keepdims=True))
        chosen = chosen | hit
        masked = jnp.where(hit, neg, masked)
    gsum = gates[0]
    for k in range(1, TOP_K):
        gsum = gsum + gates[k]

    chosen_f = jnp.where(chosen, 1.0, 0.0)
    incl = jnp.dot(chosen_f.astype(BF16), tri_ref[...], preferred_element_type=F32)
    rank_excl = base_ref[:, 0:1] + incl - chosen_f
    base_ref[...] = base_ref[...] + jnp.sum(chosen_f, axis=1, keepdims=True)
    cnt_ref[...] = base_ref[...]
    for k in range(TOP_K):
        idx_ref[k:k + 1, :] = picks[k]
        gate_ref[k:k + 1, :] = gates[k] / gsum * ROUTED_SCALE
        rk = jnp.sum(jnp.where(ei == picks[k], rank_excl, 0.0), axis=0, keepdims=True)
        rank_ref[k:k + 1, :] = rk.astype(I32)


def _mix_route(lru, att, x2, wo_bf, ln_g, ln_b, w_router, router_bias):
    n, d = x2.shape
    half = lru.shape[1]
    wr_t = w_router.T
    wr_hi = wr_t.astype(BF16)
    wr_lo = (wr_t - wr_hi.astype(F32)).astype(BF16)
    assert half == TILE_WORDS
    tri = jnp.triu(jnp.ones((MIX_TM, MIX_TM), BF16))
    const = lambda shape: pl.BlockSpec(shape, lambda i: (0,) * len(shape))
    tok = lambda dtype: jax.ShapeDtypeStruct((TOP_K, n), dtype)
    tok_spec = pl.BlockSpec((TOP_K, MIX_TM), lambda i: (0, i))
    return pl.pallas_call(
        _mix_kernel,
        grid=(n // MIX_TM,),
        in_specs=[pl.BlockSpec((MIX_TM, half), lambda i: (i, 0)),
                  pl.BlockSpec((MIX_TM, half), lambda i: (i, 0)),
                  pl.BlockSpec((MIX_TM, d), lambda i: (i, 0)),
                  const((d, d)), const((1, d)), const((1, d)),
                  const((N_EXPERTS, d)), const((N_EXPERTS, d)), const((N_EXPERTS, 1)),
                  const((MIX_TM, MIX_TM))],
        out_specs=[pl.BlockSpec((MIX_TM, d), lambda i: (i, 0)),
                   pl.BlockSpec((MIX_TM * SUBLANES, LANES), lambda i: (i, 0)),
                   tok_spec, tok_spec, tok_spec, const((N_EXPERTS, HEAD_DIM))],
        out_shape=[jax.ShapeDtypeStruct((n, d), F32), jax.ShapeDtypeStruct((n * SUBLANES, LANES), U32),
                   tok(I32), tok(F32), tok(I32),
                   jax.ShapeDtypeStruct((N_EXPERTS, HEAD_DIM), F32)],
        scratch_shapes=[pltpu.VMEM((N_EXPERTS, HEAD_DIM), F32)],
        compiler_params=_cparams(("arbitrary",)),
        name="mix_ln_route",
    )(lru, att, x2, wo_bf, ln_g.reshape(1, d), ln_b.reshape(1, d), wr_hi, wr_lo,
      router_bias.reshape(N_EXPERTS, 1), tri)


def _dispatch_kernel(start_ref, cnt_ref, pos_ref, x1_ref, zero_ref, xs_hbm, sem):
    i = pl.program_id(0)
    tt = pos_ref.shape[1]

    def row_copy(t, k):
        return pltpu.make_async_copy(x1_ref.at[t], xs_hbm.at[pos_ref[k, t]], sem)

    def issue(t, c):
        for k in range(TOP_K):
            row_copy(t, k).start()
        return c

    def drain(t, c):
        for k in range(TOP_K):
            row_copy(t, k).wait()
        return c

    lax.fori_loop(0, tt, issue, 0)
    lax.fori_loop(0, tt, drain, 0)

    @pl.when(i == 0)
    def _():
        def pad_copy(r):
            return pltpu.make_async_copy(zero_ref.at[0], xs_hbm.at[r], sem)

        def per_expert(e, c):
            lo = start_ref[e] + cnt_ref[e]
            hi = start_ref[e + 1]
            lax.fori_loop(lo, hi, lambda r, cc: (pad_copy(r).start(), cc)[1], 0)
            lax.fori_loop(lo, hi, lambda r, cc: (pad_copy(r).wait(), cc)[1], 0)
            return c

        lax.fori_loop(0, N_EXPERTS, per_expert, 0)


def _dispatch(x1p, pos_t, start, counts, total_rows):
    n = x1p.shape[0]
    smem_tok = pl.BlockSpec((TOP_K, DISPATCH_TT), lambda i, *_: (0, i), memory_space=pltpu.SMEM)
    grid_spec = pltpu.PrefetchScalarGridSpec(
        num_scalar_prefetch=2,
        grid=(n // DISPATCH_TT,),
        in_specs=[smem_tok,
                  pl.BlockSpec((DISPATCH_TT, SUBLANES, LANES), lambda i, *_: (i, 0, 0)),
                  pl.BlockSpec((1, SUBLANES, LANES), lambda i, *_: (0, 0, 0))],
        out_specs=pl.BlockSpec(memory_space=pl.ANY),
        scratch_shapes=[pltpu.SemaphoreType.DMA(())],
    )
    return pl.pallas_call(
        _dispatch_kernel,
        grid_spec=grid_spec,
        out_shape=jax.ShapeDtypeStruct((total_rows, SUBLANES, LANES), x1p.dtype),
        compiler_params=_cparams(("arbitrary",)),
        name="dispatch",
    )(start, counts, pos_t, x1p, jnp.zeros((1, SUBLANES, LANES), x1p.dtype))


def _expert_kernel(be_ref, used_ref, xs_ref, wg_ref, wu_ref, wd_ref, ys_ref, wg_bf, wu_bf, wd_bf):
    i = pl.program_id(0)
    changed = (i == 0) | (be_ref[i] != be_ref[jnp.maximum(i - 1, 0)])

    @pl.when(changed)
    def _():
        wg_bf[...] = wg_ref[...].astype(BF16)
        wu_bf[...] = wu_ref[...].astype(BF16)
        wd_bf[...] = wd_ref[...].astype(BF16)

    @pl.when(i < used_ref[0])
    def _():
        half = TILE_WORDS
        lo, hi = _unpack_bf16_pair(_load_rows_from_tiles(xs_ref, 0, EXPERT_ROWS))
        lo, hi = lo.astype(BF16), hi.astype(BF16)
        g = (jnp.dot(lo, wg_bf[0:half, :], preferred_element_type=F32)
             + jnp.dot(hi, wg_bf[half:, :], preferred_element_type=F32))
        u = (jnp.dot(lo, wu_bf[0:half, :], preferred_element_type=F32)
             + jnp.dot(hi, wu_bf[half:, :], preferred_element_type=F32))
        h = (g * jax.nn.sigmoid(g) * u).astype(BF16)
        y = jnp.dot(h, wd_bf[...], preferred_element_type=F32)
        _store_rows_as_tiles(ys_ref, _pack_bf16_pair(y[:, 0:half], y[:, half:]))

    @pl.when(i >= used_ref[0])
    def _():
        ys_ref[...] = jnp.zeros_like(ys_ref)


def _experts(xs, blk_expert, used, w_gate, w_up, w_down):
    rows = xs.shape[0] // SUBLANES
    d = 2 * TILE_WORDS
    de = w_gate.shape[2]
    nblk = rows // EXPERT_ROWS
    last = lambda i, be, used: jnp.minimum(i, used[0] - 1)
    grid_spec = pltpu.PrefetchScalarGridSpec(
        num_scalar_prefetch=2,
        grid=(nblk,),
        in_specs=[pl.BlockSpec((EXPERT_ROWS * SUBLANES, LANES),
                               lambda i, be, used: (last(i, be, used), 0)),
                  pl.BlockSpec((None, d, de), lambda i, be, used: (be[i], 0, 0)),
                  pl.BlockSpec((None, d, de), lambda i, be, used: (be[i], 0, 0)),
                  pl.BlockSpec((None, de, d), lambda i, be, used: (be[i], 0, 0))],
        out_specs=pl.BlockSpec((EXPERT_ROWS * SUBLANES, LANES), lambda i, be, used: (i, 0)),
        scratch_shapes=[pltpu.VMEM((d, de), BF16), pltpu.VMEM((d, de), BF16),
                        pltpu.VMEM((de, d), BF16)],
    )
    return pl.pallas_call(
        _expert_kernel,
        grid_spec=grid_spec,
        out_shape=jax.ShapeDtypeStruct((rows * SUBLANES, LANES), U32),
        compiler_params=_cparams(("arbitrary",)),
        name="experts",
    )(blk_expert, used, xs, w_gate, w_up, w_down)


def _combine_kernel(pos_ref, gate_ref, x1_ref, ys_hbm, wsg_ref, wsu_ref,
                    wsd_ref, g_ref, b_ref, o_ref, buf, sem):
    tt = x1_ref.shape[0]

    def row_copy(t, k):
        dst = pl.multiple_of((k * tt + t) * SUBLANES, SUBLANES)
        return pltpu.make_async_copy(ys_hbm.at[pos_ref[k, t]], buf.at[pl.ds(dst, SUBLANES)], sem)

    def issue(t, c):
        for k in range(TOP_K):
            row_copy(t, k).start()
        return c

    def drain(t, c):
        for k in range(TOP_K):
            row_copy(t, k).wait()
        return c

    lax.fori_loop(0, tt, issue, 0)

    x1 = x1_ref[...]
    xb = x1.astype(BF16)
    g = jnp.dot(xb, wsg_ref[...], preferred_element_type=F32)
    u = jnp.dot(xb, wsu_ref[...], preferred_element_type=F32)
    h = (g * jax.nn.sigmoid(g) * u).astype(BF16)
    y = DN_ALPHA * x1 + jnp.dot(h, wsd_ref[...], preferred_element_type=F32)

    lax.fori_loop(0, tt, drain, 0)
    gate = gate_ref[...]
    r_lo = jnp.zeros((tt, TILE_WORDS), F32)
    r_hi = jnp.zeros((tt, TILE_WORDS), F32)
    for k in range(TOP_K):
        lo, hi = _unpack_bf16_pair(_load_rows_from_tiles(buf, k * tt, tt))
        r_lo = r_lo + lo * gate[:, k:k + 1]
        r_hi = r_hi + hi * gate[:, k:k + 1]
    y = y + jnp.concatenate([r_lo, r_hi], axis=1)
    o_ref[...] = _layer_norm(y, g_ref[...], b_ref[...])


def _combine(x1, ys, pos_t, gate_nk, wsg_bf, wsu_bf, wsd_bf, ln_g, ln_b):
    n, d = x1.shape
    de = wsg_bf.shape[1]
    smem_tok = pl.BlockSpec((TOP_K, COMBINE_TT), lambda i, *_: (0, i), memory_space=pltpu.SMEM)
    const = lambda shape: pl.BlockSpec(shape, lambda i, *_: (0,) * len(shape))
    grid_spec = pltpu.PrefetchScalarGridSpec(
        num_scalar_prefetch=0,
        grid=(n // COMBINE_TT,),
        in_specs=[smem_tok,
                  pl.BlockSpec((COMBINE_TT, TOP_K), lambda i, *_: (i, 0)),
                  pl.BlockSpec((COMBINE_TT, d), lambda i, *_: (i, 0)),
                  pl.BlockSpec(memory_space=pl.ANY),
                  const((d, de)), const((d, de)), const((de, d)), const((1, d)), const((1, d))],
        out_specs=pl.BlockSpec((COMBINE_TT, d), lambda i, *_: (i, 0)),
        scratch_shapes=[pltpu.VMEM((TOP_K * COMBINE_TT * SUBLANES, LANES), U32),
                        pltpu.SemaphoreType.DMA(())],
    )
    return pl.pallas_call(
        _combine_kernel,
        grid_spec=grid_spec,
        out_shape=jax.ShapeDtypeStruct((n, d), F32),
        compiler_params=_cparams(("arbitrary",)),
        name="combine_ln",
    )(pos_t, gate_nk, x1, ys, wsg_bf, wsu_bf, wsd_bf, ln_g.reshape(1, d), ln_b.reshape(1, d))


def _block_layout(counts_f, n_assign):
    counts = counts_f.astype(I32)
    padded = (counts + EXPERT_ROWS - 1) // EXPERT_ROWS * EXPERT_ROWS
    end = jnp.cumsum(padded)
    total_rows = n_assign + N_EXPERTS * EXPERT_ROWS
    start = jnp.concatenate([end - padded, jnp.full((1,), total_rows, I32)])
    nblk = total_rows // EXPERT_ROWS
    used = end[-1:] // EXPERT_ROWS
    blk_start = jnp.minimum(jnp.arange(nblk, dtype=I32), used[0] - 1) * EXPERT_ROWS
    blk_expert = jnp.sum((end[None, :] <= blk_start[:, None]).astype(I32), axis=1)
    blk_expert = jnp.minimum(blk_expert, N_EXPERTS - 1)
    return start, counts, blk_expert, used.astype(I32), total_rows


def kernel(x, w_in, conv_w, conv_b, w_rg_a, b_rg_a, w_rg_x, b_rg_x, lru_lambda, w_out, ln1_g, ln1_b,
           w_router, router_bias, w_gate, w_up, w_down, ws_gate, ws_up, ws_down, ln2_g, ln2_b):
    b, s, d = x.shape
    n = b * s
    width = d // 2
    for l in range(DEPTH):
        x2 = x.reshape(n, d)
        proj = _proj(x2, w_in[l].astype(BF16), s, width)
        proj3 = proj.reshape(b, s, proj.shape[1])
        lru = _rglru(proj3, conv_w[l], conv_b[l], w_rg_a[l].astype(BF16), b_rg_a[l],
                     w_rg_x[l].astype(BF16), b_rg_x[l], lru_lambda[l], width)
        att = _attention(proj3, width)
        x1, x1p, idx_t, gate_t, rank_t, cnt = _mix_route(
            lru.reshape(n, width), att.reshape(n, width), x2, w_out[l].astype(BF16),
            ln1_g[l], ln1_b[l], w_router[l], router_bias[l])
        start, counts, blk_expert, used, total_rows = _block_layout(cnt[:, 0], n * TOP_K)
        onehot = idx_t[:, :, None] == jnp.arange(N_EXPERTS, dtype=I32)[None, None, :]
        pos_t = jnp.sum(jnp.where(onehot, start[None, None, :N_EXPERTS], 0), axis=2) + rank_t
        xs = _dispatch(x1p.reshape(n, SUBLANES, LANES), pos_t, start, counts, total_rows)
        ys = _experts(xs.reshape(total_rows * SUBLANES, LANES), blk_expert, used,
                      w_gate[l], w_up[l], w_down[l])
        out = _combine(x1, ys.reshape(total_rows, SUBLANES, LANES), pos_t, gate_t.T,
                       ws_gate[l].astype(BF16),
                       ws_up[l].astype(BF16), ws_down[l].astype(BF16), ln2_g[l], ln2_b[l])
        x = out.reshape(b, s, d)
    return x
```

```python
import functools

import jax
import jax.numpy as jnp
from jax import lax
from jax.experimental import pallas as pl
from jax.experimental.pallas import tpu as pltpu

F32 = jnp.float32
BF16 = jnp.bfloat16
I32 = jnp.int32
U32 = jnp.uint32

LRU_BLOCKS = 8
CONV_W = 4
LRU_C = 8.0
HEAD_DIM = 128
ROT_DIM = HEAD_DIM // 4
ROPE_THETA = 500000.0
DILATED_PATTERNS = ((128, 1), (512, 4), (2048, 16))
ATTN_BLOCK = 128
N_EXPERTS = 64
N_GROUPS = 8
GROUP_SIZE = N_EXPERTS // N_GROUPS
TOPK_GROUPS = 4
TOP_K = 8
ROUTED_SCALE = 2.5
DEPTH = 1
DN_ALPHA = (2 * DEPTH) ** 0.25
LN_EPS = 1e-5

PROJ_TM = 512
LRU_TC = 256
MIX_TM = 512
ATTN_UNROLL = 4
EXPERT_ROWS = 256
DISPATCH_TT = 256
COMBINE_TT = 128
SUBLANES = 8
LANES = 128
TILE_WORDS = SUBLANES * LANES
VMEM_LIMIT = 56 * 1024 * 1024


def _pack_bf16_pair(lo, hi):
    lo_bits = lax.bitcast_convert_type(lo.astype(BF16).astype(F32), U32)
    hi_bits = lax.bitcast_convert_type(hi.astype(BF16).astype(F32), U32)
    return (lo_bits >> 16) | (hi_bits & jnp.uint32(0xFFFF0000))


def _unpack_bf16_pair(packed):
    lo = lax.bitcast_convert_type(packed << 16, F32)
    hi = lax.bitcast_convert_type(packed & jnp.uint32(0xFFFF0000), F32)
    return lo, hi


def _store_rows_as_tiles(ref, words):
    rows = words.shape[0]
    for s in range(SUBLANES):
        ref[pl.ds(s, rows, stride=SUBLANES), :] = words[:, s * LANES:(s + 1) * LANES]


def _load_rows_from_tiles(ref, row0, rows):
    return jnp.concatenate(
        [ref[pl.ds(row0 * SUBLANES + s, rows, stride=SUBLANES), :] for s in range(SUBLANES)], axis=1)


def _cparams(sem):
    return pltpu.CompilerParams(dimension_semantics=sem, vmem_limit_bytes=VMEM_LIMIT)


def _proj_kernel(x_ref, w_ref, c_ref, sa_ref, sb_ref, o_ref, *, n_heads):
    j = pl.program_id(0)
    acc = jnp.dot(x_ref[...].astype(BF16), w_ref[...], preferred_element_type=F32)
    is_rope = (j == 2) | (j == 3)

    @pl.when(is_rope)
    def _():
        scale = jnp.where(j == 2, HEAD_DIM ** -0.5, 1.0).astype(F32)
        c = c_ref[...] * scale
        sa = sa_ref[...] * scale
        sb = sb_ref[...] * scale
        half = ROT_DIM // 2
        for h in range(n_heads):
            t = acc[:, h * HEAD_DIM:(h + 1) * HEAD_DIM]
            o_ref[:, h * HEAD_DIM:(h + 1) * HEAD_DIM] = (
                t * c + pltpu.roll(t, half, 1) * sa + pltpu.roll(t, HEAD_DIM - half, 1) * sb)

    @pl.when(jnp.logical_not(is_rope))
    def _():
        o_ref[...] = acc


def _rope_tables(seq):
    half = ROT_DIM // 2
    pos = jnp.arange(seq, dtype=F32)
    inv = ROPE_THETA ** (-jnp.arange(0, ROT_DIM, 2, dtype=F32) / ROT_DIM)
    ang = pos[:, None] * inv[None, :]
    cos, sin = jnp.cos(ang), jnp.sin(ang)
    ones = jnp.ones((seq, HEAD_DIM - ROT_DIM), F32)
    zeros = jnp.zeros((seq, HEAD_DIM - ROT_DIM), F32)
    zh = jnp.zeros((seq, half), F32)
    c = jnp.concatenate([cos, cos, ones], -1)
    sa = jnp.concatenate([zh, sin, zeros], -1)
    sb = jnp.concatenate([-sin, zh, zeros], -1)
    return c, sa, sb


def _proj(x2, w_in_bf, seq, width):
    n, d = x2.shape
    e = w_in_bf.shape[1]
    n_groups = e // width
    c, sa, sb = _rope_tables(seq)
    tiles_per_seq = seq // PROJ_TM
    tab_spec = pl.BlockSpec((PROJ_TM, HEAD_DIM), lambda j, i: (i % tiles_per_seq, 0))
    return pl.pallas_call(
        functools.partial(_proj_kernel, n_heads=width // HEAD_DIM),
        grid=(n_groups, n // PROJ_TM),
        in_specs=[pl.BlockSpec((PROJ_TM, d), lambda j, i: (i, 0)),
                  pl.BlockSpec((d, width), lambda j, i: (0, j)),
                  tab_spec, tab_spec, tab_spec],
        out_specs=pl.BlockSpec((PROJ_TM, width), lambda j, i: (i, j)),
        out_shape=jax.ShapeDtypeStruct((n, e), F32),
        compiler_params=_cparams(("arbitrary", "arbitrary")),
        name="proj_rope",
    )(x2, w_in_bf, c, sa, sb)


def _lru_kernel(xl_ref, gl_ref, cw_ref, cb_ref, wa_ref, ba_ref, wx_ref, bx_ref, lam_ref,
                o_ref, xext_ref, hprev_ref, a_ref, b_ref, h_ref, *, block_w):
    tc, width = xl_ref.shape
    groups = tc // SUBLANES

    @pl.when(pl.program_id(1) == 0)
    def _():
        xext_ref[0:SUBLANES, :] = jnp.zeros((SUBLANES, width), F32)
        hprev_ref[...] = jnp.zeros_like(hprev_ref)

    x = xl_ref[...]
    xext_ref[SUBLANES:SUBLANES + tc, :] = x
    xc = x * cw_ref[CONV_W - 1:CONV_W, :] + cb_ref[...]
    for j in range(CONV_W - 1):
        back = CONV_W - 1 - j
        xc = xc + xext_ref[pl.ds(SUBLANES - back, tc), :] * cw_ref[j:j + 1, :]
    xext_ref[0:SUBLANES, :] = x[tc - SUBLANES:, :]

    lam = lam_ref[...]
    softplus_neg_lam = jnp.maximum(-lam, 0.0) + jnp.log1p(jnp.exp(-jnp.abs(lam)))
    rs, is_ = [], []
    for nb in range(width // block_w):
        xb = xc[:, nb * block_w:(nb + 1) * block_w].astype(BF16)
        rs.append(jnp.dot(xb, wa_ref[nb], preferred_element_type=F32))
        is_.append(jnp.dot(xb, wx_ref[nb], preferred_element_type=F32))
    r = jax.nn.sigmoid(jnp.concatenate(rs, axis=1) + ba_ref[...])
    ig = jax.nn.sigmoid(jnp.concatenate(is_, axis=1) + bx_ref[...])
    a = jnp.exp(-LRU_C * r * softplus_neg_lam)
    b = jnp.sqrt(1.0 - a * a) * (ig * xc)

    row = lax.broadcasted_iota(I32, (tc, width), 0) & (SUBLANES - 1)
    k = 1
    while k < SUBLANES:
        keep = row >= k
        a_sh = pltpu.roll(a, k, 0)
        b_sh = pltpu.roll(b, k, 0)
        b = jnp.where(keep, a * b_sh + b, b)
        a = jnp.where(keep, a * a_sh, a)
        k *= 2
    a_ref[...] = a.reshape(groups, SUBLANES, width)
    b_ref[...] = b.reshape(groups, SUBLANES, width)

    def carry(g, hprev):
        h = a_ref[g] * hprev + b_ref[g]
        h_ref[g] = h
        return h[SUBLANES - 1:SUBLANES, :]

    hprev_ref[...] = lax.fori_loop(0, groups, carry, hprev_ref[...])
    h = h_ref[...].reshape(tc, width)
    o_ref[...] = (h * jax.nn.gelu(gl_ref[...], approximate=True)).astype(o_ref.dtype)


def _rglru(proj3, conv_w, conv_b, wa_bf, b_a, wx_bf, b_x, lam, width):
    b, s, _ = proj3.shape
    block_w = width // LRU_BLOCKS
    vec = pl.BlockSpec((1, width), lambda bi, ci: (0, 0))
    wspec = pl.BlockSpec((LRU_BLOCKS, block_w, block_w), lambda bi, ci: (0, 0, 0))
    groups = LRU_TC // SUBLANES
    return pl.pallas_call(
        functools.partial(_lru_kernel, block_w=block_w),
        grid=(b, s // LRU_TC),
        in_specs=[pl.BlockSpec((None, LRU_TC, width), lambda bi, ci: (bi, ci, 0)),
                  pl.BlockSpec((None, LRU_TC, width), lambda bi, ci: (bi, ci, 1)),
                  pl.BlockSpec((CONV_W, width), lambda bi, ci: (0, 0)),
                  vec, wspec, vec, wspec, vec, vec],
        out_specs=pl.BlockSpec((None, LRU_TC, width), lambda bi, ci: (bi, ci, 0)),
        out_shape=jax.ShapeDtypeStruct((b, s, width), BF16),
        scratch_shapes=[pltpu.VMEM((LRU_TC + SUBLANES, width), F32),
                        pltpu.VMEM((1, width), F32),
                        pltpu.VMEM((groups, SUBLANES, width), F32),
                        pltpu.VMEM((groups, SUBLANES, width), F32),
                        pltpu.VMEM((groups, SUBLANES, width), F32)],
        compiler_params=_cparams(("arbitrary", "arbitrary")),
        name="rglru",
    )(proj3, proj3, conv_w, conv_b.reshape(1, width), wa_bf, b_a.reshape(1, width),
      wx_bf, b_x.reshape(1, width), lam.reshape(1, width))


def _attn_kernel(q_ref, k_ref, v_ref, o_ref, qs, ks, vs, acc_s, m_s, l_s, acc_n, m_n, l_n):
    seq = q_ref.shape[0]
    blk = ATTN_BLOCK
    nblk = seq // blk
    n_pat = len(DILATED_PATTERNS)

    for p, (_, d) in enumerate(DILATED_PATTERNS):
        cls_len = seq // d
        ks[p, 0:blk, :] = jnp.zeros((blk, HEAD_DIM), BF16)
        vs[p, 0:blk, :] = jnp.zeros((blk, HEAD_DIM), BF16)
        for r in range(d):
            src = pl.ds(r, cls_len, stride=d) if d > 1 else pl.ds(0, seq)
            qs[p, r * cls_len:(r + 1) * cls_len, :] = q_ref[src, :].astype(BF16)
            ks[p, blk + r * cls_len:blk + (r + 1) * cls_len, :] = k_ref[src, :].astype(BF16)
            vs[p, blk + r * cls_len:blk + (r + 1) * cls_len, :] = v_ref[src, :].astype(BF16)

    qi = lax.broadcasted_iota(I32, (blk, 2 * blk), 0)
    kj = lax.broadcasted_iota(I32, (blk, 2 * blk), 1)
    band2 = (kj >= qi) & (kj <= qi + blk)
    in_block = kj >= blk
    qi1 = lax.broadcasted_iota(I32, (blk, blk), 0)
    kj1 = lax.broadcasted_iota(I32, (blk, blk), 1)
    causal1 = kj1 <= qi1
    nt = (((1,), (1,)), ((), ()))

    def one_block(p, row0, first):
        q = qs[p, pl.ds(row0, blk), :]
        if first is True:
            kk = ks[p, pl.ds(row0 + blk, blk), :]
            vv = vs[p, pl.ds(row0 + blk, blk), :]
            mask = causal1
        else:
            kk = ks[p, pl.ds(row0, 2 * blk), :]
            vv = vs[p, pl.ds(row0, 2 * blk), :]
            mask = band2 if first is False else band2 & (in_block | jnp.logical_not(first))
        s = lax.dot_general(q, kk, nt, preferred_element_type=F32)
        s = jnp.where(mask, s, -jnp.inf)
        m = jnp.max(s, axis=1, keepdims=True)
        e = jnp.exp(s - m)
        l = jnp.sum(e, axis=1, keepdims=True)
        acc = jnp.dot(e.astype(BF16), vv, preferred_element_type=F32)
        acc_s[p, pl.ds(row0, blk), :] = acc
        m_s[p, pl.ds(row0, blk), :] = jnp.broadcast_to(m, (blk, HEAD_DIM))
        l_s[p, pl.ds(row0, blk), :] = jnp.broadcast_to(l, (blk, HEAD_DIM))

    def body(it, carry):
        for p, (window, d) in enumerate(DILATED_PATTERNS):
            assert window // d == blk
            blocks_per_class = (seq // d) // blk
            for j in range(ATTN_UNROLL):
                row0 = pl.multiple_of((it * ATTN_UNROLL + j) * blk, blk)
                if ATTN_UNROLL % blocks_per_class == 0:
                    first = j % blocks_per_class == 0
                else:
                    assert blocks_per_class % ATTN_UNROLL == 0
                    first = (it % (blocks_per_class // ATTN_UNROLL) == 0) if j == 0 else False
                one_block(p, row0, first)
        return carry

    lax.fori_loop(0, nblk // ATTN_UNROLL, body, 0)

    for p, (_, d) in enumerate(DILATED_PATTERNS):
        cls_len = seq // d
        for r in range(d):
            dst = pl.ds(r, cls_len, stride=d) if d > 1 else pl.ds(0, seq)
            acc_n[p, dst, :] = acc_s[p, r * cls_len:(r + 1) * cls_len, :]
            m_n[p, dst, :] = m_s[p, r * cls_len:(r + 1) * cls_len, :]
            l_n[p, dst, :] = l_s[p, r * cls_len:(r + 1) * cls_len, :]

    mx = m_n[0]
    for p in range(1, n_pat):
        mx = jnp.maximum(mx, m_n[p])
    num = jnp.zeros((seq, HEAD_DIM), F32)
    den = jnp.zeros((seq, HEAD_DIM), F32)
    for p in range(n_pat):
        w = jnp.exp(m_n[p] - mx)
        num = num + w * acc_n[p]
        den = den + w * l_n[p]
    o_ref[...] = (num / den).astype(o_ref.dtype)


def _attention(proj3, width):
    b, s, _ = proj3.shape
    n_heads = width // HEAD_DIM
    n_pat = len(DILATED_PATTERNS)
    col0 = 2 * width // HEAD_DIM

    def spec(group):
        return pl.BlockSpec((None, s, HEAD_DIM),
                            lambda bi, hi: (bi, 0, col0 + group * n_heads + hi))

    return pl.pallas_call(
        _attn_kernel,
        grid=(b, n_heads),
        in_specs=[spec(0), spec(1), spec(2)],
        out_specs=pl.BlockSpec((None, s, HEAD_DIM), lambda bi, hi: (bi, 0, hi)),
        out_shape=jax.ShapeDtypeStruct((b, s, width), BF16),
        scratch_shapes=[pltpu.VMEM((n_pat, s, HEAD_DIM), BF16),
                        pltpu.VMEM((n_pat, s + ATTN_BLOCK, HEAD_DIM), BF16),
                        pltpu.VMEM((n_pat, s + ATTN_BLOCK, HEAD_DIM), BF16)]
                       + [pltpu.VMEM((n_pat, s, HEAD_DIM), F32)] * 6,
        compiler_params=_cparams(("arbitrary", "arbitrary")),
        name="dilated_attention",
    )(proj3, proj3, proj3)


def _layer_norm(y, g, b):
    mu = jnp.mean(y, axis=-1, keepdims=True)
    yc = y - mu
    var = jnp.mean(yc * yc, axis=-1, keepdims=True)
    return yc * lax.rsqrt(var + LN_EPS) * g + b


def _first_index(hit, iota, size):
    return jnp.min(jnp.where(hit, iota, size), axis=0, keepdims=True)


def _mix_kernel(lru_ref, att_ref, x_ref, wo_ref, g_ref, b_ref, wrh_ref, wrl_ref, rb_ref, tri_ref,
                x1_ref, x1p_ref, idx_ref, gate_ref, rank_ref, cnt_ref, base_ref):
    tm = x_ref.shape[0]
    half = lru_ref.shape[1]

    @pl.when(pl.program_id(0) == 0)
    def _():
        base_ref[...] = jnp.zeros_like(base_ref)

    mix = jnp.dot(lru_ref[...], wo_ref[0:half, :], preferred_element_type=F32)
    mix = mix + jnp.dot(att_ref[...], wo_ref[half:, :], preferred_element_type=F32)
    x1 = _layer_norm(DN_ALPHA * x_ref[...] + mix, g_ref[...], b_ref[...])
    x1_ref[...] = x1
    _store_rows_as_tiles(x1p_ref, _pack_bf16_pair(x1[:, 0:half], x1[:, half:]))

    x_hi = x1.astype(BF16)
    x_lo = (x1 - x_hi.astype(F32)).astype(BF16)
    nt = (((1,), (1,)), ((), ()))
    logits = lax.dot_general(wrh_ref[...], x_hi, nt, preferred_element_type=F32)
    logits = logits + lax.dot_general(wrh_ref[...], x_lo, nt, preferred_element_type=F32)
    logits = logits + lax.dot_general(wrl_ref[...], x_hi, nt, preferred_element_type=F32)
    scores = jax.nn.sigmoid(logits)
    sel = scores + rb_ref[...]

    neg = -jnp.inf
    gi = lax.broadcasted_iota(I32, (GROUP_SIZE, tm), 0)
    grp_rows = []
    for g in range(N_GROUPS):
        blk = sel[g * GROUP_SIZE:(g + 1) * GROUP_SIZE, :]
        m1 = jnp.max(blk, axis=0, keepdims=True)
        f1 = _first_index(blk == m1, gi, GROUP_SIZE)
        m2 = jnp.max(jnp.where(gi == f1, neg, blk), axis=0, keepdims=True)
        grp_rows.append(m1 + m2)
    ni = lax.broadcasted_iota(I32, (N_GROUPS, tm), 0)
    gscore = jnp.zeros((N_GROUPS, tm), F32)
    for g in range(N_GROUPS):
        gscore = jnp.where(ni == g, grp_rows[g], gscore)
    gmask = jnp.zeros((N_GROUPS, tm), jnp.bool_)
    for _ in range(TOPK_GROUPS):
        m = jnp.max(gscore, axis=0, keepdims=True)
        f = _first_index(gscore == m, ni, N_GROUPS)
        hit = ni == f
        gmask = gmask | hit
        gscore = jnp.where(hit, neg, gscore)
    emask = jnp.concatenate(
        [jnp.broadcast_to(gmask[g:g + 1, :], (GROUP_SIZE, tm)) for g in range(N_GROUPS)], axis=0)
    masked = jnp.where(emask, sel, neg)

    ei = lax.broadcasted_iota(I32, (N_EXPERTS, tm), 0)
    chosen = jnp.zeros((N_EXPERTS, tm), jnp.bool_)
    picks, gates = [], []
    for _ in range(TOP_K):
        m = jnp.max(masked, axis=0, keepdims=True)
        f = _first_index(masked == m, ei, N_EXPERTS)
        hit = ei == f
        picks.append(f)
        gates.append(jnp.sum(jnp.where(hit, scores, 0.0), axis=0, keepdims=True))
        chosen = chosen | hit
        masked = jnp.where(hit, neg, masked)
    gsum = gates[0]
    for k in range(1, TOP_K):
        gsum = gsum + gates[k]

    chosen_f = jnp.where(chosen, 1.0, 0.0)
    incl = jnp.dot(chosen_f.astype(BF16), tri_ref[...], preferred_element_type=F32)
    rank_excl = base_ref[:, 0:1] + incl - chosen_f
    base_ref[...] = base_ref[...] + jnp.sum(chosen_f, axis=1, keepdims=True)
    cnt_ref[...] = base_ref[...]
    for k in range(TOP_K):
        idx_ref[k:k + 1, :] = picks[k]
        gate_ref[k:k + 1, :] = gates[k] / gsum * ROUTED_SCALE
        rk = jnp.sum(jnp.where(ei == picks[k], rank_excl, 0.0), axis=0, keepdims=True)
        rank_ref[k:k + 1, :] = rk.astype(I32)


def _mix_route(lru, att, x2, wo_bf, ln_g, ln_b, w_router, router_bias):
    n, d = x2.shape
    half = lru.shape[1]
    wr_t = w_router.T
    wr_hi = wr_t.astype(BF16)
    wr_lo = (wr_t - wr_hi.astype(F32)).astype(BF16)
    assert half == TILE_WORDS
    tri = jnp.triu(jnp.ones((MIX_TM, MIX_TM), BF16))
    const = lambda shape: pl.BlockSpec(shape, lambda i: (0,) * len(shape))
    tok = lambda dtype: jax.ShapeDtypeStruct((TOP_K, n), dtype)
    tok_spec = pl.BlockSpec((TOP_K, MIX_TM), lambda i: (0, i))
    return pl.pallas_call(
        _mix_kernel,
        grid=(n // MIX_TM,),
        in_specs=[pl.BlockSpec((MIX_TM, half), lambda i: (i, 0)),
                  pl.BlockSpec((MIX_TM, half), lambda i: (i, 0)),
                  pl.BlockSpec((MIX_TM, d), lambda i: (i, 0)),
                  const((d, d)), const((1, d)), const((1, d)),
                  const((N_EXPERTS, d)), const((N_EXPERTS, d)), const((N_EXPERTS, 1)),
                  const((MIX_TM, MIX_TM))],
        out_specs=[pl.BlockSpec((MIX_TM, d), lambda i: (i, 0)),
                   pl.BlockSpec((MIX_TM * SUBLANES, LANES), lambda i: (i, 0)),
                   tok_spec, tok_spec, tok_spec, const((N_EXPERTS, HEAD_DIM))],
        out_shape=[jax.ShapeDtypeStruct((n, d), F32), jax.ShapeDtypeStruct((n * SUBLANES, LANES), U32),
                   tok(I32), tok(F32), tok(I32),
                   jax.ShapeDtypeStruct((N_EXPERTS, HEAD_DIM), F32)],
        scratch_shapes=[pltpu.VMEM((N_EXPERTS, HEAD_DIM), F32)],
        compiler_params=_cparams(("arbitrary",)),
        name="mix_ln_route",
    )(lru, att, x2, wo_bf, ln_g.reshape(1, d), ln_b.reshape(1, d), wr_hi, wr_lo,
      router_bias.reshape(N_EXPERTS, 1), tri)


def _dispatch_kernel(start_ref, cnt_ref, pos_ref, x1_ref, zero_ref, xs_hbm, sem):
    i = pl.program_id(0)
    tt = pos_ref.shape[1]

    def row_copy(t, k):
        return pltpu.make_async_copy(x1_ref.at[t], xs_hbm.at[pos_ref[k, t]], sem)

    def issue(t, c):
        for k in range(TOP_K):
            row_copy(t, k).start()
        return c

    def drain(t, c):
        for k in range(TOP_K):
            row_copy(t, k).wait()
        return c

    lax.fori_loop(0, tt, issue, 0)
    lax.fori_loop(0, tt, drain, 0)

    @pl.when(i == 0)
    def _():
        def pad_copy(r):
            return pltpu.make_async_copy(zero_ref.at[0], xs_hbm.at[r], sem)

        def per_expert(e, c):
            lo = start_ref[e] + cnt_ref[e]
            hi = start_ref[e + 1]
            lax.fori_loop(lo, hi, lambda r, cc: (pad_copy(r).start(), cc)[1], 0)
            lax.fori_loop(lo, hi, lambda r, cc: (pad_copy(r).wait(), cc)[1], 0)
            return c

        lax.fori_loop(0, N_EXPERTS, per_expert, 0)


def _dispatch(x1p, pos_t, start, counts, total_rows):
    n = x1p.shape[0]
    smem_tok = pl.BlockSpec((TOP_K, DISPATCH_TT), lambda i, *_: (0, i), memory_space=pltpu.SMEM)
    grid_spec = pltpu.PrefetchScalarGridSpec(
        num_scalar_prefetch=2,
        grid=(n // DISPATCH_TT,),
        in_specs=[smem_tok,
                  pl.BlockSpec((DISPATCH_TT, SUBLANES, LANES), lambda i, *_: (i, 0, 0)),
                  pl.BlockSpec((1, SUBLANES, LANES), lambda i, *_: (0, 0, 0))],
        out_specs=pl.BlockSpec(memory_space=pl.ANY),
        scratch_shapes=[pltpu.SemaphoreType.DMA(())],
    )
    return pl.pallas_call(
        _dispatch_kernel,
        grid_spec=grid_spec,
        out_shape=jax.ShapeDtypeStruct((total_rows, SUBLANES, LANES), x1p.dtype),
        compiler_params=_cparams(("arbitrary",)),
        name="dispatch",
    )(start, counts, pos_t, x1p, jnp.zeros((1, SUBLANES, LANES), x1p.dtype))


def _expert_kernel(be_ref, used_ref, nxt_ref, slot_ref, xs_ref, wg_hbm, wu_hbm, wd_hbm, ys_ref,
                   wg_f32, wu_f32, wd_f32, wg_bf, wu_bf, wd_bf, sems):
    i = pl.program_id(0)
    e = be_ref[i]
    slot = slot_ref[i]
    changed = (i == 0) | (e != be_ref[jnp.maximum(i - 1, 0)])

    def weight_copies(expert, s):
        return (pltpu.make_async_copy(wg_hbm.at[expert], wg_f32.at[s], sems.at[0, s]),
                pltpu.make_async_copy(wu_hbm.at[expert], wu_f32.at[s], sems.at[1, s]),
                pltpu.make_async_copy(wd_hbm.at[expert], wd_f32.at[s], sems.at[2, s]))

    @pl.when(i == 0)
    def _():
        for c in weight_copies(e, slot):
            c.start()

    @pl.when(changed)
    def _():
        for c in weight_copies(e, slot):
            c.wait()
        nxt = nxt_ref[i]

        @pl.when(nxt != e)
        def _():
            for c in weight_copies(nxt, 1 - slot):
                c.start()

        wg_bf[...] = wg_f32[slot].astype(BF16)
        wu_bf[...] = wu_f32[slot].astype(BF16)
        wd_bf[...] = wd_f32[slot].astype(BF16)

    @pl.when(i < used_ref[0])
    def _():
        half = TILE_WORDS
        lo, hi = _unpack_bf16_pair(_load_rows_from_tiles(xs_ref, 0, EXPERT_ROWS))
        lo, hi = lo.astype(BF16), hi.astype(BF16)
        g = (jnp.dot(lo, wg_bf[0:half, :], preferred_element_type=F32)
             + jnp.dot(hi, wg_bf[half:, :], preferred_element_type=F32))
        u = (jnp.dot(lo, wu_bf[0:half, :], preferred_element_type=F32)
             + jnp.dot(hi, wu_bf[half:, :], preferred_element_type=F32))
        h = (g * jax.nn.sigmoid(g) * u).astype(BF16)
        y = jnp.dot(h, wd_bf[...], preferred_element_type=F32)
        _store_rows_as_tiles(ys_ref, _pack_bf16_pair(y[:, 0:half], y[:, half:]))

    @pl.when(i >= used_ref[0])
    def _():
        ys_ref[...] = jnp.zeros_like(ys_ref)


def _experts(xs, blk_expert, used, w_gate, w_up, w_down):
    rows = xs.shape[0] // SUBLANES
    d = 2 * TILE_WORDS
    de = w_gate.shape[2]
    nblk = rows // EXPERT_ROWS
    later = blk_expert[None, :] > blk_expert[:, None]
    nxt = jnp.min(jnp.where(later, blk_expert[None, :], N_EXPERTS), axis=1)
    nxt = jnp.where(nxt == N_EXPERTS, blk_expert, nxt)
    is_new = jnp.concatenate([jnp.zeros((1,), I32), (blk_expert[1:] != blk_expert[:-1]).astype(I32)])
    slot = jnp.cumsum(is_new) % 2
    grid_spec = pltpu.PrefetchScalarGridSpec(
        num_scalar_prefetch=4,
        grid=(nblk,),
        in_specs=[pl.BlockSpec((EXPERT_ROWS * SUBLANES, LANES),
                               lambda i, be, used, *_: (jnp.minimum(i, used[0] - 1), 0)),
                  pl.BlockSpec(memory_space=pl.ANY), pl.BlockSpec(memory_space=pl.ANY),
                  pl.BlockSpec(memory_space=pl.ANY)],
        out_specs=pl.BlockSpec((EXPERT_ROWS * SUBLANES, LANES), lambda i, *_: (i, 0)),
        scratch_shapes=[pltpu.VMEM((2, d, de), F32), pltpu.VMEM((2, d, de), F32),
                        pltpu.VMEM((2, de, d), F32),
                        pltpu.VMEM((d, de), BF16), pltpu.VMEM((d, de), BF16),
                        pltpu.VMEM((de, d), BF16), pltpu.SemaphoreType.DMA((3, 2))],
    )
    return pl.pallas_call(
        _expert_kernel,
        grid_spec=grid_spec,
        out_shape=jax.ShapeDtypeStruct((rows * SUBLANES, LANES), U32),
        compiler_params=_cparams(("arbitrary",)),
        name="experts",
    )(blk_expert, used, nxt.astype(I32), slot.astype(I32), xs, w_gate, w_up, w_down)


def _combine_kernel(pos_ref, gate_ref, x1_ref, ys_hbm, wsg_ref, wsu_ref,
                    wsd_ref, g_ref, b_ref, o_ref, buf, sem):
    tt = x1_ref.shape[0]

    def row_copy(t, k):
        dst = pl.multiple_of((k * tt + t) * SUBLANES, SUBLANES)
        return pltpu.make_async_copy(ys_hbm.at[pos_ref[k, t]], buf.at[pl.ds(dst, SUBLANES)], sem)

    def issue(t, c):
        for k in range(TOP_K):
            row_copy(t, k).start()
        return c

    def drain(t, c):
        for k in range(TOP_K):
            row_copy(t, k).wait()
        return c

    lax.fori_loop(0, tt, issue, 0)

    x1 = x1_ref[...]
    xb = x1.astype(BF16)
    g = jnp.dot(xb, wsg_ref[...], preferred_element_type=F32)
    u = jnp.dot(xb, wsu_ref[...], preferred_element_type=F32)
    h = (g * jax.nn.sigmoid(g) * u).astype(BF16)
    y = DN_ALPHA * x1 + jnp.dot(h, wsd_ref[...], preferred_element_type=F32)

    lax.fori_loop(0, tt, drain, 0)
    gate = gate_ref[...]
    r_lo = jnp.zeros((tt, TILE_WORDS), F32)
    r_hi = jnp.zeros((tt, TILE_WORDS), F32)
    for k in range(TOP_K):
        lo, hi = _unpack_bf16_pair(_load_rows_from_tiles(buf, k * tt, tt))
        r_lo = r_lo + lo * gate[:, k:k + 1]
        r_hi = r_hi + hi * gate[:, k:k + 1]
    y = y + jnp.concatenate([r_lo, r_hi], axis=1)
    o_ref[...] = _layer_norm(y, g_ref[...], b_ref[...])


def _combine(x1, ys, pos_t, gate_nk, wsg_bf, wsu_bf, wsd_bf, ln_g, ln_b):
    n, d = x1.shape
    de = wsg_bf.shape[1]
    smem_tok = pl.BlockSpec((TOP_K, COMBINE_TT), lambda i, *_: (0, i), memory_space=pltpu.SMEM)
    const = lambda shape: pl.BlockSpec(shape, lambda i, *_: (0,) * len(shape))
    grid_spec = pltpu.PrefetchScalarGridSpec(
        num_scalar_prefetch=0,
        grid=(n // COMBINE_TT,),
        in_specs=[smem_tok,
                  pl.BlockSpec((COMBINE_TT, TOP_K), lambda i, *_: (i, 0)),
                  pl.BlockSpec((COMBINE_TT, d), lambda i, *_: (i, 0)),
                  pl.BlockSpec(memory_space=pl.ANY),
                  const((d, de)), const((d, de)), const((de, d)), const((1, d)), const((1, d))],
        out_specs=pl.BlockSpec((COMBINE_TT, d), lambda i, *_: (i, 0)),
        scratch_shapes=[pltpu.VMEM((TOP_K * COMBINE_TT * SUBLANES, LANES), U32),
                        pltpu.SemaphoreType.DMA(())],
    )
    return pl.pallas_call(
        _combine_kernel,
        grid_spec=grid_spec,
        out_shape=jax.ShapeDtypeStruct((n, d), F32),
        compiler_params=_cparams(("arbitrary",)),
        name="combine_ln",
    )(pos_t, gate_nk, x1, ys, wsg_bf, wsu_bf, wsd_bf, ln_g.reshape(1, d), ln_b.reshape(1, d))


def _block_layout(counts_f, n_assign):
    counts = counts_f.astype(I32)
    padded = (counts + EXPERT_ROWS - 1) // EXPERT_ROWS * EXPERT_ROWS
    end = jnp.cumsum(padded)
    total_rows = n_assign + N_EXPERTS * EXPERT_ROWS
    start = jnp.concatenate([end - padded, jnp.full((1,), total_rows, I32)])
    nblk = total_rows // EXPERT_ROWS
    used = end[-1:] // EXPERT_ROWS
    blk_start = jnp.minimum(jnp.arange(nblk, dtype=I32), used[0] - 1) * EXPERT_ROWS
    blk_expert = jnp.sum((end[None, :] <= blk_start[:, None]).astype(I32), axis=1)
    blk_expert = jnp.minimum(blk_expert, N_EXPERTS - 1)
    return start, counts, blk_expert, used.astype(I32), total_rows


def kernel(x, w_in, conv_w, conv_b, w_rg_a, b_rg_a, w_rg_x, b_rg_x, lru_lambda, w_out, ln1_g, ln1_b,
           w_router, router_bias, w_gate, w_up, w_down, ws_gate, ws_up, ws_down, ln2_g, ln2_b):
    b, s, d = x.shape
    n = b * s
    width = d // 2
    for l in range(DEPTH):
        x2 = x.reshape(n, d)
        proj = _proj(x2, w_in[l].astype(BF16), s, width)
        proj3 = proj.reshape(b, s, proj.shape[1])
        lru = _rglru(proj3, conv_w[l], conv_b[l], w_rg_a[l].astype(BF16), b_rg_a[l],
                     w_rg_x[l].astype(BF16), b_rg_x[l], lru_lambda[l], width)
        att = _attention(proj3, width)
        x1, x1p, idx_t, gate_t, rank_t, cnt = _mix_route(
            lru.reshape(n, width), att.reshape(n, width), x2, w_out[l].astype(BF16),
            ln1_g[l], ln1_b[l], w_router[l], router_bias[l])
        start, counts, blk_expert, used, total_rows = _block_layout(cnt[:, 0], n * TOP_K)
        onehot = idx_t[:, :, None] == jnp.arange(N_EXPERTS, dtype=I32)[None, None, :]
        pos_t = jnp.sum(jnp.where(onehot, start[None, None, :N_EXPERTS], 0), axis=2) + rank_t
        xs = _dispatch(x1p.reshape(n, SUBLANES, LANES), pos_t, start, counts, total_rows)
        ys = _experts(xs.reshape(total_rows * SUBLANES, LANES), blk_expert, used,
                      w_gate[l], w_up[l], w_down[l])
        out = _combine(x1, ys.reshape(total_rows, SUBLANES, LANES), pos_t, gate_t.T,
                       ws_gate[l].astype(BF16),
                       ws_up[l].astype(BF16), ws_down[l].astype(BF16), ln2_g[l], ln2_b[l])
        x = out.reshape(b, s, d)
    return x
```

```python
import functools

import jax
import jax.numpy as jnp
from jax import lax
from jax.experimental import pallas as pl
from jax.experimental.pallas import tpu as pltpu

F32 = jnp.float32
BF16 = jnp.bfloat16
I32 = jnp.int32
U32 = jnp.uint32

LRU_BLOCKS = 8
CONV_W = 4
LRU_C = 8.0
HEAD_DIM = 128
ROT_DIM = HEAD_DIM // 4
ROPE_THETA = 500000.0
DILATED_PATTERNS = ((128, 1), (512, 4), (2048, 16))
ATTN_BLOCK = 128
N_EXPERTS = 64
N_GROUPS = 8
GROUP_SIZE = N_EXPERTS // N_GROUPS
TOPK_GROUPS = 4
TOP_K = 8
ROUTED_SCALE = 2.5
DEPTH = 1
DN_ALPHA = (2 * DEPTH) ** 0.25
LN_EPS = 1e-5

PROJ_TM = 1024
LRU_TC = 512
MIX_TM = 512
ATTN_UNROLL = 8
EXPERT_ROWS = 512
DISPATCH_TT = 512
COMBINE_TT = 256
ZERO_ROWS = 64
SUBLANES = 8
LANES = 128
TILE_WORDS = SUBLANES * LANES
VMEM_LIMIT = 56 * 1024 * 1024


def _pack_bf16_pair(lo, hi):
    lo_bits = lax.bitcast_convert_type(lo.astype(BF16).astype(F32), U32)
    hi_bits = lax.bitcast_convert_type(hi.astype(BF16).astype(F32), U32)
    return (lo_bits >> 16) | (hi_bits & jnp.uint32(0xFFFF0000))


def _unpack_bf16_pair(packed):
    lo = lax.bitcast_convert_type(packed << 16, F32)
    hi = lax.bitcast_convert_type(packed & jnp.uint32(0xFFFF0000), F32)
    return lo, hi


def _store_rows_as_tiles(ref, words):
    rows = words.shape[0]
    for s in range(SUBLANES):
        ref[pl.ds(s, rows, stride=SUBLANES), :] = words[:, s * LANES:(s + 1) * LANES]


def _load_rows_from_tiles(ref, row0, rows):
    return jnp.concatenate(
        [ref[pl.ds(row0 * SUBLANES + s, rows, stride=SUBLANES), :] for s in range(SUBLANES)], axis=1)


def _cparams(sem):
    return pltpu.CompilerParams(dimension_semantics=sem, vmem_limit_bytes=VMEM_LIMIT)


def _proj_kernel(x_ref, w_ref, c_ref, sa_ref, sb_ref, o_ref, *, n_heads):
    j = pl.program_id(0)
    acc = jnp.dot(x_ref[...].astype(BF16), w_ref[...], preferred_element_type=F32)
    is_rope = (j == 2) | (j == 3)

    @pl.when(is_rope)
    def _():
        scale = jnp.where(j == 2, HEAD_DIM ** -0.5, 1.0).astype(F32)
        c = c_ref[...] * scale
        sa = sa_ref[...] * scale
        sb = sb_ref[...] * scale
        half = ROT_DIM // 2
        for h in range(n_heads):
            t = acc[:, h * HEAD_DIM:(h + 1) * HEAD_DIM]
            o_ref[:, h * HEAD_DIM:(h + 1) * HEAD_DIM] = (
                t * c + pltpu.roll(t, half, 1) * sa + pltpu.roll(t, HEAD_DIM - half, 1) * sb)

    @pl.when(jnp.logical_not(is_rope))
    def _():
        o_ref[...] = acc


def _rope_tables(seq):
    half = ROT_DIM // 2
    pos = jnp.arange(seq, dtype=F32)
    inv = ROPE_THETA ** (-jnp.arange(0, ROT_DIM, 2, dtype=F32) / ROT_DIM)
    ang = pos[:, None] * inv[None, :]
    cos, sin = jnp.cos(ang), jnp.sin(ang)
    ones = jnp.ones((seq, HEAD_DIM - ROT_DIM), F32)
    zeros = jnp.zeros((seq, HEAD_DIM - ROT_DIM), F32)
    zh = jnp.zeros((seq, half), F32)
    c = jnp.concatenate([cos, cos, ones], -1)
    sa = jnp.concatenate([zh, sin, zeros], -1)
    sb = jnp.concatenate([-sin, zh, zeros], -1)
    return c, sa, sb


def _proj(x2, w_in_bf, seq, width):
    n, d = x2.shape
    e = w_in_bf.shape[1]
    n_groups = e // width
    c, sa, sb = _rope_tables(seq)
    tiles_per_seq = seq // PROJ_TM
    tab_spec = pl.BlockSpec((PROJ_TM, HEAD_DIM), lambda j, i: (i % tiles_per_seq, 0))
    return pl.pallas_call(
        functools.partial(_proj_kernel, n_heads=width // HEAD_DIM),
        grid=(n_groups, n // PROJ_TM),
        in_specs=[pl.BlockSpec((PROJ_TM, d), lambda j, i: (i, 0)),
                  pl.BlockSpec((d, width), lambda j, i: (0, j)),
                  tab_spec, tab_spec, tab_spec],
        out_specs=pl.BlockSpec((PROJ_TM, width), lambda j, i: (i, j)),
        out_shape=jax.ShapeDtypeStruct((n, e), F32),
        compiler_params=_cparams(("arbitrary", "arbitrary")),
        name="proj_rope",
    )(x2, w_in_bf, c, sa, sb)


def _lru_kernel(xl_ref, gl_ref, cw_ref, cb_ref, wa_ref, ba_ref, wx_ref, bx_ref, lam_ref,
                o_ref, xext_ref, hprev_ref, a_ref, b_ref, h_ref, *, block_w):
    tc, width = xl_ref.shape
    groups = tc // SUBLANES

    @pl.when(pl.program_id(1) == 0)
    def _():
        xext_ref[0:SUBLANES, :] = jnp.zeros((SUBLANES, width), F32)
        hprev_ref[...] = jnp.zeros_like(hprev_ref)

    x = xl_ref[...]
    xext_ref[SUBLANES:SUBLANES + tc, :] = x
    xc = x * cw_ref[CONV_W - 1:CONV_W, :] + cb_ref[...]
    for j in range(CONV_W - 1):
        back = CONV_W - 1 - j
        xc = xc + xext_ref[pl.ds(SUBLANES - back, tc), :] * cw_ref[j:j + 1, :]
    xext_ref[0:SUBLANES, :] = x[tc - SUBLANES:, :]

    lam = lam_ref[...]
    softplus_neg_lam = jnp.maximum(-lam, 0.0) + jnp.log1p(jnp.exp(-jnp.abs(lam)))
    rs, is_ = [], []
    for nb in range(width // block_w):
        xb = xc[:, nb * block_w:(nb + 1) * block_w].astype(BF16)
        rs.append(jnp.dot(xb, wa_ref[nb], preferred_element_type=F32))
        is_.append(jnp.dot(xb, wx_ref[nb], preferred_element_type=F32))
    r = jax.nn.sigmoid(jnp.concatenate(rs, axis=1) + ba_ref[...])
    ig = jax.nn.sigmoid(jnp.concatenate(is_, axis=1) + bx_ref[...])
    a = jnp.exp(-LRU_C * r * softplus_neg_lam)
    b = jnp.sqrt(1.0 - a * a) * (ig * xc)

    row = lax.broadcasted_iota(I32, (tc, width), 0) & (SUBLANES - 1)
    k = 1
    while k < SUBLANES:
        keep = row >= k
        a_sh = pltpu.roll(a, k, 0)
        b_sh = pltpu.roll(b, k, 0)
        b = jnp.where(keep, a * b_sh + b, b)
        a = jnp.where(keep, a * a_sh, a)
        k *= 2
    a_ref[...] = a.reshape(groups, SUBLANES, width)
    b_ref[...] = b.reshape(groups, SUBLANES, width)

    def carry(g, hprev):
        h = a_ref[g] * hprev + b_ref[g]
        h_ref[g] = h
        return h[SUBLANES - 1:SUBLANES, :]

    hprev_ref[...] = lax.fori_loop(0, groups, carry, hprev_ref[...])
    h = h_ref[...].reshape(tc, width)
    o_ref[...] = (h * jax.nn.gelu(gl_ref[...], approximate=True)).astype(o_ref.dtype)


def _rglru(proj3, conv_w, conv_b, wa_bf, b_a, wx_bf, b_x, lam, width):
    b, s, _ = proj3.shape
    block_w = width // LRU_BLOCKS
    vec = pl.BlockSpec((1, width), lambda bi, ci: (0, 0))
    wspec = pl.BlockSpec((LRU_BLOCKS, block_w, block_w), lambda bi, ci: (0, 0, 0))
    groups = LRU_TC // SUBLANES
    return pl.pallas_call(
        functools.partial(_lru_kernel, block_w=block_w),
        grid=(b, s // LRU_TC),
        in_specs=[pl.BlockSpec((None, LRU_TC, width), lambda bi, ci: (bi, ci, 0)),
                  pl.BlockSpec((None, LRU_TC, width), lambda bi, ci: (bi, ci, 1)),
                  pl.BlockSpec((CONV_W, width), lambda bi, ci: (0, 0)),
                  vec, wspec, vec, wspec, vec, vec],
        out_specs=pl.BlockSpec((None, LRU_TC, width), lambda bi, ci: (bi, ci, 0)),
        out_shape=jax.ShapeDtypeStruct((b, s, width), BF16),
        scratch_shapes=[pltpu.VMEM((LRU_TC + SUBLANES, width), F32),
                        pltpu.VMEM((1, width), F32),
                        pltpu.VMEM((groups, SUBLANES, width), F32),
                        pltpu.VMEM((groups, SUBLANES, width), F32),
                        pltpu.VMEM((groups, SUBLANES, width), F32)],
        compiler_params=_cparams(("arbitrary", "arbitrary")),
        name="rglru",
    )(proj3, proj3, conv_w, conv_b.reshape(1, width), wa_bf, b_a.reshape(1, width),
      wx_bf, b_x.reshape(1, width), lam.reshape(1, width))


def _attn_kernel(q_ref, k_ref, v_ref, o_ref, qs, ks, vs, acc_s, m_s, l_s, acc_n, m_n, l_n):
    seq = q_ref.shape[0]
    blk = ATTN_BLOCK
    nblk = seq // blk
    n_pat = len(DILATED_PATTERNS)

    @pl.when((pl.program_id(0) == 0) & (pl.program_id(1) == 0))
    def _():
        for p in range(n_pat):
            vs[p, :, HEAD_DIM:] = jnp.ones((seq + blk, HEAD_DIM), BF16)

    for p, (_, d) in enumerate(DILATED_PATTERNS):
        cls_len = seq // d
        ks[p, 0:blk, :] = jnp.zeros((blk, HEAD_DIM), BF16)
        vs[p, 0:blk, 0:HEAD_DIM] = jnp.zeros((blk, HEAD_DIM), BF16)
        for r in range(d):
            src = pl.ds(r, cls_len, stride=d) if d > 1 else pl.ds(0, seq)
            qs[p, r * cls_len:(r + 1) * cls_len, :] = q_ref[src, :].astype(BF16)
            ks[p, blk + r * cls_len:blk + (r + 1) * cls_len, :] = k_ref[src, :].astype(BF16)
            vs[p, blk + r * cls_len:blk + (r + 1) * cls_len, 0:HEAD_DIM] = v_ref[src, :].astype(BF16)

    qi = lax.broadcasted_iota(I32, (blk, 2 * blk), 0)
    kj = lax.broadcasted_iota(I32, (blk, 2 * blk), 1)
    band2 = (kj >= qi) & (kj <= qi + blk)
    in_block = kj >= blk
    qi1 = lax.broadcasted_iota(I32, (blk, blk), 0)
    kj1 = lax.broadcasted_iota(I32, (blk, blk), 1)
    causal1 = kj1 <= qi1
    nt = (((1,), (1,)), ((), ()))

    def one_block(p, row0, first):
        q = qs[p, pl.ds(row0, blk), :]
        if first is True:
            kk = ks[p, pl.ds(row0 + blk, blk), :]
            vv = vs[p, pl.ds(row0 + blk, blk), :]
            mask = causal1
        else:
            kk = ks[p, pl.ds(row0, 2 * blk), :]
            vv = vs[p, pl.ds(row0, 2 * blk), :]
            mask = band2 if first is False else band2 & (in_block | jnp.logical_not(first))
        s = lax.dot_general(q, kk, nt, preferred_element_type=F32)
        s = jnp.where(mask, s, -jnp.inf)
        m = jnp.max(s, axis=1, keepdims=True)
        e = jnp.exp(s - m)
        acc_l = jnp.dot(e.astype(BF16), vv, preferred_element_type=F32)
        acc_s[p, pl.ds(row0, blk), :] = acc_l[:, 0:HEAD_DIM]
        m_s[p, pl.ds(row0, blk), :] = jnp.broadcast_to(m, (blk, HEAD_DIM))
        l_s[p, pl.ds(row0, blk), :] = acc_l[:, HEAD_DIM:]

    def body(it, carry):
        for p, (window, d) in enumerate(DILATED_PATTERNS):
            assert window // d == blk
            blocks_per_class = (seq // d) // blk
            for j in range(ATTN_UNROLL):
                row0 = pl.multiple_of((it * ATTN_UNROLL + j) * blk, blk)
                if ATTN_UNROLL % blocks_per_class == 0:
                    first = j % blocks_per_class == 0
                else:
                    assert blocks_per_class % ATTN_UNROLL == 0
                    first = (it % (blocks_per_class // ATTN_UNROLL) == 0) if j == 0 else False
                one_block(p, row0, first)
        return carry

    lax.fori_loop(0, nblk // ATTN_UNROLL, body, 0)

    for p, (_, d) in enumerate(DILATED_PATTERNS):
        cls_len = seq // d
        for r in range(d):
            dst = pl.ds(r, cls_len, stride=d) if d > 1 else pl.ds(0, seq)
            acc_n[p, dst, :] = acc_s[p, r * cls_len:(r + 1) * cls_len, :]
            m_n[p, dst, :] = m_s[p, r * cls_len:(r + 1) * cls_len, :]
            l_n[p, dst, :] = l_s[p, r * cls_len:(r + 1) * cls_len, :]

    mx = m_n[0]
    for p in range(1, n_pat):
        mx = jnp.maximum(mx, m_n[p])
    num = jnp.zeros((seq, HEAD_DIM), F32)
    den = jnp.zeros((seq, HEAD_DIM), F32)
    for p in range(n_pat):
        w = jnp.exp(m_n[p] - mx)
        num = num + w * acc_n[p]
        den = den + w * l_n[p]
    o_ref[...] = (num / den).astype(o_ref.dtype)


def _attention(proj3, width):
    b, s, _ = proj3.shape
    n_heads = width // HEAD_DIM
    n_pat = len(DILATED_PATTERNS)
    col0 = 2 * width // HEAD_DIM

    def spec(group):
        return pl.BlockSpec((None, s, HEAD_DIM),
                            lambda bi, hi: (bi, 0, col0 + group * n_heads + hi))

    return pl.pallas_call(
        _attn_kernel,
        grid=(b, n_heads),
        in_specs=[spec(0), spec(1), spec(2)],
        out_specs=pl.BlockSpec((None, s, HEAD_DIM), lambda bi, hi: (bi, 0, hi)),
        out_shape=jax.ShapeDtypeStruct((b, s, width), BF16),
        scratch_shapes=[pltpu.VMEM((n_pat, s, HEAD_DIM), BF16),
                        pltpu.VMEM((n_pat, s + ATTN_BLOCK, HEAD_DIM), BF16),
                        pltpu.VMEM((n_pat, s + ATTN_BLOCK, 2 * HEAD_DIM), BF16)]
                       + [pltpu.VMEM((n_pat, s, HEAD_DIM), F32)] * 6,
        compiler_params=_cparams(("arbitrary", "arbitrary")),
        name="dilated_attention",
    )(proj3, proj3, proj3)


def _layer_norm(y, g, b):
    mu = jnp.mean(y, axis=-1, keepdims=True)
    yc = y - mu
    var = jnp.mean(yc * yc, axis=-1, keepdims=True)
    return yc * lax.rsqrt(var + LN_EPS) * g + b


def _first_index(hit, iota, size):
    return jnp.min(jnp.where(hit, iota, size), axis=0, keepdims=True)


def _mix_kernel(lru_ref, att_ref, x_ref, wo_ref, g_ref, b_ref, wrh_ref, wrl_ref, rb_ref, tri_ref,
                x1_ref, x1p_ref, idx_ref, gate_ref, rank_ref, cnt_ref, base_ref):
    tm = x_ref.shape[0]
    half = lru_ref.shape[1]

    @pl.when(pl.program_id(0) == 0)
    def _():
        base_ref[...] = jnp.zeros_like(base_ref)

    mix = jnp.dot(lru_ref[...], wo_ref[0:half, :], preferred_element_type=F32)
    mix = mix + jnp.dot(att_ref[...], wo_ref[half:, :], preferred_element_type=F32)
    x1 = _layer_norm(DN_ALPHA * x_ref[...] + mix, g_ref[...], b_ref[...])
    x1_ref[...] = x1
    _store_rows_as_tiles(x1p_ref, _pack_bf16_pair(x1[:, 0:half], x1[:, half:]))

    x_hi = x1.astype(BF16)
    x_lo = (x1 - x_hi.astype(F32)).astype(BF16)
    nt = (((1,), (1,)), ((), ()))
    logits = lax.dot_general(wrh_ref[...], x_hi, nt, preferred_element_type=F32)
    logits = logits + lax.dot_general(wrh_ref[...], x_lo, nt, preferred_element_type=F32)
    logits = logits + lax.dot_general(wrl_ref[...], x_hi, nt, preferred_element_type=F32)
    scores = jax.nn.sigmoid(logits)
    sel = scores + rb_ref[...]

    neg = -jnp.inf
    gi = lax.broadcasted_iota(I32, (GROUP_SIZE, tm), 0)
    grp_rows = []
    for g in range(N_GROUPS):
        blk = sel[g * GROUP_SIZE:(g + 1) * GROUP_SIZE, :]
        m1 = jnp.max(blk, axis=0, keepdims=True)
        f1 = _first_index(blk == m1, gi, GROUP_SIZE)
        m2 = jnp.max(jnp.where(gi == f1, neg, blk), axis=0, keepdims=True)
        grp_rows.append(m1 + m2)
    ni = lax.broadcasted_iota(I32, (N_GROUPS, tm), 0)
    gscore = jnp.zeros((N_GROUPS, tm), F32)
    for g in range(N_GROUPS):
        gscore = jnp.where(ni == g, grp_rows[g], gscore)
    gmask = jnp.zeros((N_GROUPS, tm), jnp.bool_)
    for _ in range(TOPK_GROUPS):
        m = jnp.max(gscore, axis=0, keepdims=True)
        f = _first_index(gscore == m, ni, N_GROUPS)
        hit = ni == f
        gmask = gmask | hit
        gscore = jnp.where(hit, neg, gscore)
    emask = jnp.concatenate(
        [jnp.broadcast_to(gmask[g:g + 1, :], (GROUP_SIZE, tm)) for g in range(N_GROUPS)], axis=0)
    masked = jnp.where(emask, sel, neg)

    ei = lax.broadcasted_iota(I32, (N_EXPERTS, tm), 0)
    chosen = jnp.zeros((N_EXPERTS, tm), jnp.bool_)
    picks, gates = [], []
    for _ in range(TOP_K):
        m = jnp.max(masked, axis=0, keepdims=True)
        f = _first_index(masked == m, ei, N_EXPERTS)
        hit = ei == f
        picks.append(f)
        gates.append(jnp.sum(jnp.where(hit, scores, 0.0), axis=0, keepdims=True))
        chosen = chosen | hit
        masked = jnp.where(hit, neg, masked)
    gsum = gates[0]
    for k in range(1, TOP_K):
        gsum = gsum + gates[k]

    chosen_f = jnp.where(chosen, 1.0, 0.0)
    incl = jnp.dot(chosen_f.astype(BF16), tri_ref[...], preferred_element_type=F32)
    rank_excl = base_ref[:, 0:1] + incl - chosen_f
    base_ref[...] = base_ref[...] + jnp.sum(chosen_f, axis=1, keepdims=True)
    cnt_ref[...] = base_ref[...]
    for k in range(TOP_K):
        idx_ref[k:k + 1, :] = picks[k]
        gate_ref[k:k + 1, :] = gates[k] / gsum * ROUTED_SCALE
        rk = jnp.sum(jnp.where(ei == picks[k], rank_excl, 0.0), axis=0, keepdims=True)
        rank_ref[k:k + 1, :] = rk.astype(I32)


def _mix_route(lru, att, x2, wo_bf, ln_g, ln_b, w_router, router_bias):
    n, d = x2.shape
    half = lru.shape[1]
    wr_t = w_router.T
    wr_hi = wr_t.astype(BF16)
    wr_lo = (wr_t - wr_hi.astype(F32)).astype(BF16)
    assert half == TILE_WORDS
    tri = jnp.triu(jnp.ones((MIX_TM, MIX_TM), BF16))
    const = lambda shape: pl.BlockSpec(shape, lambda i: (0,) * len(shape))
    tok = lambda dtype: jax.ShapeDtypeStruct((TOP_K, n), dtype)
    tok_spec = pl.BlockSpec((TOP_K, MIX_TM), lambda i: (0, i))
    return pl.pallas_call(
        _mix_kernel,
        grid=(n // MIX_TM,),
        in_specs=[pl.BlockSpec((MIX_TM, half), lambda i: (i, 0)),
                  pl.BlockSpec((MIX_TM, half), lambda i: (i, 0)),
                  pl.BlockSpec((MIX_TM, d), lambda i: (i, 0)),
                  const((d, d)), const((1, d)), const((1, d)),
                  const((N_EXPERTS, d)), const((N_EXPERTS, d)), const((N_EXPERTS, 1)),
                  const((MIX_TM, MIX_TM))],
        out_specs=[pl.BlockSpec((MIX_TM, d), lambda i: (i, 0)),
                   pl.BlockSpec((MIX_TM * SUBLANES, LANES), lambda i: (i, 0)),
                   tok_spec, tok_spec, tok_spec, const((N_EXPERTS, HEAD_DIM))],
        out_shape=[jax.ShapeDtypeStruct((n, d), F32), jax.ShapeDtypeStruct((n * SUBLANES, LANES), U32),
                   tok(I32), tok(F32), tok(I32),
                   jax.ShapeDtypeStruct((N_EXPERTS, HEAD_DIM), F32)],
        scratch_shapes=[pltpu.VMEM((N_EXPERTS, HEAD_DIM), F32)],
        compiler_params=_cparams(("arbitrary",)),
        name="mix_ln_route",
    )(lru, att, x2, wo_bf, ln_g.reshape(1, d), ln_b.reshape(1, d), wr_hi, wr_lo,
      router_bias.reshape(N_EXPERTS, 1), tri)


def _dispatch_kernel(start_ref, cnt_ref, pos_ref, x1_ref, zero_ref, xs_hbm, sem):
    i = pl.program_id(0)
    tt = pos_ref.shape[1]

    def row_copy(t, k):
        return pltpu.make_async_copy(x1_ref.at[t], xs_hbm.at[pos_ref[k, t]], sem)

    def issue(t, c):
        for k in range(TOP_K):
            row_copy(t, k).start()
        return c

    def drain(t, c):
        for k in range(TOP_K):
            row_copy(t, k).wait()
        return c

    lax.fori_loop(0, tt, issue, 0)
    lax.fori_loop(0, tt, drain, 0)

    @pl.when(i == 0)
    def _():
        def pad_copy(r, rows):
            return pltpu.make_async_copy(zero_ref.at[pl.ds(0, rows)], xs_hbm.at[pl.ds(r, rows)], sem)

        def per_expert(e, c):
            lo = start_ref[e] + cnt_ref[e]
            hi = start_ref[e + 1]
            chunks = (hi - lo) // ZERO_ROWS
            mid = lo + chunks * ZERO_ROWS
            big = lambda j: pad_copy(lo + j * ZERO_ROWS, ZERO_ROWS)
            lax.fori_loop(0, chunks, lambda j, cc: (big(j).start(), cc)[1], 0)
            lax.fori_loop(mid, hi, lambda r, cc: (pad_copy(r, 1).start(), cc)[1], 0)
            lax.fori_loop(0, chunks, lambda j, cc: (big(j).wait(), cc)[1], 0)
            lax.fori_loop(mid, hi, lambda r, cc: (pad_copy(r, 1).wait(), cc)[1], 0)
            return c

        lax.fori_loop(0, N_EXPERTS, per_expert, 0)


def _dispatch(x1p, pos_t, start, counts, total_rows):
    n = x1p.shape[0]
    smem_tok = pl.BlockSpec((TOP_K, DISPATCH_TT), lambda i, *_: (0, i), memory_space=pltpu.SMEM)
    grid_spec = pltpu.PrefetchScalarGridSpec(
        num_scalar_prefetch=2,
        grid=(n // DISPATCH_TT,),
        in_specs=[smem_tok,
                  pl.BlockSpec((DISPATCH_TT, SUBLANES, LANES), lambda i, *_: (i, 0, 0)),
                  pl.BlockSpec((ZERO_ROWS, SUBLANES, LANES), lambda i, *_: (0, 0, 0))],
        out_specs=pl.BlockSpec(memory_space=pl.ANY),
        scratch_shapes=[pltpu.SemaphoreType.DMA(())],
    )
    return pl.pallas_call(
        _dispatch_kernel,
        grid_spec=grid_spec,
        out_shape=jax.ShapeDtypeStruct((total_rows, SUBLANES, LANES), x1p.dtype),
        compiler_params=_cparams(("arbitrary",)),
        name="dispatch",
    )(start, counts, pos_t, x1p, jnp.zeros((ZERO_ROWS, SUBLANES, LANES), x1p.dtype))


def _expert_kernel(be_ref, used_ref, nxt_ref, slot_ref, xs_ref, wg_hbm, wu_hbm, wd_hbm, ys_ref,
                   wg_f32, wu_f32, wd_f32, wg_bf, wu_bf, wd_bf, sems):
    i = pl.program_id(0)
    e = be_ref[i]
    slot = slot_ref[i]
    changed = (i == 0) | (e != be_ref[jnp.maximum(i - 1, 0)])

    def weight_copies(expert, s):
        return (pltpu.make_async_copy(wg_hbm.at[expert], wg_f32.at[s], sems.at[0, s]),
                pltpu.make_async_copy(wu_hbm.at[expert], wu_f32.at[s], sems.at[1, s]),
                pltpu.make_async_copy(wd_hbm.at[expert], wd_f32.at[s], sems.at[2, s]))

    @pl.when(i == 0)
    def _():
        for c in weight_copies(e, slot):
            c.start()

    @pl.when(changed)
    def _():
        for c in weight_copies(e, slot):
            c.wait()
        nxt = nxt_ref[i]

        @pl.when(nxt != e)
        def _():
            for c in weight_copies(nxt, 1 - slot):
                c.start()

        wg_bf[...] = wg_f32[slot].astype(BF16)
        wu_bf[...] = wu_f32[slot].astype(BF16)
        wd_bf[...] = wd_f32[slot].astype(BF16)

    @pl.when(i < used_ref[0])
    def _():
        half = TILE_WORDS
        lo, hi = _unpack_bf16_pair(_load_rows_from_tiles(xs_ref, 0, EXPERT_ROWS))
        lo, hi = lo.astype(BF16), hi.astype(BF16)
        g = (jnp.dot(lo, wg_bf[0:half, :], preferred_element_type=F32)
             + jnp.dot(hi, wg_bf[half:, :], preferred_element_type=F32))
        u = (jnp.dot(lo, wu_bf[0:half, :], preferred_element_type=F32)
             + jnp.dot(hi, wu_bf[half:, :], preferred_element_type=F32))
        h = (g * jax.nn.sigmoid(g) * u).astype(BF16)
        y = jnp.dot(h, wd_bf[...], preferred_element_type=F32)
        _store_rows_as_tiles(ys_ref, _pack_bf16_pair(y[:, 0:half], y[:, half:]))

    @pl.when(i >= used_ref[0])
    def _():
        ys_ref[...] = jnp.zeros_like(ys_ref)


def _experts(xs, blk_expert, used, w_gate, w_up, w_down):
    rows = xs.shape[0] // SUBLANES
    d = 2 * TILE_WORDS
    de = w_gate.shape[2]
    nblk = rows // EXPERT_ROWS
    later = blk_expert[None, :] > blk_expert[:, None]
    nxt = jnp.min(jnp.where(later, blk_expert[None, :], N_EXPERTS), axis=1)
    nxt = jnp.where(nxt == N_EXPERTS, blk_expert, nxt)
    is_new = jnp.concatenate([jnp.zeros((1,), I32), (blk_expert[1:] != blk_expert[:-1]).astype(I32)])
    slot = jnp.cumsum(is_new) % 2
    grid_spec = pltpu.PrefetchScalarGridSpec(
        num_scalar_prefetch=4,
        grid=(nblk,),
        in_specs=[pl.BlockSpec((EXPERT_ROWS * SUBLANES, LANES),
                               lambda i, be, used, *_: (jnp.minimum(i, used[0] - 1), 0)),
                  pl.BlockSpec(memory_space=pl.ANY), pl.BlockSpec(memory_space=pl.ANY),
                  pl.BlockSpec(memory_space=pl.ANY)],
        out_specs=pl.BlockSpec((EXPERT_ROWS * SUBLANES, LANES), lambda i, *_: (i, 0)),
        scratch_shapes=[pltpu.VMEM((2, d, de), F32), pltpu.VMEM((2, d, de), F32),
                        pltpu.VMEM((2, de, d), F32),
                        pltpu.VMEM((d, de), BF16), pltpu.VMEM((d, de), BF16),
                        pltpu.VMEM((de, d), BF16), pltpu.SemaphoreType.DMA((3, 2))],
    )
    return pl.pallas_call(
        _expert_kernel,
        grid_spec=grid_spec,
        out_shape=jax.ShapeDtypeStruct((rows * SUBLANES, LANES), U32),
        compiler_params=_cparams(("arbitrary",)),
        name="experts",
    )(blk_expert, used, nxt.astype(I32), slot.astype(I32), xs, w_gate, w_up, w_down)


def _combine_kernel(pos_ref, gate_ref, x1_ref, ys_hbm, wsg_ref, wsu_ref,
                    wsd_ref, g_ref, b_ref, o_ref, buf, sem):
    tt = x1_ref.shape[0]

    def row_copy(t, k):
        dst = pl.multiple_of((k * tt + t) * SUBLANES, SUBLANES)
        return pltpu.make_async_copy(ys_hbm.at[pos_ref[k, t]], buf.at[pl.ds(dst, SUBLANES)], sem)

    def issue(t, c):
        for k in range(TOP_K):
            row_copy(t, k).start()
        return c

    def drain(t, c):
        for k in range(TOP_K):
            row_copy(t, k).wait()
        return c

    lax.fori_loop(0, tt, issue, 0)

    x1 = x1_ref[...]
    xb = x1.astype(BF16)
    g = jnp.dot(xb, wsg_ref[...], preferred_element_type=F32)
    u = jnp.dot(xb, wsu_ref[...], preferred_element_type=F32)
    h = (g * jax.nn.sigmoid(g) * u).astype(BF16)
    y = DN_ALPHA * x1 + jnp.dot(h, wsd_ref[...], preferred_element_type=F32)

    lax.fori_loop(0, tt, drain, 0)
    gate = gate_ref[...]
    r_lo = jnp.zeros((tt, TILE_WORDS), F32)
    r_hi = jnp.zeros((tt, TILE_WORDS), F32)
    for k in range(TOP_K):
        lo, hi = _unpack_bf16_pair(_load_rows_from_tiles(buf, k * tt, tt))
        r_lo = r_lo + lo * gate[:, k:k + 1]
        r_hi = r_hi + hi * gate[:, k:k + 1]
    y = y + jnp.concatenate([r_lo, r_hi], axis=1)
    o_ref[...] = _layer_norm(y, g_ref[...], b_ref[...])


def _combine(x1, ys, pos_t, gate_nk, wsg_bf, wsu_bf, wsd_bf, ln_g, ln_b):
    n, d = x1.shape
    de = wsg_bf.shape[1]
    smem_tok = pl.BlockSpec((TOP_K, COMBINE_TT), lambda i, *_: (0, i), memory_space=pltpu.SMEM)
    const = lambda shape: pl.BlockSpec(shape, lambda i, *_: (0,) * len(shape))
    grid_spec = pltpu.PrefetchScalarGridSpec(
        num_scalar_prefetch=0,
        grid=(n // COMBINE_TT,),
        in_specs=[smem_tok,
                  pl.BlockSpec((COMBINE_TT, TOP_K), lambda i, *_: (i, 0)),
                  pl.BlockSpec((COMBINE_TT, d), lambda i, *_: (i, 0)),
                  pl.BlockSpec(memory_space=pl.ANY),
                  const((d, de)), const((d, de)), const((de, d)), const((1, d)), const((1, d))],
        out_specs=pl.BlockSpec((COMBINE_TT, d), lambda i, *_: (i, 0)),
        scratch_shapes=[pltpu.VMEM((TOP_K * COMBINE_TT * SUBLANES, LANES), U32),
                        pltpu.SemaphoreType.DMA(())],
    )
    return pl.pallas_call(
        _combine_kernel,
        grid_spec=grid_spec,
        out_shape=jax.ShapeDtypeStruct((n, d), F32),
        compiler_params=_cparams(("arbitrary",)),
        name="combine_ln",
    )(pos_t, gate_nk, x1, ys, wsg_bf, wsu_bf, wsd_bf, ln_g.reshape(1, d), ln_b.reshape(1, d))


def _block_layout(counts_f, n_assign):
    counts = counts_f.astype(I32)
    padded = (counts + EXPERT_ROWS - 1) // EXPERT_ROWS * EXPERT_ROWS
    end = jnp.cumsum(padded)
    total_rows = n_assign + N_EXPERTS * EXPERT_ROWS
    start = jnp.concatenate([end - padded, jnp.full((1,), total_rows, I32)])
    nblk = total_rows // EXPERT_ROWS
    used = end[-1:] // EXPERT_ROWS
    blk_start = jnp.minimum(jnp.arange(nblk, dtype=I32), used[0] - 1) * EXPERT_ROWS
    blk_expert = jnp.sum((end[None, :] <= blk_start[:, None]).astype(I32), axis=1)
    blk_expert = jnp.minimum(blk_expert, N_EXPERTS - 1)
    return start, counts, blk_expert, used.astype(I32), total_rows


def kernel(x, w_in, conv_w, conv_b, w_rg_a, b_rg_a, w_rg_x, b_rg_x, lru_lambda, w_out, ln1_g, ln1_b,
           w_router, router_bias, w_gate, w_up, w_down, ws_gate, ws_up, ws_down, ln2_g, ln2_b):
    b, s, d = x.shape
    n = b * s
    width = d // 2
    for l in range(DEPTH):
        x2 = x.reshape(n, d)
        proj = _proj(x2, w_in[l].astype(BF16), s, width)
        proj3 = proj.reshape(b, s, proj.shape[1])
        lru = _rglru(proj3, conv_w[l], conv_b[l], w_rg_a[l].astype(BF16), b_rg_a[l],
                     w_rg_x[l].astype(BF16), b_rg_x[l], lru_lambda[l], width)
        att = _attention(proj3, width)
        x1, x1p, idx_t, gate_t, rank_t, cnt = _mix_route(
            lru.reshape(n, width), att.reshape(n, width), x2, w_out[l].astype(BF16),
            ln1_g[l], ln1_b[l], w_router[l], router_bias[l])
        start, counts, blk_expert, used, total_rows = _block_layout(cnt[:, 0], n * TOP_K)
        onehot = idx_t[:, :, None] == jnp.arange(N_EXPERTS, dtype=I32)[None, None, :]
        pos_t = jnp.sum(jnp.where(onehot, start[None, None, :N_EXPERTS], 0), axis=2) + rank_t
        xs = _dispatch(x1p.reshape(n, SUBLANES, LANES), pos_t, start, counts, total_rows)
        ys = _experts(xs.reshape(total_rows * SUBLANES, LANES), blk_expert, used,
                      w_gate[l], w_up[l], w_down[l])
        out = _combine(x1, ys.reshape(total_rows, SUBLANES, LANES), pos_t, gate_t.T,
                       ws_gate[l].astype(BF16),
                       ws_up[l].astype(BF16), ws_down[l].astype(BF16), ln2_g[l], ln2_b[l])
        x = out.reshape(b, s, d)
    return x
```

```python
import functools

import jax
import jax.numpy as jnp
from jax import lax
from jax.experimental import pallas as pl
from jax.experimental.pallas import tpu as pltpu

F32 = jnp.float32
BF16 = jnp.bfloat16
I32 = jnp.int32
U32 = jnp.uint32

LRU_BLOCKS = 8
CONV_W = 4
LRU_C = 8.0
HEAD_DIM = 128
ROT_DIM = HEAD_DIM // 4
ROPE_THETA = 500000.0
DILATED_PATTERNS = ((128, 1), (512, 4), (2048, 16))
ATTN_BLOCK = 128
N_EXPERTS = 64
N_GROUPS = 8
GROUP_SIZE = N_EXPERTS // N_GROUPS
TOPK_GROUPS = 4
TOP_K = 8
ROUTED_SCALE = 2.5
DEPTH = 1
DN_ALPHA = (2 * DEPTH) ** 0.25
LN_EPS = 1e-5

PROJ_TM = 1024
LRU_TC = 512
MIX_TM = 512
ATTN_UNROLL = 8
EXPERT_ROWS = 512
DISPATCH_TT = 512
COMBINE_TT = 256
ZERO_ROWS = 64
SUBLANES = 8
LANES = 128
TILE_WORDS = SUBLANES * LANES
VMEM_LIMIT = 56 * 1024 * 1024


def _pack_bf16_pair(lo, hi):
    lo_bits = lax.bitcast_convert_type(lo.astype(BF16).astype(F32), U32)
    hi_bits = lax.bitcast_convert_type(hi.astype(BF16).astype(F32), U32)
    return (lo_bits >> 16) | (hi_bits & jnp.uint32(0xFFFF0000))


def _unpack_bf16_pair(packed):
    lo = lax.bitcast_convert_type(packed << 16, F32)
    hi = lax.bitcast_convert_type(packed & jnp.uint32(0xFFFF0000), F32)
    return lo, hi


def _store_rows_as_tiles(ref, words):
    rows = words.shape[0]
    for s in range(SUBLANES):
        ref[pl.ds(s, rows, stride=SUBLANES), :] = words[:, s * LANES:(s + 1) * LANES]


def _load_rows_from_tiles(ref, row0, rows):
    return jnp.concatenate(
        [ref[pl.ds(row0 * SUBLANES + s, rows, stride=SUBLANES), :] for s in range(SUBLANES)], axis=1)


def _cparams(sem):
    return pltpu.CompilerParams(dimension_semantics=sem, vmem_limit_bytes=VMEM_LIMIT)


def _proj_kernel(x_ref, w_ref, c_ref, sa_ref, sb_ref, o_ref, *, n_heads):
    j = pl.program_id(0)
    acc = jnp.dot(x_ref[...].astype(BF16), w_ref[...], preferred_element_type=F32)
    is_rope = (j == 2) | (j == 3)

    @pl.when(is_rope)
    def _():
        scale = jnp.where(j == 2, HEAD_DIM ** -0.5, 1.0).astype(F32)
        c = c_ref[...] * scale
        sa = sa_ref[...] * scale
        sb = sb_ref[...] * scale
        half = ROT_DIM // 2
        for h in range(n_heads):
            t = acc[:, h * HEAD_DIM:(h + 1) * HEAD_DIM]
            o_ref[:, h * HEAD_DIM:(h + 1) * HEAD_DIM] = (
                t * c + pltpu.roll(t, half, 1) * sa + pltpu.roll(t, HEAD_DIM - half, 1) * sb)

    @pl.when(jnp.logical_not(is_rope))
    def _():
        o_ref[...] = acc


def _rope_tables(seq):
    half = ROT_DIM // 2
    pos = jnp.arange(seq, dtype=F32)
    inv = ROPE_THETA ** (-jnp.arange(0, ROT_DIM, 2, dtype=F32) / ROT_DIM)
    ang = pos[:, None] * inv[None, :]
    cos, sin = jnp.cos(ang), jnp.sin(ang)
    ones = jnp.ones((seq, HEAD_DIM - ROT_DIM), F32)
    zeros = jnp.zeros((seq, HEAD_DIM - ROT_DIM), F32)
    zh = jnp.zeros((seq, half), F32)
    c = jnp.concatenate([cos, cos, ones], -1)
    sa = jnp.concatenate([zh, sin, zeros], -1)
    sb = jnp.concatenate([-sin, zh, zeros], -1)
    return c, sa, sb


def _proj(x2, w_in_bf, seq, width):
    n, d = x2.shape
    e = w_in_bf.shape[1]
    n_groups = e // width
    c, sa, sb = _rope_tables(seq)
    tiles_per_seq = seq // PROJ_TM
    tab_spec = pl.BlockSpec((PROJ_TM, HEAD_DIM), lambda j, i: (i % tiles_per_seq, 0))
    return pl.pallas_call(
        functools.partial(_proj_kernel, n_heads=width // HEAD_DIM),
        grid=(n_groups, n // PROJ_TM),
        in_specs=[pl.BlockSpec((PROJ_TM, d), lambda j, i: (i, 0)),
                  pl.BlockSpec((d, width), lambda j, i: (0, j)),
                  tab_spec, tab_spec, tab_spec],
        out_specs=pl.BlockSpec((PROJ_TM, width), lambda j, i: (i, j)),
        out_shape=jax.ShapeDtypeStruct((n, e), F32),
        compiler_params=_cparams(("arbitrary", "arbitrary")),
        name="proj_rope",
    )(x2, w_in_bf, c, sa, sb)


def _lru_kernel(xl_ref, gl_ref, cw_ref, cb_ref, wa_ref, ba_ref, wx_ref, bx_ref, lam_ref,
                o_ref, xext_ref, hprev_ref, a_ref, b_ref, h_ref, *, block_w):
    tc, width = xl_ref.shape
    groups = tc // SUBLANES

    @pl.when(pl.program_id(1) == 0)
    def _():
        xext_ref[0:SUBLANES, :] = jnp.zeros((SUBLANES, width), F32)
        hprev_ref[...] = jnp.zeros_like(hprev_ref)

    x = xl_ref[...]
    xext_ref[SUBLANES:SUBLANES + tc, :] = x
    xc = x * cw_ref[CONV_W - 1:CONV_W, :] + cb_ref[...]
    for j in range(CONV_W - 1):
        back = CONV_W - 1 - j
        xc = xc + xext_ref[pl.ds(SUBLANES - back, tc), :] * cw_ref[j:j + 1, :]
    xext_ref[0:SUBLANES, :] = x[tc - SUBLANES:, :]

    lam = lam_ref[...]
    softplus_neg_lam = jnp.maximum(-lam, 0.0) + jnp.log1p(jnp.exp(-jnp.abs(lam)))
    rs, is_ = [], []
    for nb in range(width // block_w):
        xb = xc[:, nb * block_w:(nb + 1) * block_w].astype(BF16)
        rs.append(jnp.dot(xb, wa_ref[nb], preferred_element_type=F32))
        is_.append(jnp.dot(xb, wx_ref[nb], preferred_element_type=F32))
    r = jax.nn.sigmoid(jnp.concatenate(rs, axis=1) + ba_ref[...])
    ig = jax.nn.sigmoid(jnp.concatenate(is_, axis=1) + bx_ref[...])
    a = jnp.exp(-LRU_C * r * softplus_neg_lam)
    b = jnp.sqrt(1.0 - a * a) * (ig * xc)

    row = lax.broadcasted_iota(I32, (tc, width), 0) & (SUBLANES - 1)
    k = 1
    while k < SUBLANES:
        keep = row >= k
        a_sh = pltpu.roll(a, k, 0)
        b_sh = pltpu.roll(b, k, 0)
        b = jnp.where(keep, a * b_sh + b, b)
        a = jnp.where(keep, a * a_sh, a)
        k *= 2
    a_ref[...] = a.reshape(groups, SUBLANES, width)
    b_ref[...] = b.reshape(groups, SUBLANES, width)

    def carry(g, hprev):
        h = a_ref[g] * hprev + b_ref[g]
        h_ref[g] = h
        return h[SUBLANES - 1:SUBLANES, :]

    hprev_ref[...] = lax.fori_loop(0, groups, carry, hprev_ref[...])
    h = h_ref[...].reshape(tc, width)
    o_ref[...] = (h * jax.nn.gelu(gl_ref[...], approximate=True)).astype(o_ref.dtype)


def _rglru(proj3, conv_w, conv_b, wa_bf, b_a, wx_bf, b_x, lam, width):
    b, s, _ = proj3.shape
    block_w = width // LRU_BLOCKS
    vec = pl.BlockSpec((1, width), lambda bi, ci: (0, 0))
    wspec = pl.BlockSpec((LRU_BLOCKS, block_w, block_w), lambda bi, ci: (0, 0, 0))
    groups = LRU_TC // SUBLANES
    return pl.pallas_call(
        functools.partial(_lru_kernel, block_w=block_w),
        grid=(b, s // LRU_TC),
        in_specs=[pl.BlockSpec((None, LRU_TC, width), lambda bi, ci: (bi, ci, 0)),
                  pl.BlockSpec((None, LRU_TC, width), lambda bi, ci: (bi, ci, 1)),
                  pl.BlockSpec((CONV_W, width), lambda bi, ci: (0, 0)),
                  vec, wspec, vec, wspec, vec, vec],
        out_specs=pl.BlockSpec((None, LRU_TC, width), lambda bi, ci: (bi, ci, 0)),
        out_shape=jax.ShapeDtypeStruct((b, s, width), BF16),
        scratch_shapes=[pltpu.VMEM((LRU_TC + SUBLANES, width), F32),
                        pltpu.VMEM((1, width), F32),
                        pltpu.VMEM((groups, SUBLANES, width), F32),
                        pltpu.VMEM((groups, SUBLANES, width), F32),
                        pltpu.VMEM((groups, SUBLANES, width), F32)],
        compiler_params=_cparams(("arbitrary", "arbitrary")),
        name="rglru",
    )(proj3, proj3, conv_w, conv_b.reshape(1, width), wa_bf, b_a.reshape(1, width),
      wx_bf, b_x.reshape(1, width), lam.reshape(1, width))


def _attn_kernel(q_ref, k_ref, v_ref, o_ref, qs, ks, vs, acc_s, m_s, l_s, acc_n, m_n, l_n):
    seq = q_ref.shape[0]
    blk = ATTN_BLOCK
    nblk = seq // blk
    n_pat = len(DILATED_PATTERNS)

    @pl.when((pl.program_id(0) == 0) & (pl.program_id(1) == 0))
    def _():
        for p in range(n_pat):
            vs[p, :, HEAD_DIM:] = jnp.ones((seq + blk, HEAD_DIM), BF16)

    for p, (_, d) in enumerate(DILATED_PATTERNS):
        cls_len = seq // d
        ks[p, 0:blk, :] = jnp.zeros((blk, HEAD_DIM), BF16)
        vs[p, 0:blk, 0:HEAD_DIM] = jnp.zeros((blk, HEAD_DIM), BF16)
        for r in range(d):
            src = pl.ds(r, cls_len, stride=d) if d > 1 else pl.ds(0, seq)
            qs[p, r * cls_len:(r + 1) * cls_len, :] = q_ref[src, :].astype(BF16)
            ks[p, blk + r * cls_len:blk + (r + 1) * cls_len, :] = k_ref[src, :].astype(BF16)
            vs[p, blk + r * cls_len:blk + (r + 1) * cls_len, 0:HEAD_DIM] = v_ref[src, :].astype(BF16)

    qi = lax.broadcasted_iota(I32, (blk, 2 * blk), 0)
    kj = lax.broadcasted_iota(I32, (blk, 2 * blk), 1)
    band2 = (kj >= qi) & (kj <= qi + blk)
    in_block = kj >= blk
    qi1 = lax.broadcasted_iota(I32, (blk, blk), 0)
    kj1 = lax.broadcasted_iota(I32, (blk, blk), 1)
    causal1 = kj1 <= qi1
    nt = (((1,), (1,)), ((), ()))

    def one_block(p, row0, first):
        q = qs[p, pl.ds(row0, blk), :]
        if first is True:
            kk = ks[p, pl.ds(row0 + blk, blk), :]
            vv = vs[p, pl.ds(row0 + blk, blk), :]
            mask = causal1
        else:
            kk = ks[p, pl.ds(row0, 2 * blk), :]
            vv = vs[p, pl.ds(row0, 2 * blk), :]
            mask = band2 if first is False else band2 & (in_block | jnp.logical_not(first))
        s = lax.dot_general(q, kk, nt, preferred_element_type=F32)
        s = jnp.where(mask, s, -jnp.inf)
        m = jnp.max(s, axis=1, keepdims=True)
        e = jnp.exp(s - m)
        acc_l = jnp.dot(e.astype(BF16), vv, preferred_element_type=F32)
        acc_s[p, pl.ds(row0, blk), :] = acc_l[:, 0:HEAD_DIM]
        m_s[p, pl.ds(row0, blk), :] = jnp.broadcast_to(m, (blk, HEAD_DIM))
        l_s[p, pl.ds(row0, blk), :] = acc_l[:, HEAD_DIM:]

    def body(it, carry):
        for p, (window, d) in enumerate(DILATED_PATTERNS):
            assert window // d == blk
            blocks_per_class = (seq // d) // blk
            for j in range(ATTN_UNROLL):
                row0 = pl.multiple_of((it * ATTN_UNROLL + j) * blk, blk)
                if ATTN_UNROLL % blocks_per_class == 0:
                    first = j % blocks_per_class == 0
                else:
                    assert blocks_per_class % ATTN_UNROLL == 0
                    first = (it % (blocks_per_class // ATTN_UNROLL) == 0) if j == 0 else False
                one_block(p, row0, first)
        return carry

    lax.fori_loop(0, nblk // ATTN_UNROLL, body, 0)

    for p, (_, d) in enumerate(DILATED_PATTERNS):
        cls_len = seq // d
        for r in range(d):
            dst = pl.ds(r, cls_len, stride=d) if d > 1 else pl.ds(0, seq)
            acc_n[p, dst, :] = acc_s[p, r * cls_len:(r + 1) * cls_len, :]
            m_n[p, dst, :] = m_s[p, r * cls_len:(r + 1) * cls_len, :]
            l_n[p, dst, :] = l_s[p, r * cls_len:(r + 1) * cls_len, :]

    mx = m_n[0]
    for p in range(1, n_pat):
        mx = jnp.maximum(mx, m_n[p])
    num = jnp.zeros((seq, HEAD_DIM), F32)
    den = jnp.zeros((seq, HEAD_DIM), F32)
    for p in range(n_pat):
        w = jnp.exp(m_n[p] - mx)
        num = num + w * acc_n[p]
        den = den + w * l_n[p]
    o_ref[...] = (num / den).astype(o_ref.dtype)


def _attention(proj3, width):
    b, s, _ = proj3.shape
    n_heads = width // HEAD_DIM
    n_pat = len(DILATED_PATTERNS)
    col0 = 2 * width // HEAD_DIM

    def spec(group):
        return pl.BlockSpec((None, s, HEAD_DIM),
                            lambda bi, hi: (bi, 0, col0 + group * n_heads + hi))

    return pl.pallas_call(
        _attn_kernel,
        grid=(b, n_heads),
        in_specs=[spec(0), spec(1), spec(2)],
        out_specs=pl.BlockSpec((None, s, HEAD_DIM), lambda bi, hi: (bi, 0, hi)),
        out_shape=jax.ShapeDtypeStruct((b, s, width), BF16),
        scratch_shapes=[pltpu.VMEM((n_pat, s, HEAD_DIM), BF16),
                        pltpu.VMEM((n_pat, s + ATTN_BLOCK, HEAD_DIM), BF16),
                        pltpu.VMEM((n_pat, s + ATTN_BLOCK, 2 * HEAD_DIM), BF16)]
                       + [pltpu.VMEM((n_pat, s, HEAD_DIM), F32)] * 6,
        compiler_params=_cparams(("arbitrary", "arbitrary")),
        name="dilated_attention",
    )(proj3, proj3, proj3)


def _layer_norm(y, g, b):
    mu = jnp.mean(y, axis=-1, keepdims=True)
    yc = y - mu
    var = jnp.mean(yc * yc, axis=-1, keepdims=True)
    return yc * lax.rsqrt(var + LN_EPS) * g + b


def _first_index(hit, iota, size):
    return jnp.min(jnp.where(hit, iota, size), axis=0, keepdims=True)


def _mix_kernel(lru_ref, att_ref, x_ref, wo_ref, g_ref, b_ref, wrh_ref, wrl_ref, rb_ref, tri_ref,
                x1_ref, x1p_ref, idx_ref, gate_ref, rank_ref, cnt_ref, base_ref):
    tm = x_ref.shape[0]
    half = lru_ref.shape[1]

    @pl.when(pl.program_id(0) == 0)
    def _():
        base_ref[...] = jnp.zeros_like(base_ref)

    mix = jnp.dot(lru_ref[...], wo_ref[0:half, :], preferred_element_type=F32)
    mix = mix + jnp.dot(att_ref[...], wo_ref[half:, :], preferred_element_type=F32)
    x1 = _layer_norm(DN_ALPHA * x_ref[...] + mix, g_ref[...], b_ref[...])
    x1_ref[...] = x1
    _store_rows_as_tiles(x1p_ref, _pack_bf16_pair(x1[:, 0:half], x1[:, half:]))

    x_hi = x1.astype(BF16)
    x_lo = (x1 - x_hi.astype(F32)).astype(BF16)
    nt = (((1,), (1,)), ((), ()))
    logits = lax.dot_general(wrh_ref[...], x_hi, nt, preferred_element_type=F32)
    logits = logits + lax.dot_general(wrh_ref[...], x_lo, nt, preferred_element_type=F32)
    logits = logits + lax.dot_general(wrl_ref[...], x_hi, nt, preferred_element_type=F32)
    scores = jax.nn.sigmoid(logits)
    sel = scores + rb_ref[...]

    neg = -jnp.inf
    gi = lax.broadcasted_iota(I32, (GROUP_SIZE, tm), 0)
    grp_rows = []
    for g in range(N_GROUPS):
        blk = sel[g * GROUP_SIZE:(g + 1) * GROUP_SIZE, :]
        m1 = jnp.max(blk, axis=0, keepdims=True)
        f1 = _first_index(blk == m1, gi, GROUP_SIZE)
        m2 = jnp.max(jnp.where(gi == f1, neg, blk), axis=0, keepdims=True)
        grp_rows.append(m1 + m2)
    ni = lax.broadcasted_iota(I32, (N_GROUPS, tm), 0)
    gscore = jnp.zeros((N_GROUPS, tm), F32)
    for g in range(N_GROUPS):
        gscore = jnp.where(ni == g, grp_rows[g], gscore)
    gmask = jnp.zeros((N_GROUPS, tm), jnp.bool_)
    for _ in range(TOPK_GROUPS):
        m = jnp.max(gscore, axis=0, keepdims=True)
        f = _first_index(gscore == m, ni, N_GROUPS)
        hit = ni == f
        gmask = gmask | hit
        gscore = jnp.where(hit, neg, gscore)
    emask = jnp.concatenate(
        [jnp.broadcast_to(gmask[g:g + 1, :], (GROUP_SIZE, tm)) for g in range(N_GROUPS)], axis=0)
    masked = jnp.where(emask, sel, neg)

    ei = lax.broadcasted_iota(I32, (N_EXPERTS, tm), 0)
    chosen = jnp.zeros((N_EXPERTS, tm), jnp.bool_)
    picks, gates = [], []
    for _ in range(TOP_K):
        m = jnp.max(masked, axis=0, keepdims=True)
        f = _first_index(masked == m, ei, N_EXPERTS)
        hit = ei == f
        picks.append(f)
        gates.append(jnp.sum(jnp.where(hit, scores, 0.0), axis=0, keepdims=True))
        chosen = chosen | hit
        masked = jnp.where(hit, neg, masked)
    gsum = gates[0]
    for k in range(1, TOP_K):
        gsum = gsum + gates[k]

    chosen_f = jnp.where(chosen, 1.0, 0.0)
    incl = jnp.dot(chosen_f.astype(BF16), tri_ref[...], preferred_element_type=F32)
    rank_excl = base_ref[:, 0:1] + incl - chosen_f
    base_ref[...] = base_ref[...] + jnp.sum(chosen_f, axis=1, keepdims=True)
    cnt_ref[...] = base_ref[...]
    for k in range(TOP_K):
        idx_ref[k:k + 1, :] = picks[k]
        gate_ref[k:k + 1, :] = gates[k] / gsum * ROUTED_SCALE
        rk = jnp.sum(jnp.where(ei == picks[k], rank_excl, 0.0), axis=0, keepdims=True)
        rank_ref[k:k + 1, :] = rk.astype(I32)


def _mix_route(lru, att, x2, wo_bf, ln_g, ln_b, w_router, router_bias):
    n, d = x2.shape
    half = lru.shape[1]
    wr_t = w_router.T
    wr_hi = wr_t.astype(BF16)
    wr_lo = (wr_t - wr_hi.astype(F32)).astype(BF16)
    assert half == TILE_WORDS
    tri = jnp.triu(jnp.ones((MIX_TM, MIX_TM), BF16))
    const = lambda shape: pl.BlockSpec(shape, lambda i: (0,) * len(shape))
    tok = lambda dtype: jax.ShapeDtypeStruct((TOP_K, n), dtype)
    tok_spec = pl.BlockSpec((TOP_K, MIX_TM), lambda i: (0, i))
    return pl.pallas_call(
        _mix_kernel,
        grid=(n // MIX_TM,),
        in_specs=[pl.BlockSpec((MIX_TM, half), lambda i: (i, 0)),
                  pl.BlockSpec((MIX_TM, half), lambda i: (i, 0)),
                  pl.BlockSpec((MIX_TM, d), lambda i: (i, 0)),
                  const((d, d)), const((1, d)), const((1, d)),
                  const((N_EXPERTS, d)), const((N_EXPERTS, d)), const((N_EXPERTS, 1)),
                  const((MIX_TM, MIX_TM))],
        out_specs=[pl.BlockSpec((MIX_TM, d), lambda i: (i, 0)),
                   pl.BlockSpec((MIX_TM * SUBLANES, LANES), lambda i: (i, 0)),
                   tok_spec, tok_spec, tok_spec, const((N_EXPERTS, HEAD_DIM))],
        out_shape=[jax.ShapeDtypeStruct((n, d), F32), jax.ShapeDtypeStruct((n * SUBLANES, LANES), U32),
                   tok(I32), tok(F32), tok(I32),
                   jax.ShapeDtypeStruct((N_EXPERTS, HEAD_DIM), F32)],
        scratch_shapes=[pltpu.VMEM((N_EXPERTS, HEAD_DIM), F32)],
        compiler_params=_cparams(("arbitrary",)),
        name="mix_ln_route",
    )(lru, att, x2, wo_bf, ln_g.reshape(1, d), ln_b.reshape(1, d), wr_hi, wr_lo,
      router_bias.reshape(N_EXPERTS, 1), tri)


def _dispatch_kernel(start_ref, cnt_ref, pos_ref, x1_ref, zero_ref, xs_hbm, sem):
    i = pl.program_id(0)
    tt = pos_ref.shape[1]

    def row_copy(t, k):
        return pltpu.make_async_copy(x1_ref.at[t], xs_hbm.at[pos_ref[k, t]], sem)

    def issue(t, c):
        for k in range(TOP_K):
            row_copy(t, k).start()
        return c

    def drain(t, c):
        for k in range(TOP_K):
            row_copy(t, k).wait()
        return c

    lax.fori_loop(0, tt, issue, 0)
    lax.fori_loop(0, tt, drain, 0)

    @pl.when(i == 0)
    def _():
        def pad_copy(r, rows):
            return pltpu.make_async_copy(zero_ref.at[pl.ds(0, rows)], xs_hbm.at[pl.ds(r, rows)], sem)

        def per_expert(e, c):
            lo = start_ref[e] + cnt_ref[e]
            hi = start_ref[e + 1]
            chunks = (hi - lo) // ZERO_ROWS
            mid = lo + chunks * ZERO_ROWS
            big = lambda j: pad_copy(lo + j * ZERO_ROWS, ZERO_ROWS)
            lax.fori_loop(0, chunks, lambda j, cc: (big(j).start(), cc)[1], 0)
            lax.fori_loop(mid, hi, lambda r, cc: (pad_copy(r, 1).start(), cc)[1], 0)
            lax.fori_loop(0, chunks, lambda j, cc: (big(j).wait(), cc)[1], 0)
            lax.fori_loop(mid, hi, lambda r, cc: (pad_copy(r, 1).wait(), cc)[1], 0)
            return c

        lax.fori_loop(0, N_EXPERTS, per_expert, 0)


def _dispatch(x1p, pos_t, start, counts, total_rows):
    n = x1p.shape[0]
    smem_tok = pl.BlockSpec((TOP_K, DISPATCH_TT), lambda i, *_: (0, i), memory_space=pltpu.SMEM)
    grid_spec = pltpu.PrefetchScalarGridSpec(
        num_scalar_prefetch=2,
        grid=(n // DISPATCH_TT,),
        in_specs=[smem_tok,
                  pl.BlockSpec((DISPATCH_TT, SUBLANES, LANES), lambda i, *_: (i, 0, 0)),
                  pl.BlockSpec((ZERO_ROWS, SUBLANES, LANES), lambda i, *_: (0, 0, 0))],
        out_specs=pl.BlockSpec(memory_space=pl.ANY),
        scratch_shapes=[pltpu.SemaphoreType.DMA(())],
    )
    return pl.pallas_call(
        _dispatch_kernel,
        grid_spec=grid_spec,
        out_shape=jax.ShapeDtypeStruct((total_rows, SUBLANES, LANES), x1p.dtype),
        compiler_params=_cparams(("arbitrary",)),
        name="dispatch",
    )(start, counts, pos_t, x1p, jnp.zeros((ZERO_ROWS, SUBLANES, LANES), x1p.dtype))


def _expert_kernel(be_ref, used_ref, nxt_ref, slot_ref, xs_ref, wg_hbm, wu_hbm, wd_hbm, ys_ref,
                   wg_f32, wu_f32, wd_f32, wg_bf, wu_bf, wd_bf, sems):
    i = pl.program_id(0)
    e = be_ref[i]
    slot = slot_ref[i]
    changed = (i == 0) | (e != be_ref[jnp.maximum(i - 1, 0)])

    def weight_copies(expert, s):
        return (pltpu.make_async_copy(wg_hbm.at[expert], wg_f32.at[s], sems.at[0, s]),
                pltpu.make_async_copy(wu_hbm.at[expert], wu_f32.at[s], sems.at[1, s]),
                pltpu.make_async_copy(wd_hbm.at[expert], wd_f32.at[s], sems.at[2, s]))

    @pl.when(i == 0)
    def _():
        for c in weight_copies(e, slot):
            c.start()

    @pl.when(changed)
    def _():
        for c in weight_copies(e, slot):
            c.wait()
        nxt = nxt_ref[i]

        @pl.when(nxt != e)
        def _():
            for c in weight_copies(nxt, 1 - slot):
                c.start()

        wg_bf[...] = wg_f32[slot].astype(BF16)
        wu_bf[...] = wu_f32[slot].astype(BF16)
        wd_bf[...] = wd_f32[slot].astype(BF16)

    @pl.when(i < used_ref[0])
    def _():
        half = TILE_WORDS
        lo, hi = _unpack_bf16_pair(_load_rows_from_tiles(xs_ref, 0, EXPERT_ROWS))
        lo, hi = lo.astype(BF16), hi.astype(BF16)
        g = (jnp.dot(lo, wg_bf[0:half, :], preferred_element_type=F32)
             + jnp.dot(hi, wg_bf[half:, :], preferred_element_type=F32))
        u = (jnp.dot(lo, wu_bf[0:half, :], preferred_element_type=F32)
             + jnp.dot(hi, wu_bf[half:, :], preferred_element_type=F32))
        h = (g * jax.nn.sigmoid(g) * u).astype(BF16)
        y = jnp.dot(h, wd_bf[...], preferred_element_type=F32)
        _store_rows_as_tiles(ys_ref, _pack_bf16_pair(y[:, 0:half], y[:, half:]))

    @pl.when(i >= used_ref[0])
    def _():
        ys_ref[...] = jnp.zeros_like(ys_ref)


def _experts(xs, blk_expert, used, w_gate, w_up, w_down):
    rows = xs.shape[0] // SUBLANES
    d = 2 * TILE_WORDS
    de = w_gate.shape[2]
    nblk = rows // EXPERT_ROWS
    later = blk_expert[None, :] > blk_expert[:, None]
    nxt = jnp.min(jnp.where(later, blk_expert[None, :], N_EXPERTS), axis=1)
    nxt = jnp.where(nxt == N_EXPERTS, blk_expert, nxt)
    is_new = jnp.concatenate([jnp.zeros((1,), I32), (blk_expert[1:] != blk_expert[:-1]).astype(I32)])
    slot = jnp.cumsum(is_new) % 2
    grid_spec = pltpu.PrefetchScalarGridSpec(
        num_scalar_prefetch=4,
        grid=(nblk,),
        in_specs=[pl.BlockSpec((EXPERT_ROWS * SUBLANES, LANES),
                               lambda i, be, used, *_: (jnp.minimum(i, used[0] - 1), 0)),
                  pl.BlockSpec(memory_space=pl.ANY), pl.BlockSpec(memory_space=pl.ANY),
                  pl.BlockSpec(memory_space=pl.ANY)],
        out_specs=pl.BlockSpec((EXPERT_ROWS * SUBLANES, LANES), lambda i, *_: (i, 0)),
        scratch_shapes=[pltpu.VMEM((2, d, de), F32), pltpu.VMEM((2, d, de), F32),
                        pltpu.VMEM((2, de, d), F32),
                        pltpu.VMEM((d, de), BF16), pltpu.VMEM((d, de), BF16),
                        pltpu.VMEM((de, d), BF16), pltpu.SemaphoreType.DMA((3, 2))],
    )
    return pl.pallas_call(
        _expert_kernel,
        grid_spec=grid_spec,
        out_shape=jax.ShapeDtypeStruct((rows * SUBLANES, LANES), U32),
        compiler_params=_cparams(("arbitrary",)),
        name="experts",
    )(blk_expert, used, nxt.astype(I32), slot.astype(I32), xs, w_gate, w_up, w_down)


def _combine_kernel(pos_ref, pos_next_ref, gate_ref, x1_ref, ys_hbm, wsg_ref, wsu_ref,
                    wsd_ref, g_ref, b_ref, o_ref, buf0, buf1, sems):
    tt = x1_ref.shape[0]
    i = pl.program_id(0)
    even = i % 2 == 0

    def row_copy(positions, buf, s, t, k):
        dst = pl.multiple_of((k * tt + t) * SUBLANES, SUBLANES)
        return pltpu.make_async_copy(ys_hbm.at[positions[k, t]], buf.at[pl.ds(dst, SUBLANES)],
                                     sems.at[s])

    def gather(positions, buf, s, start):
        def step(t, c):
            for k in range(TOP_K):
                copy = row_copy(positions, buf, s, t, k)
                copy.start() if start else copy.wait()
            return c
        lax.fori_loop(0, tt, step, 0)

    @pl.when(i == 0)
    def _():
        gather(pos_ref, buf0, 0, True)

    has_next = i + 1 < pl.num_programs(0)

    @pl.when(has_next & even)
    def _():
        gather(pos_next_ref, buf1, 1, True)

    @pl.when(has_next & jnp.logical_not(even))
    def _():
        gather(pos_next_ref, buf0, 0, True)

    x1 = x1_ref[...]
    xb = x1.astype(BF16)
    g = jnp.dot(xb, wsg_ref[...], preferred_element_type=F32)
    u = jnp.dot(xb, wsu_ref[...], preferred_element_type=F32)
    h = (g * jax.nn.sigmoid(g) * u).astype(BF16)
    y = DN_ALPHA * x1 + jnp.dot(h, wsd_ref[...], preferred_element_type=F32)

    gate = gate_ref[...]

    def finish(buf, s):
        gather(pos_ref, buf, s, False)
        r_lo = jnp.zeros((tt, TILE_WORDS), F32)
        r_hi = jnp.zeros((tt, TILE_WORDS), F32)
        for k in range(TOP_K):
            lo, hi = _unpack_bf16_pair(_load_rows_from_tiles(buf, k * tt, tt))
            r_lo = r_lo + lo * gate[:, k:k + 1]
            r_hi = r_hi + hi * gate[:, k:k + 1]
        out = y + jnp.concatenate([r_lo, r_hi], axis=1)
        o_ref[...] = _layer_norm(out, g_ref[...], b_ref[...])

    @pl.when(even)
    def _():
        finish(buf0, 0)

    @pl.when(jnp.logical_not(even))
    def _():
        finish(buf1, 1)


def _combine(x1, ys, pos_t, gate_nk, wsg_bf, wsu_bf, wsd_bf, ln_g, ln_b):
    n, d = x1.shape
    de = wsg_bf.shape[1]
    smem_tok = pl.BlockSpec((TOP_K, COMBINE_TT), lambda i, *_: (0, i), memory_space=pltpu.SMEM)
    steps = n // COMBINE_TT
    smem_next = pl.BlockSpec((TOP_K, COMBINE_TT), lambda i, *_: (0, jnp.minimum(i + 1, steps - 1)),
                             memory_space=pltpu.SMEM)
    const = lambda shape: pl.BlockSpec(shape, lambda i, *_: (0,) * len(shape))
    buf = pltpu.VMEM((TOP_K * COMBINE_TT * SUBLANES, LANES), U32)
    grid_spec = pltpu.PrefetchScalarGridSpec(
        num_scalar_prefetch=0,
        grid=(steps,),
        in_specs=[smem_tok, smem_next,
                  pl.BlockSpec((COMBINE_TT, TOP_K), lambda i, *_: (i, 0)),
                  pl.BlockSpec((COMBINE_TT, d), lambda i, *_: (i, 0)),
                  pl.BlockSpec(memory_space=pl.ANY),
                  const((d, de)), const((d, de)), const((de, d)), const((1, d)), const((1, d))],
        out_specs=pl.BlockSpec((COMBINE_TT, d), lambda i, *_: (i, 0)),
        scratch_shapes=[buf, buf, pltpu.SemaphoreType.DMA((2,))],
    )
    return pl.pallas_call(
        _combine_kernel,
        grid_spec=grid_spec,
        out_shape=jax.ShapeDtypeStruct((n, d), F32),
        compiler_params=_cparams(("arbitrary",)),
        name="combine_ln",
    )(pos_t, pos_t, gate_nk, x1, ys, wsg_bf, wsu_bf, wsd_bf, ln_g.reshape(1, d), ln_b.reshape(1, d))


def _block_layout(counts_f, n_assign):
    counts = counts_f.astype(I32)
    padded = (counts + EXPERT_ROWS - 1) // EXPERT_ROWS * EXPERT_ROWS
    end = jnp.cumsum(padded)
    total_rows = n_assign + N_EXPERTS * EXPERT_ROWS
    start = jnp.concatenate([end - padded, jnp.full((1,), total_rows, I32)])
    nblk = total_rows // EXPERT_ROWS
    used = end[-1:] // EXPERT_ROWS
    blk_start = jnp.minimum(jnp.arange(nblk, dtype=I32), used[0] - 1) * EXPERT_ROWS
    blk_expert = jnp.sum((end[None, :] <= blk_start[:, None]).astype(I32), axis=1)
    blk_expert = jnp.minimum(blk_expert, N_EXPERTS - 1)
    return start, counts, blk_expert, used.astype(I32), total_rows


def kernel(x, w_in, conv_w, conv_b, w_rg_a, b_rg_a, w_rg_x, b_rg_x, lru_lambda, w_out, ln1_g, ln1_b,
           w_router, router_bias, w_gate, w_up, w_down, ws_gate, ws_up, ws_down, ln2_g, ln2_b):
    b, s, d = x.shape
    n = b * s
    width = d // 2
    for l in range(DEPTH):
        x2 = x.reshape(n, d)
        proj = _proj(x2, w_in[l].astype(BF16), s, width)
        proj3 = proj.reshape(b, s, proj.shape[1])
        lru = _rglru(proj3, conv_w[l], conv_b[l], w_rg_a[l].astype(BF16), b_rg_a[l],
                     w_rg_x[l].astype(BF16), b_rg_x[l], lru_lambda[l], width)
        att = _attention(proj3, width)
        x1, x1p, idx_t, gate_t, rank_t, cnt = _mix_route(
            lru.reshape(n, width), att.reshape(n, width), x2, w_out[l].astype(BF16),
            ln1_g[l], ln1_b[l], w_router[l], router_bias[l])
        start, counts, blk_expert, used, total_rows = _block_layout(cnt[:, 0], n * TOP_K)
        onehot = idx_t[:, :, None] == jnp.arange(N_EXPERTS, dtype=I32)[None, None, :]
        pos_t = jnp.sum(jnp.where(onehot, start[None, None, :N_EXPERTS], 0), axis=2) + rank_t
        xs = _dispatch(x1p.reshape(n, SUBLANES, LANES), pos_t, start, counts, total_rows)
        ys = _experts(xs.reshape(total_rows * SUBLANES, LANES), blk_expert, used,
                      w_gate[l], w_up[l], w_down[l])
        out = _combine(x1, ys.reshape(total_rows, SUBLANES, LANES), pos_t, gate_t.T,
                       ws_gate[l].astype(BF16),
                       ws_up[l].astype(BF16), ws_down[l].astype(BF16), ln2_g[l], ln2_b[l])
        x = out.reshape(b, s, d)
    return x
```

```python
import functools

import jax
import jax.numpy as jnp
from jax import lax
from jax.experimental import pallas as pl
from jax.experimental.pallas import tpu as pltpu

F32 = jnp.float32
BF16 = jnp.bfloat16
I32 = jnp.int32
U32 = jnp.uint32

LRU_BLOCKS = 8
CONV_W = 4
LRU_C = 8.0
HEAD_DIM = 128
ROT_DIM = HEAD_DIM // 4
ROPE_THETA = 500000.0
DILATED_PATTERNS = ((128, 1), (512, 4), (2048, 16))
ATTN_BLOCK = 128
N_EXPERTS = 64
N_GROUPS = 8
GROUP_SIZE = N_EXPERTS // N_GROUPS
TOPK_GROUPS = 4
TOP_K = 8
ROUTED_SCALE = 2.5
DEPTH = 1
DN_ALPHA = (2 * DEPTH) ** 0.25
LN_EPS = 1e-5

PROJ_TM = 1024
LRU_TC = 512
MIX_TM = 512
ATTN_UNROLL = 8
EXPERT_ROWS = 512
DISPATCH_TT = 512
COMBINE_TT = 256
ZERO_ROWS = 64
SUBLANES = 8
LANES = 128
TILE_WORDS = SUBLANES * LANES
VMEM_LIMIT = 56 * 1024 * 1024


def _pack_bf16_pair(lo, hi):
    lo_bits = lax.bitcast_convert_type(lo.astype(BF16).astype(F32), U32)
    hi_bits = lax.bitcast_convert_type(hi.astype(BF16).astype(F32), U32)
    return (lo_bits >> 16) | (hi_bits & jnp.uint32(0xFFFF0000))


def _unpack_bf16_pair(packed):
    lo = lax.bitcast_convert_type(packed << 16, F32)
    hi = lax.bitcast_convert_type(packed & jnp.uint32(0xFFFF0000), F32)
    return lo, hi


def _store_rows_as_tiles(ref, words):
    rows = words.shape[0]
    for s in range(SUBLANES):
        ref[pl.ds(s, rows, stride=SUBLANES), :] = words[:, s * LANES:(s + 1) * LANES]


def _load_rows_from_tiles(ref, row0, rows):
    return jnp.concatenate(
        [ref[pl.ds(row0 * SUBLANES + s, rows, stride=SUBLANES), :] for s in range(SUBLANES)], axis=1)


def _cparams(sem):
    return pltpu.CompilerParams(dimension_semantics=sem, vmem_limit_bytes=VMEM_LIMIT)


def _proj_kernel(x_ref, w_ref, c_ref, sa_ref, sb_ref, o_ref, xb_ref, *, n_heads):
    j = pl.program_id(1)

    @pl.when(j == 0)
    def _():
        xb_ref[...] = x_ref[...].astype(BF16)

    acc = jnp.dot(xb_ref[...], w_ref[...], preferred_element_type=F32)
    is_rope = (j == 2) | (j == 3)

    @pl.when(is_rope)
    def _():
        scale = jnp.where(j == 2, HEAD_DIM ** -0.5, 1.0).astype(F32)
        c = c_ref[...] * scale
        sa = sa_ref[...] * scale
        sb = sb_ref[...] * scale
        half = ROT_DIM // 2
        for h in range(n_heads):
            t = acc[:, h * HEAD_DIM:(h + 1) * HEAD_DIM]
            o_ref[:, h * HEAD_DIM:(h + 1) * HEAD_DIM] = (
                t * c + pltpu.roll(t, half, 1) * sa + pltpu.roll(t, HEAD_DIM - half, 1) * sb)

    @pl.when(jnp.logical_not(is_rope))
    def _():
        o_ref[...] = acc


def _rope_tables(seq):
    half = ROT_DIM // 2
    pos = jnp.arange(seq, dtype=F32)
    inv = ROPE_THETA ** (-jnp.arange(0, ROT_DIM, 2, dtype=F32) / ROT_DIM)
    ang = pos[:, None] * inv[None, :]
    cos, sin = jnp.cos(ang), jnp.sin(ang)
    ones = jnp.ones((seq, HEAD_DIM - ROT_DIM), F32)
    zeros = jnp.zeros((seq, HEAD_DIM - ROT_DIM), F32)
    zh = jnp.zeros((seq, half), F32)
    c = jnp.concatenate([cos, cos, ones], -1)
    sa = jnp.concatenate([zh, sin, zeros], -1)
    sb = jnp.concatenate([-sin, zh, zeros], -1)
    return c, sa, sb


def _proj(x2, w_in_bf, seq, width):
    n, d = x2.shape
    e = w_in_bf.shape[1]
    n_groups = e // width
    c, sa, sb = _rope_tables(seq)
    tiles_per_seq = seq // PROJ_TM
    tab_spec = pl.BlockSpec((PROJ_TM, HEAD_DIM), lambda i, j: (i % tiles_per_seq, 0))
    return pl.pallas_call(
        functools.partial(_proj_kernel, n_heads=width // HEAD_DIM),
        grid=(n // PROJ_TM, n_groups),
        in_specs=[pl.BlockSpec((PROJ_TM, d), lambda i, j: (i, 0)),
                  pl.BlockSpec((d, width), lambda i, j: (0, j)),
                  tab_spec, tab_spec, tab_spec],
        out_specs=pl.BlockSpec((PROJ_TM, width), lambda i, j: (i, j)),
        out_shape=jax.ShapeDtypeStruct((n, e), F32),
        scratch_shapes=[pltpu.VMEM((PROJ_TM, d), BF16)],
        compiler_params=_cparams(("arbitrary", "arbitrary")),
        name="proj_rope",
    )(x2, w_in_bf, c, sa, sb)


def _lru_kernel(xl_ref, gl_ref, cw_ref, cb_ref, wa_ref, ba_ref, wx_ref, bx_ref, lam_ref,
                o_ref, xext_ref, hprev_ref, a_ref, b_ref, h_ref, *, block_w):
    tc, width = xl_ref.shape
    groups = tc // SUBLANES

    @pl.when(pl.program_id(1) == 0)
    def _():
        xext_ref[0:SUBLANES, :] = jnp.zeros((SUBLANES, width), F32)
        hprev_ref[...] = jnp.zeros_like(hprev_ref)

    x = xl_ref[...]
    xext_ref[SUBLANES:SUBLANES + tc, :] = x
    xc = x * cw_ref[CONV_W - 1:CONV_W, :] + cb_ref[...]
    for j in range(CONV_W - 1):
        back = CONV_W - 1 - j
        xc = xc + xext_ref[pl.ds(SUBLANES - back, tc), :] * cw_ref[j:j + 1, :]
    xext_ref[0:SUBLANES, :] = x[tc - SUBLANES:, :]

    lam = lam_ref[...]
    softplus_neg_lam = jnp.maximum(-lam, 0.0) + jnp.log1p(jnp.exp(-jnp.abs(lam)))
    rs, is_ = [], []
    for nb in range(width // block_w):
        xb = xc[:, nb * block_w:(nb + 1) * block_w].astype(BF16)
        rs.append(jnp.dot(xb, wa_ref[nb], preferred_element_type=F32))
        is_.append(jnp.dot(xb, wx_ref[nb], preferred_element_type=F32))
    r = jax.nn.sigmoid(jnp.concatenate(rs, axis=1) + ba_ref[...])
    ig = jax.nn.sigmoid(jnp.concatenate(is_, axis=1) + bx_ref[...])
    a = jnp.exp(-LRU_C * r * softplus_neg_lam)
    b = jnp.sqrt(1.0 - a * a) * (ig * xc)

    row = lax.broadcasted_iota(I32, (tc, width), 0) & (SUBLANES - 1)
    k = 1
    while k < SUBLANES:
        keep = row >= k
        a_sh = pltpu.roll(a, k, 0)
        b_sh = pltpu.roll(b, k, 0)
        b = jnp.where(keep, a * b_sh + b, b)
        a = jnp.where(keep, a * a_sh, a)
        k *= 2
    a_ref[...] = a.reshape(groups, SUBLANES, width)
    b_ref[...] = b.reshape(groups, SUBLANES, width)

    def carry(g, hprev):
        h = a_ref[g] * hprev + b_ref[g]
        h_ref[g] = h
        return h[SUBLANES - 1:SUBLANES, :]

    hprev_ref[...] = lax.fori_loop(0, groups, carry, hprev_ref[...])
    h = h_ref[...].reshape(tc, width)
    o_ref[...] = (h * jax.nn.gelu(gl_ref[...], approximate=True)).astype(o_ref.dtype)


def _rglru(proj3, conv_w, conv_b, wa_bf, b_a, wx_bf, b_x, lam, width):
    b, s, _ = proj3.shape
    block_w = width // LRU_BLOCKS
    vec = pl.BlockSpec((1, width), lambda bi, ci: (0, 0))
    wspec = pl.BlockSpec((LRU_BLOCKS, block_w, block_w), lambda bi, ci: (0, 0, 0))
    groups = LRU_TC // SUBLANES
    return pl.pallas_call(
        functools.partial(_lru_kernel, block_w=block_w),
        grid=(b, s // LRU_TC),
        in_specs=[pl.BlockSpec((None, LRU_TC, width), lambda bi, ci: (bi, ci, 0)),
                  pl.BlockSpec((None, LRU_TC, width), lambda bi, ci: (bi, ci, 1)),
                  pl.BlockSpec((CONV_W, width), lambda bi, ci: (0, 0)),
                  vec, wspec, vec, wspec, vec, vec],
        out_specs=pl.BlockSpec((None, LRU_TC, width), lambda bi, ci: (bi, ci, 0)),
        out_shape=jax.ShapeDtypeStruct((b, s, width), BF16),
        scratch_shapes=[pltpu.VMEM((LRU_TC + SUBLANES, width), F32),
                        pltpu.VMEM((1, width), F32),
                        pltpu.VMEM((groups, SUBLANES, width), F32),
                        pltpu.VMEM((groups, SUBLANES, width), F32),
                        pltpu.VMEM((groups, SUBLANES, width), F32)],
        compiler_params=_cparams(("arbitrary", "arbitrary")),
        name="rglru",
    )(proj3, proj3, conv_w, conv_b.reshape(1, width), wa_bf, b_a.reshape(1, width),
      wx_bf, b_x.reshape(1, width), lam.reshape(1, width))


def _attn_kernel(q_ref, k_ref, v_ref, o_ref, qs, ks, vs, acc_s, m_s, l_s, acc_n, m_n, l_n):
    seq = q_ref.shape[0]
    blk = ATTN_BLOCK
    nblk = seq // blk
    n_pat = len(DILATED_PATTERNS)

    @pl.when((pl.program_id(0) == 0) & (pl.program_id(1) == 0))
    def _():
        for p in range(n_pat):
            vs[p, :, HEAD_DIM:] = jnp.ones((seq + blk, HEAD_DIM), BF16)

    for p, (_, d) in enumerate(DILATED_PATTERNS):
        cls_len = seq // d
        ks[p, 0:blk, :] = jnp.zeros((blk, HEAD_DIM), BF16)
        vs[p, 0:blk, 0:HEAD_DIM] = jnp.zeros((blk, HEAD_DIM), BF16)
        for r in range(d):
            src = pl.ds(r, cls_len, stride=d) if d > 1 else pl.ds(0, seq)
            qs[p, r * cls_len:(r + 1) * cls_len, :] = q_ref[src, :].astype(BF16)
            ks[p, blk + r * cls_len:blk + (r + 1) * cls_len, :] = k_ref[src, :].astype(BF16)
            vs[p, blk + r * cls_len:blk + (r + 1) * cls_len, 0:HEAD_DIM] = v_ref[src, :].astype(BF16)

    qi = lax.broadcasted_iota(I32, (blk, 2 * blk), 0)
    kj = lax.broadcasted_iota(I32, (blk, 2 * blk), 1)
    band2 = (kj >= qi) & (kj <= qi + blk)
    in_block = kj >= blk
    qi1 = lax.broadcasted_iota(I32, (blk, blk), 0)
    kj1 = lax.broadcasted_iota(I32, (blk, blk), 1)
    causal1 = kj1 <= qi1
    nt = (((1,), (1,)), ((), ()))

    def one_block(p, row0, first):
        q = qs[p, pl.ds(row0, blk), :]
        if first is True:
            kk = ks[p, pl.ds(row0 + blk, blk), :]
            vv = vs[p, pl.ds(row0 + blk, blk), :]
            mask = causal1
        else:
            kk = ks[p, pl.ds(row0, 2 * blk), :]
            vv = vs[p, pl.ds(row0, 2 * blk), :]
            mask = band2 if first is False else band2 & (in_block | jnp.logical_not(first))
        s = lax.dot_general(q, kk, nt, preferred_element_type=F32)
        s = jnp.where(mask, s, -jnp.inf)
        m = jnp.max(s, axis=1, keepdims=True)
        e = jnp.exp(s - m)
        acc_l = jnp.dot(e.astype(BF16), vv, preferred_element_type=F32)
        acc_s[p, pl.ds(row0, blk), :] = acc_l[:, 0:HEAD_DIM]
        m_s[p, pl.ds(row0, blk), :] = jnp.broadcast_to(m, (blk, HEAD_DIM))
        l_s[p, pl.ds(row0, blk), :] = acc_l[:, HEAD_DIM:]

    def body(it, carry):
        for p, (window, d) in enumerate(DILATED_PATTERNS):
            assert window // d == blk
            blocks_per_class = (seq // d) // blk
            for j in range(ATTN_UNROLL):
                row0 = pl.multiple_of((it * ATTN_UNROLL + j) * blk, blk)
                if ATTN_UNROLL % blocks_per_class == 0:
                    first = j % blocks_per_class == 0
                else:
                    assert blocks_per_class % ATTN_UNROLL == 0
                    first = (it % (blocks_per_class // ATTN_UNROLL) == 0) if j == 0 else False
                one_block(p, row0, first)
        return carry

    lax.fori_loop(0, nblk // ATTN_UNROLL, body, 0)

    for p, (_, d) in enumerate(DILATED_PATTERNS):
        cls_len = seq // d
        for r in range(d):
            dst = pl.ds(r, cls_len, stride=d) if d > 1 else pl.ds(0, seq)
            acc_n[p, dst, :] = acc_s[p, r * cls_len:(r + 1) * cls_len, :]
            m_n[p, dst, :] = m_s[p, r * cls_len:(r + 1) * cls_len, :]
            l_n[p, dst, :] = l_s[p, r * cls_len:(r + 1) * cls_len, :]

    mx = m_n[0]
    for p in range(1, n_pat):
        mx = jnp.maximum(mx, m_n[p])
    num = jnp.zeros((seq, HEAD_DIM), F32)
    den = jnp.zeros((seq, HEAD_DIM), F32)
    for p in range(n_pat):
        w = jnp.exp(m_n[p] - mx)
        num = num + w * acc_n[p]
        den = den + w * l_n[p]
    o_ref[...] = (num / den).astype(o_ref.dtype)


def _attention(proj3, width):
    b, s, _ = proj3.shape
    n_heads = width // HEAD_DIM
    n_pat = len(DILATED_PATTERNS)
    col0 = 2 * width // HEAD_DIM

    def spec(group):
        return pl.BlockSpec((None, s, HEAD_DIM),
                            lambda bi, hi: (bi, 0, col0 + group * n_heads + hi))

    return pl.pallas_call(
        _attn_kernel,
        grid=(b, n_heads),
        in_specs=[spec(0), spec(1), spec(2)],
        out_specs=pl.BlockSpec((None, s, HEAD_DIM), lambda bi, hi: (bi, 0, hi)),
        out_shape=jax.ShapeDtypeStruct((b, s, width), BF16),
        scratch_shapes=[pltpu.VMEM((n_pat, s, HEAD_DIM), BF16),
                        pltpu.VMEM((n_pat, s + ATTN_BLOCK, HEAD_DIM), BF16),
                        pltpu.VMEM((n_pat, s + ATTN_BLOCK, 2 * HEAD_DIM), BF16)]
                       + [pltpu.VMEM((n_pat, s, HEAD_DIM), F32)] * 6,
        compiler_params=_cparams(("arbitrary", "arbitrary")),
        name="dilated_attention",
    )(proj3, proj3, proj3)


def _layer_norm(y, g, b):
    mu = jnp.mean(y, axis=-1, keepdims=True)
    yc = y - mu
    var = jnp.mean(yc * yc, axis=-1, keepdims=True)
    return yc * lax.rsqrt(var + LN_EPS) * g + b


def _first_index(hit, iota, size):
    return jnp.min(jnp.where(hit, iota, size), axis=0, keepdims=True)


def _mix_kernel(lru_ref, att_ref, x_ref, wo_ref, g_ref, b_ref, wrh_ref, wrl_ref, rb_ref, tri_ref,
                x1_ref, x1p_ref, idx_ref, gate_ref, rank_ref, cnt_ref, base_ref):
    tm = x_ref.shape[0]
    half = lru_ref.shape[1]

    @pl.when(pl.program_id(0) == 0)
    def _():
        base_ref[...] = jnp.zeros_like(base_ref)

    mix = jnp.dot(lru_ref[...], wo_ref[0:half, :], preferred_element_type=F32)
    mix = mix + jnp.dot(att_ref[...], wo_ref[half:, :], preferred_element_type=F32)
    x1 = _layer_norm(DN_ALPHA * x_ref[...] + mix, g_ref[...], b_ref[...])
    x1_ref[...] = x1
    _store_rows_as_tiles(x1p_ref, _pack_bf16_pair(x1[:, 0:half], x1[:, half:]))

    x_hi = x1.astype(BF16)
    x_lo = (x1 - x_hi.astype(F32)).astype(BF16)
    nt = (((1,), (1,)), ((), ()))
    logits = lax.dot_general(wrh_ref[...], x_hi, nt, preferred_element_type=F32)
    logits = logits + lax.dot_general(wrh_ref[...], x_lo, nt, preferred_element_type=F32)
    logits = logits + lax.dot_general(wrl_ref[...], x_hi, nt, preferred_element_type=F32)
    scores = jax.nn.sigmoid(logits)
    sel = scores + rb_ref[...]

    neg = -jnp.inf
    gi = lax.broadcasted_iota(I32, (GROUP_SIZE, tm), 0)
    grp_rows = []
    for g in range(N_GROUPS):
        blk = sel[g * GROUP_SIZE:(g + 1) * GROUP_SIZE, :]
        m1 = jnp.max(blk, axis=0, keepdims=True)
        f1 = _first_index(blk == m1, gi, GROUP_SIZE)
        m2 = jnp.max(jnp.where(gi == f1, neg, blk), axis=0, keepdims=True)
        grp_rows.append(m1 + m2)
    ni = lax.broadcasted_iota(I32, (N_GROUPS, tm), 0)
    gscore = jnp.zeros((N_GROUPS, tm), F32)
    for g in range(N_GROUPS):
        gscore = jnp.where(ni == g, grp_rows[g], gscore)
    gmask = jnp.zeros((N_GROUPS, tm), jnp.bool_)
    for _ in range(TOPK_GROUPS):
        m = jnp.max(gscore, axis=0, keepdims=True)
        f = _first_index(gscore == m, ni, N_GROUPS)
        hit = ni == f
        gmask = gmask | hit
        gscore = jnp.where(hit, neg, gscore)
    emask = jnp.concatenate(
        [jnp.broadcast_to(gmask[g:g + 1, :], (GROUP_SIZE, tm)) for g in range(N_GROUPS)], axis=0)
    masked = jnp.where(emask, sel, neg)

    ei = lax.broadcasted_iota(I32, (N_EXPERTS, tm), 0)
    chosen = jnp.zeros((N_EXPERTS, tm), jnp.bool_)
    picks, gates = [], []
    for _ in range(TOP_K):
        m = jnp.max(masked, axis=0, keepdims=True)
        f = _first_index(masked == m, ei, N_EXPERTS)
        hit = ei == f
        picks.append(f)
        gates.append(jnp.sum(jnp.where(hit, scores, 0.0), axis=0, keepdims=True))
        chosen = chosen | hit
        masked = jnp.where(hit, neg, masked)
    gsum = gates[0]
    for k in range(1, TOP_K):
        gsum = gsum + gates[k]

    chosen_f = jnp.where(chosen, 1.0, 0.0)
    incl = jnp.dot(chosen_f.astype(BF16), tri_ref[...], preferred_element_type=F32)
    rank_excl = base_ref[:, 0:1] + incl - chosen_f
    base_ref[...] = base_ref[...] + jnp.sum(chosen_f, axis=1, keepdims=True)
    cnt_ref[...] = base_ref[...]
    for k in range(TOP_K):
        idx_ref[k:k + 1, :] = picks[k]
        gate_ref[k:k + 1, :] = gates[k] / gsum * ROUTED_SCALE
        rk = jnp.sum(jnp.where(ei == picks[k], rank_excl, 0.0), axis=0, keepdims=True)
        rank_ref[k:k + 1, :] = rk.astype(I32)


def _mix_route(lru, att, x2, wo_bf, ln_g, ln_b, w_router, router_bias):
    n, d = x2.shape
    half = lru.shape[1]
    wr_t = w_router.T
    wr_hi = wr_t.astype(BF16)
    wr_lo = (wr_t - wr_hi.astype(F32)).astype(BF16)
    assert half == TILE_WORDS
    tri = jnp.triu(jnp.ones((MIX_TM, MIX_TM), BF16))
    const = lambda shape: pl.BlockSpec(shape, lambda i: (0,) * len(shape))
    tok = lambda dtype: jax.ShapeDtypeStruct((TOP_K, n), dtype)
    tok_spec = pl.BlockSpec((TOP_K, MIX_TM), lambda i: (0, i))
    return pl.pallas_call(
        _mix_kernel,
        grid=(n // MIX_TM,),
        in_specs=[pl.BlockSpec((MIX_TM, half), lambda i: (i, 0)),
                  pl.BlockSpec((MIX_TM, half), lambda i: (i, 0)),
                  pl.BlockSpec((MIX_TM, d), lambda i: (i, 0)),
                  const((d, d)), const((1, d)), const((1, d)),
                  const((N_EXPERTS, d)), const((N_EXPERTS, d)), const((N_EXPERTS, 1)),
                  const((MIX_TM, MIX_TM))],
        out_specs=[pl.BlockSpec((MIX_TM, d), lambda i: (i, 0)),
                   pl.BlockSpec((MIX_TM * SUBLANES, LANES), lambda i: (i, 0)),
                   tok_spec, tok_spec, tok_spec, const((N_EXPERTS, HEAD_DIM))],
        out_shape=[jax.ShapeDtypeStruct((n, d), F32), jax.ShapeDtypeStruct((n * SUBLANES, LANES), U32),
                   tok(I32), tok(F32), tok(I32),
                   jax.ShapeDtypeStruct((N_EXPERTS, HEAD_DIM), F32)],
        scratch_shapes=[pltpu.VMEM((N_EXPERTS, HEAD_DIM), F32)],
        compiler_params=_cparams(("arbitrary",)),
        name="mix_ln_route",
    )(lru, att, x2, wo_bf, ln_g.reshape(1, d), ln_b.reshape(1, d), wr_hi, wr_lo,
      router_bias.reshape(N_EXPERTS, 1), tri)


def _dispatch_kernel(start_ref, cnt_ref, pos_ref, x1_ref, zero_ref, xs_hbm, sem):
    i = pl.program_id(0)
    tt = pos_ref.shape[1]

    def row_copy(t, k):
        return pltpu.make_async_copy(x1_ref.at[t], xs_hbm.at[pos_ref[k, t]], sem)

    def issue(t, c):
        for k in range(TOP_K):
            row_copy(t, k).start()
        return c

    def drain(t, c):
        for k in range(TOP_K):
            row_copy(t, k).wait()
        return c

    lax.fori_loop(0, tt, issue, 0)
    lax.fori_loop(0, tt, drain, 0)

    @pl.when(i == 0)
    def _():
        def pad_copy(r, rows):
            return pltpu.make_async_copy(zero_ref.at[pl.ds(0, rows)], xs_hbm.at[pl.ds(r, rows)], sem)

        def per_expert(e, c):
            lo = start_ref[e] + cnt_ref[e]
            hi = start_ref[e + 1]
            chunks = (hi - lo) // ZERO_ROWS
            mid = lo + chunks * ZERO_ROWS
            big = lambda j: pad_copy(lo + j * ZERO_ROWS, ZERO_ROWS)
            lax.fori_loop(0, chunks, lambda j, cc: (big(j).start(), cc)[1], 0)
            lax.fori_loop(mid, hi, lambda r, cc: (pad_copy(r, 1).start(), cc)[1], 0)
            lax.fori_loop(0, chunks, lambda j, cc: (big(j).wait(), cc)[1], 0)
            lax.fori_loop(mid, hi, lambda r, cc: (pad_copy(r, 1).wait(), cc)[1], 0)
            return c

        lax.fori_loop(0, N_EXPERTS, per_expert, 0)


def _dispatch(x1p, pos_t, start, counts, total_rows):
    n = x1p.shape[0]
    smem_tok = pl.BlockSpec((TOP_K, DISPATCH_TT), lambda i, *_: (0, i), memory_space=pltpu.SMEM)
    grid_spec = pltpu.PrefetchScalarGridSpec(
        num_scalar_prefetch=2,
        grid=(n // DISPATCH_TT,),
        in_specs=[smem_tok,
                  pl.BlockSpec((DISPATCH_TT, SUBLANES, LANES), lambda i, *_: (i, 0, 0)),
                  pl.BlockSpec((ZERO_ROWS, SUBLANES, LANES), lambda i, *_: (0, 0, 0))],
        out_specs=pl.BlockSpec(memory_space=pl.ANY),
        scratch_shapes=[pltpu.SemaphoreType.DMA(())],
    )
    return pl.pallas_call(
        _dispatch_kernel,
        grid_spec=grid_spec,
        out_shape=jax.ShapeDtypeStruct((total_rows, SUBLANES, LANES), x1p.dtype),
        compiler_params=_cparams(("arbitrary",)),
        name="dispatch",
    )(start, counts, pos_t, x1p, jnp.zeros((ZERO_ROWS, SUBLANES, LANES), x1p.dtype))


def _expert_kernel(start_ref, cnt_ref, xs_hbm, wg_ref, wu_ref, wd_ref, ys_hbm,
                   xbuf, ybuf, wg_bf, wu_bf, wd_bf, in_sems, out_sems):
    e = pl.program_id(0)
    row0 = start_ref[e]
    nb = (cnt_ref[e] + EXPERT_ROWS - 1) // EXPERT_ROWS
    block_words = EXPERT_ROWS * SUBLANES
    half = TILE_WORDS

    def rows_of(b):
        return pl.ds(pl.multiple_of((row0 + b * EXPERT_ROWS) * SUBLANES, block_words), block_words)

    def in_copy(b, s):
        return pltpu.make_async_copy(xs_hbm.at[rows_of(b)], xbuf.at[s], in_sems.at[s])

    def out_copy(b, s):
        return pltpu.make_async_copy(ybuf.at[s], ys_hbm.at[rows_of(b)], out_sems.at[s])

    def swiglu(s):
        lo, hi = _unpack_bf16_pair(_load_rows_from_tiles(xbuf.at[s], 0, EXPERT_ROWS))
        lo, hi = lo.astype(BF16), hi.astype(BF16)
        g = (jnp.dot(lo, wg_bf[0:half, :], preferred_element_type=F32)
             + jnp.dot(hi, wg_bf[half:, :], preferred_element_type=F32))
        u = (jnp.dot(lo, wu_bf[0:half, :], preferred_element_type=F32)
             + jnp.dot(hi, wu_bf[half:, :], preferred_element_type=F32))
        h = (g * jax.nn.sigmoid(g) * u).astype(BF16)
        y = jnp.dot(h, wd_bf[...], preferred_element_type=F32)
        _store_rows_as_tiles(ybuf.at[s], _pack_bf16_pair(y[:, 0:half], y[:, half:]))

    def one_block(b, s):
        in_copy(b, s).wait()

        @pl.when(b + 1 < nb)
        def _():
            in_copy(b + 1, 1 - s).start()

        @pl.when(b >= 2)
        def _():
            out_copy(b - 2, s).wait()

        swiglu(s)
        out_copy(b, s).start()

    def pair(p, c):
        one_block(2 * p, 0)

        @pl.when(2 * p + 1 < nb)
        def _():
            one_block(2 * p + 1, 1)

        return c

    @pl.when(nb > 0)
    def _():
        in_copy(0, 0).start()
        wg_bf[...] = wg_ref[...].astype(BF16)
        wu_bf[...] = wu_ref[...].astype(BF16)
        wd_bf[...] = wd_ref[...].astype(BF16)
        lax.fori_loop(0, (nb + 1) // 2, pair, 0)

        @pl.when(nb >= 2)
        def _():
            out_copy(nb - 2, nb % 2).wait()

        out_copy(nb - 1, (nb - 1) % 2).wait()


def _experts(xs, start, counts, w_gate, w_up, w_down):
    d = 2 * TILE_WORDS
    de = w_gate.shape[2]
    block = pltpu.VMEM((2, EXPERT_ROWS * SUBLANES, LANES), U32)
    grid_spec = pltpu.PrefetchScalarGridSpec(
        num_scalar_prefetch=2,
        grid=(N_EXPERTS,),
        in_specs=[pl.BlockSpec(memory_space=pl.ANY),
                  pl.BlockSpec((None, d, de), lambda e, *_: (e, 0, 0)),
                  pl.BlockSpec((None, d, de), lambda e, *_: (e, 0, 0)),
                  pl.BlockSpec((None, de, d), lambda e, *_: (e, 0, 0))],
        out_specs=pl.BlockSpec(memory_space=pl.ANY),
        scratch_shapes=[block, block,
                        pltpu.VMEM((d, de), BF16), pltpu.VMEM((d, de), BF16),
                        pltpu.VMEM((de, d), BF16),
                        pltpu.SemaphoreType.DMA((2,)), pltpu.SemaphoreType.DMA((2,))],
    )
    return pl.pallas_call(
        _expert_kernel,
        grid_spec=grid_spec,
        out_shape=jax.ShapeDtypeStruct(xs.shape, xs.dtype),
        input_output_aliases={2: 0},
        compiler_params=_cparams(("arbitrary",)),
        name="experts",
    )(start, counts, xs, w_gate, w_up, w_down)


def _combine_kernel(pos_ref, gate_ref, x1_ref, ys_hbm, wsg_ref, wsu_ref,
                    wsd_ref, g_ref, b_ref, o_ref, buf, sem):
    tt = x1_ref.shape[0]

    def row_copy(t, k):
        dst = pl.multiple_of((k * tt + t) * SUBLANES, SUBLANES)
        return pltpu.make_async_copy(ys_hbm.at[pos_ref[k, t]], buf.at[pl.ds(dst, SUBLANES)], sem)

    def gather(start):
        def step(t, c):
            for k in range(TOP_K):
                copy = row_copy(t, k)
                copy.start() if start else copy.wait()
            return c
        lax.fori_loop(0, tt, step, 0)

    gather(True)

    x1 = x1_ref[...]
    xb = x1.astype(BF16)
    g = jnp.dot(xb, wsg_ref[...], preferred_element_type=F32)
    u = jnp.dot(xb, wsu_ref[...], preferred_element_type=F32)
    h = (g * jax.nn.sigmoid(g) * u).astype(BF16)
    y = DN_ALPHA * x1 + jnp.dot(h, wsd_ref[...], preferred_element_type=F32)

    gather(False)
    gate = gate_ref[...]
    r_lo = jnp.zeros((tt, TILE_WORDS), F32)
    r_hi = jnp.zeros((tt, TILE_WORDS), F32)
    for k in range(TOP_K):
        lo, hi = _unpack_bf16_pair(_load_rows_from_tiles(buf, k * tt, tt))
        r_lo = r_lo + lo * gate[:, k:k + 1]
        r_hi = r_hi + hi * gate[:, k:k + 1]
    y = y + jnp.concatenate([r_lo, r_hi], axis=1)
    o_ref[...] = _layer_norm(y, g_ref[...], b_ref[...])


def _combine(x1, ys, pos_t, gate_nk, wsg_bf, wsu_bf, wsd_bf, ln_g, ln_b):
    n, d = x1.shape
    de = wsg_bf.shape[1]
    smem_tok = pl.BlockSpec((TOP_K, COMBINE_TT), lambda i, *_: (0, i), memory_space=pltpu.SMEM)
    const = lambda shape: pl.BlockSpec(shape, lambda i, *_: (0,) * len(shape))
    grid_spec = pltpu.PrefetchScalarGridSpec(
        num_scalar_prefetch=0,
        grid=(n // COMBINE_TT,),
        in_specs=[smem_tok,
                  pl.BlockSpec((COMBINE_TT, TOP_K), lambda i, *_: (i, 0)),
                  pl.BlockSpec((COMBINE_TT, d), lambda i, *_: (i, 0)),
                  pl.BlockSpec(memory_space=pl.ANY),
                  const((d, de)), const((d, de)), const((de, d)), const((1, d)), const((1, d))],
        out_specs=pl.BlockSpec((COMBINE_TT, d), lambda i, *_: (i, 0)),
        scratch_shapes=[pltpu.VMEM((TOP_K * COMBINE_TT * SUBLANES, LANES), U32),
                        pltpu.SemaphoreType.DMA(())],
    )
    return pl.pallas_call(
        _combine_kernel,
        grid_spec=grid_spec,
        out_shape=jax.ShapeDtypeStruct((n, d), F32),
        compiler_params=_cparams(("arbitrary",)),
        name="combine_ln",
    )(pos_t, gate_nk, x1, ys, wsg_bf, wsu_bf, wsd_bf, ln_g.reshape(1, d), ln_b.reshape(1, d))


def _block_layout(counts_f, n_assign):
    counts = counts_f.astype(I32)
    padded = (counts + EXPERT_ROWS - 1) // EXPERT_ROWS * EXPERT_ROWS
    end = jnp.cumsum(padded)
    total_rows = n_assign + N_EXPERTS * EXPERT_ROWS
    start = jnp.concatenate([end - padded, jnp.full((1,), total_rows, I32)])
    return start, counts, total_rows


def kernel(x, w_in, conv_w, conv_b, w_rg_a, b_rg_a, w_rg_x, b_rg_x, lru_lambda, w_out, ln1_g, ln1_b,
           w_router, router_bias, w_gate, w_up, w_down, ws_gate, ws_up, ws_down, ln2_g, ln2_b):
    b, s, d = x.shape
    n = b * s
    width = d // 2
    for l in range(DEPTH):
        x2 = x.reshape(n, d)
        proj = _proj(x2, w_in[l].astype(BF16), s, width)
        proj3 = proj.reshape(b, s, proj.shape[1])
        lru = _rglru(proj3, conv_w[l], conv_b[l], w_rg_a[l].astype(BF16), b_rg_a[l],
                     w_rg_x[l].astype(BF16), b_rg_x[l], lru_lambda[l], width)
        att = _attention(proj3, width)
        x1, x1p, idx_t, gate_t, rank_t, cnt = _mix_route(
            lru.reshape(n, width), att.reshape(n, width), x2, w_out[l].astype(BF16),
            ln1_g[l], ln1_b[l], w_router[l], router_bias[l])
        start, counts, total_rows = _block_layout(cnt[:, 0], n * TOP_K)
        onehot = idx_t[:, :, None] == jnp.arange(N_EXPERTS, dtype=I32)[None, None, :]
        pos_t = jnp.sum(jnp.where(onehot, start[None, None, :N_EXPERTS], 0), axis=2) + rank_t
        xs = _dispatch(x1p.reshape(n, SUBLANES, LANES), pos_t, start, counts, total_rows)
        ys = _experts(xs.reshape(total_rows * SUBLANES, LANES), start, counts,
                      w_gate[l], w_up[l], w_down[l])
        out = _combine(x1, ys.reshape(total_rows, SUBLANES, LANES), pos_t, gate_t.T,
                       ws_gate[l].astype(BF16),
                       ws_up[l].astype(BF16), ws_down[l].astype(BF16), ln2_g[l], ln2_b[l])
        x = out.reshape(b, s, d)
    return x
```

```python
import functools

import jax
import jax.numpy as jnp
from jax import lax
from jax.experimental import pallas as pl
from jax.experimental.pallas import tpu as pltpu

F32 = jnp.float32
BF16 = jnp.bfloat16
I32 = jnp.int32
U32 = jnp.uint32

LRU_BLOCKS = 8
CONV_W = 4
LRU_C = 8.0
HEAD_DIM = 128
ROT_DIM = HEAD_DIM // 4
ROPE_THETA = 500000.0
DILATED_PATTERNS = ((128, 1), (512, 4), (2048, 16))
ATTN_BLOCK = 128
N_EXPERTS = 64
N_GROUPS = 8
GROUP_SIZE = N_EXPERTS // N_GROUPS
TOPK_GROUPS = 4
TOP_K = 8
ROUTED_SCALE = 2.5
DEPTH = 1
DN_ALPHA = (2 * DEPTH) ** 0.25
LN_EPS = 1e-5

PROJ_TM = 1024
LRU_TC = 512
MIX_TM = 512
ATTN_UNROLL = 8
EXPERT_ROWS = 512
DISPATCH_TT = 512
COMBINE_TT = 256
ZERO_ROWS = 64
DMA_QUEUES = 2
SUBLANES = 8
LANES = 128
TILE_WORDS = SUBLANES * LANES
VMEM_LIMIT = 56 * 1024 * 1024


def _pack_bf16_pair(lo, hi):
    lo_bits = lax.bitcast_convert_type(lo.astype(BF16).astype(F32), U32)
    hi_bits = lax.bitcast_convert_type(hi.astype(BF16).astype(F32), U32)
    return (lo_bits >> 16) | (hi_bits & jnp.uint32(0xFFFF0000))


def _unpack_bf16_pair(packed):
    lo = lax.bitcast_convert_type(packed << 16, F32)
    hi = lax.bitcast_convert_type(packed & jnp.uint32(0xFFFF0000), F32)
    return lo, hi


def _store_rows_as_tiles(ref, words):
    rows = words.shape[0]
    for s in range(SUBLANES):
        ref[pl.ds(s, rows, stride=SUBLANES), :] = words[:, s * LANES:(s + 1) * LANES]


def _load_rows_from_tiles(ref, row0, rows):
    return jnp.concatenate(
        [ref[pl.ds(row0 * SUBLANES + s, rows, stride=SUBLANES), :] for s in range(SUBLANES)], axis=1)


def _cparams(sem):
    return pltpu.CompilerParams(dimension_semantics=sem, vmem_limit_bytes=VMEM_LIMIT)


def _proj_kernel(x_ref, w_ref, c_ref, sa_ref, sb_ref, o_ref, *, n_heads):
    j = pl.program_id(0)
    acc = jnp.dot(x_ref[...].astype(BF16), w_ref[...], preferred_element_type=F32)
    is_rope = (j == 2) | (j == 3)

    @pl.when(is_rope)
    def _():
        scale = jnp.where(j == 2, HEAD_DIM ** -0.5, 1.0).astype(F32)
        c = c_ref[...] * scale
        sa = sa_ref[...] * scale
        sb = sb_ref[...] * scale
        half = ROT_DIM // 2
        for h in range(n_heads):
            t = acc[:, h * HEAD_DIM:(h + 1) * HEAD_DIM]
            o_ref[:, h * HEAD_DIM:(h + 1) * HEAD_DIM] = (
                t * c + pltpu.roll(t, half, 1) * sa + pltpu.roll(t, HEAD_DIM - half, 1) * sb)

    @pl.when(jnp.logical_not(is_rope))
    def _():
        o_ref[...] = acc


def _rope_tables(seq):
    half = ROT_DIM // 2
    pos = jnp.arange(seq, dtype=F32)
    inv = ROPE_THETA ** (-jnp.arange(0, ROT_DIM, 2, dtype=F32) / ROT_DIM)
    ang = pos[:, None] * inv[None, :]
    cos, sin = jnp.cos(ang), jnp.sin(ang)
    ones = jnp.ones((seq, HEAD_DIM - ROT_DIM), F32)
    zeros = jnp.zeros((seq, HEAD_DIM - ROT_DIM), F32)
    zh = jnp.zeros((seq, half), F32)
    c = jnp.concatenate([cos, cos, ones], -1)
    sa = jnp.concatenate([zh, sin, zeros], -1)
    sb = jnp.concatenate([-sin, zh, zeros], -1)
    return c, sa, sb


def _proj(x2, w_in_bf, seq, width):
    n, d = x2.shape
    e = w_in_bf.shape[1]
    n_groups = e // width
    c, sa, sb = _rope_tables(seq)
    tiles_per_seq = seq // PROJ_TM
    tab_spec = pl.BlockSpec((PROJ_TM, HEAD_DIM), lambda j, i: (i % tiles_per_seq, 0))
    return pl.pallas_call(
        functools.partial(_proj_kernel, n_heads=width // HEAD_DIM),
        grid=(n_groups, n // PROJ_TM),
        in_specs=[pl.BlockSpec((PROJ_TM, d), lambda j, i: (i, 0)),
                  pl.BlockSpec((d, width), lambda j, i: (0, j)),
                  tab_spec, tab_spec, tab_spec],
        out_specs=pl.BlockSpec((PROJ_TM, width), lambda j, i: (i, j)),
        out_shape=jax.ShapeDtypeStruct((n, e), F32),
        compiler_params=_cparams(("arbitrary", "arbitrary")),
        name="proj_rope",
    )(x2, w_in_bf, c, sa, sb)


def _lru_kernel(xl_ref, gl_ref, cw_ref, cb_ref, wa_ref, ba_ref, wx_ref, bx_ref, lam_ref,
                o_ref, xext_ref, hprev_ref, a_ref, b_ref, h_ref, *, block_w):
    tc, width = xl_ref.shape
    groups = tc // SUBLANES

    @pl.when(pl.program_id(1) == 0)
    def _():
        xext_ref[0:SUBLANES, :] = jnp.zeros((SUBLANES, width), F32)
        hprev_ref[...] = jnp.zeros_like(hprev_ref)

    x = xl_ref[...]
    xext_ref[SUBLANES:SUBLANES + tc, :] = x
    xc = x * cw_ref[CONV_W - 1:CONV_W, :] + cb_ref[...]
    for j in range(CONV_W - 1):
        back = CONV_W - 1 - j
        xc = xc + xext_ref[pl.ds(SUBLANES - back, tc), :] * cw_ref[j:j + 1, :]
    xext_ref[0:SUBLANES, :] = x[tc - SUBLANES:, :]

    lam = lam_ref[...]
    softplus_neg_lam = jnp.maximum(-lam, 0.0) + jnp.log1p(jnp.exp(-jnp.abs(lam)))
    rs, is_ = [], []
    for nb in range(width // block_w):
        xb = xc[:, nb * block_w:(nb + 1) * block_w].astype(BF16)
        rs.append(jnp.dot(xb, wa_ref[nb], preferred_element_type=F32))
        is_.append(jnp.dot(xb, wx_ref[nb], preferred_element_type=F32))
    r = jax.nn.sigmoid(jnp.concatenate(rs, axis=1) + ba_ref[...])
    ig = jax.nn.sigmoid(jnp.concatenate(is_, axis=1) + bx_ref[...])
    a = jnp.exp(-LRU_C * r * softplus_neg_lam)
    b = jnp.sqrt(1.0 - a * a) * (ig * xc)

    row = lax.broadcasted_iota(I32, (tc, width), 0) & (SUBLANES - 1)
    k = 1
    while k < SUBLANES:
        keep = row >= k
        a_sh = pltpu.roll(a, k, 0)
        b_sh = pltpu.roll(b, k, 0)
        b = jnp.where(keep, a * b_sh + b, b)
        a = jnp.where(keep, a * a_sh, a)
        k *= 2
    a_ref[...] = a.reshape(groups, SUBLANES, width)
    b_ref[...] = b.reshape(groups, SUBLANES, width)

    def carry(g, hprev):
        h = a_ref[g] * hprev + b_ref[g]
        h_ref[g] = h
        return h[SUBLANES - 1:SUBLANES, :]

    hprev_ref[...] = lax.fori_loop(0, groups, carry, hprev_ref[...])
    h = h_ref[...].reshape(tc, width)
    o_ref[...] = (h * jax.nn.gelu(gl_ref[...], approximate=True)).astype(o_ref.dtype)


def _rglru(proj3, conv_w, conv_b, wa_bf, b_a, wx_bf, b_x, lam, width):
    b, s, _ = proj3.shape
    block_w = width // LRU_BLOCKS
    vec = pl.BlockSpec((1, width), lambda bi, ci: (0, 0))
    wspec = pl.BlockSpec((LRU_BLOCKS, block_w, block_w), lambda bi, ci: (0, 0, 0))
    groups = LRU_TC // SUBLANES
    return pl.pallas_call(
        functools.partial(_lru_kernel, block_w=block_w),
        grid=(b, s // LRU_TC),
        in_specs=[pl.BlockSpec((None, LRU_TC, width), lambda bi, ci: (bi, ci, 0)),
                  pl.BlockSpec((None, LRU_TC, width), lambda bi, ci: (bi, ci, 1)),
                  pl.BlockSpec((CONV_W, width), lambda bi, ci: (0, 0)),
                  vec, wspec, vec, wspec, vec, vec],
        out_specs=pl.BlockSpec((None, LRU_TC, width), lambda bi, ci: (bi, ci, 0)),
        out_shape=jax.ShapeDtypeStruct((b, s, width), BF16),
        scratch_shapes=[pltpu.VMEM((LRU_TC + SUBLANES, width), F32),
                        pltpu.VMEM((1, width), F32),
                        pltpu.VMEM((groups, SUBLANES, width), F32),
                        pltpu.VMEM((groups, SUBLANES, width), F32),
                        pltpu.VMEM((groups, SUBLANES, width), F32)],
        compiler_params=_cparams(("arbitrary", "arbitrary")),
        name="rglru",
    )(proj3, proj3, conv_w, conv_b.reshape(1, width), wa_bf, b_a.reshape(1, width),
      wx_bf, b_x.reshape(1, width), lam.reshape(1, width))


def _attn_kernel(q_ref, k_ref, v_ref, o_ref, qs, ks, vs, acc_s, m_s, l_s, acc_n, m_n, l_n):
    seq = q_ref.shape[0]
    blk = ATTN_BLOCK
    nblk = seq // blk
    n_pat = len(DILATED_PATTERNS)

    @pl.when((pl.program_id(0) == 0) & (pl.program_id(1) == 0))
    def _():
        for p in range(n_pat):
            vs[p, :, HEAD_DIM:] = jnp.ones((seq + blk, HEAD_DIM), BF16)

    for p, (_, d) in enumerate(DILATED_PATTERNS):
        cls_len = seq // d
        ks[p, 0:blk, :] = jnp.zeros((blk, HEAD_DIM), BF16)
        vs[p, 0:blk, 0:HEAD_DIM] = jnp.zeros((blk, HEAD_DIM), BF16)
        for r in range(d):
            src = pl.ds(r, cls_len, stride=d) if d > 1 else pl.ds(0, seq)
            qs[p, r * cls_len:(r + 1) * cls_len, :] = q_ref[src, :].astype(BF16)
            ks[p, blk + r * cls_len:blk + (r + 1) * cls_len, :] = k_ref[src, :].astype(BF16)
            vs[p, blk + r * cls_len:blk + (r + 1) * cls_len, 0:HEAD_DIM] = v_ref[src, :].astype(BF16)

    qi = lax.broadcasted_iota(I32, (blk, 2 * blk), 0)
    kj = lax.broadcasted_iota(I32, (blk, 2 * blk), 1)
    band2 = (kj >= qi) & (kj <= qi + blk)
    in_block = kj >= blk
    qi1 = lax.broadcasted_iota(I32, (blk, blk), 0)
    kj1 = lax.broadcasted_iota(I32, (blk, blk), 1)
    causal1 = kj1 <= qi1
    nt = (((1,), (1,)), ((), ()))

    def one_block(p, row0, first):
        q = qs[p, pl.ds(row0, blk), :]
        if first is True:
            kk = ks[p, pl.ds(row0 + blk, blk), :]
            vv = vs[p, pl.ds(row0 + blk, blk), :]
            mask = causal1
        else:
            kk = ks[p, pl.ds(row0, 2 * blk), :]
            vv = vs[p, pl.ds(row0, 2 * blk), :]
            mask = band2 if first is False else band2 & (in_block | jnp.logical_not(first))
        s = lax.dot_general(q, kk, nt, preferred_element_type=F32)
        s = jnp.where(mask, s, -jnp.inf)
        m = jnp.max(s, axis=1, keepdims=True)
        e = jnp.exp(s - m)
        acc_l = jnp.dot(e.astype(BF16), vv, preferred_element_type=F32)
        acc_s[p, pl.ds(row0, blk), :] = acc_l[:, 0:HEAD_DIM]
        m_s[p, pl.ds(row0, blk), :] = jnp.broadcast_to(m, (blk, HEAD_DIM))
        l_s[p, pl.ds(row0, blk), :] = acc_l[:, HEAD_DIM:]

    def body(it, carry):
        for p, (window, d) in enumerate(DILATED_PATTERNS):
            assert window // d == blk
            blocks_per_class = (seq // d) // blk
            for j in range(ATTN_UNROLL):
                row0 = pl.multiple_of((it * ATTN_UNROLL + j) * blk, blk)
                if ATTN_UNROLL % blocks_per_class == 0:
                    first = j % blocks_per_class == 0
                else:
                    assert blocks_per_class % ATTN_UNROLL == 0
                    first = (it % (blocks_per_class // ATTN_UNROLL) == 0) if j == 0 else False
                one_block(p, row0, first)
        return carry

    lax.fori_loop(0, nblk // ATTN_UNROLL, body, 0)

    for p, (_, d) in enumerate(DILATED_PATTERNS):
        cls_len = seq // d
        for r in range(d):
            dst = pl.ds(r, cls_len, stride=d) if d > 1 else pl.ds(0, seq)
            acc_n[p, dst, :] = acc_s[p, r * cls_len:(r + 1) * cls_len, :]
            m_n[p, dst, :] = m_s[p, r * cls_len:(r + 1) * cls_len, :]
            l_n[p, dst, :] = l_s[p, r * cls_len:(r + 1) * cls_len, :]

    mx = m_n[0]
    for p in range(1, n_pat):
        mx = jnp.maximum(mx, m_n[p])
    num = jnp.zeros((seq, HEAD_DIM), F32)
    den = jnp.zeros((seq, HEAD_DIM), F32)
    for p in range(n_pat):
        w = jnp.exp(m_n[p] - mx)
        num = num + w * acc_n[p]
        den = den + w * l_n[p]
    o_ref[...] = (num / den).astype(o_ref.dtype)


def _attention(proj3, width):
    b, s, _ = proj3.shape
    n_heads = width // HEAD_DIM
    n_pat = len(DILATED_PATTERNS)
    col0 = 2 * width // HEAD_DIM

    def spec(group):
        return pl.BlockSpec((None, s, HEAD_DIM),
                            lambda bi, hi: (bi, 0, col0 + group * n_heads + hi))

    return pl.pallas_call(
        _attn_kernel,
        grid=(b, n_heads),
        in_specs=[spec(0), spec(1), spec(2)],
        out_specs=pl.BlockSpec((None, s, HEAD_DIM), lambda bi, hi: (bi, 0, hi)),
        out_shape=jax.ShapeDtypeStruct((b, s, width), BF16),
        scratch_shapes=[pltpu.VMEM((n_pat, s, HEAD_DIM), BF16),
                        pltpu.VMEM((n_pat, s + ATTN_BLOCK, HEAD_DIM), BF16),
                        pltpu.VMEM((n_pat, s + ATTN_BLOCK, 2 * HEAD_DIM), BF16)]
                       + [pltpu.VMEM((n_pat, s, HEAD_DIM), F32)] * 6,
        compiler_params=_cparams(("arbitrary", "arbitrary")),
        name="dilated_attention",
    )(proj3, proj3, proj3)


def _layer_norm(y, g, b):
    mu = jnp.mean(y, axis=-1, keepdims=True)
    yc = y - mu
    var = jnp.mean(yc * yc, axis=-1, keepdims=True)
    return yc * lax.rsqrt(var + LN_EPS) * g + b


def _first_index(hit, iota, size):
    return jnp.min(jnp.where(hit, iota, size), axis=0, keepdims=True)


def _mix_kernel(lru_ref, att_ref, x_ref, wo_ref, g_ref, b_ref, wrh_ref, wrl_ref, rb_ref, tri_ref,
                x1_ref, x1p_ref, idx_ref, gate_ref, rank_ref, cnt_ref, base_ref):
    tm = x_ref.shape[0]
    half = lru_ref.shape[1]

    @pl.when(pl.program_id(0) == 0)
    def _():
        base_ref[...] = jnp.zeros_like(base_ref)

    mix = jnp.dot(lru_ref[...], wo_ref[0:half, :], preferred_element_type=F32)
    mix = mix + jnp.dot(att_ref[...], wo_ref[half:, :], preferred_element_type=F32)
    x1 = _layer_norm(DN_ALPHA * x_ref[...] + mix, g_ref[...], b_ref[...])
    x1_ref[...] = x1
    _store_rows_as_tiles(x1p_ref, _pack_bf16_pair(x1[:, 0:half], x1[:, half:]))

    x_hi = x1.astype(BF16)
    x_lo = (x1 - x_hi.astype(F32)).astype(BF16)
    nt = (((1,), (1,)), ((), ()))
    logits = lax.dot_general(wrh_ref[...], x_hi, nt, preferred_element_type=F32)
    logits = logits + lax.dot_general(wrh_ref[...], x_lo, nt, preferred_element_type=F32)
    logits = logits + lax.dot_general(wrl_ref[...], x_hi, nt, preferred_element_type=F32)
    scores = jax.nn.sigmoid(logits)
    sel = scores + rb_ref[...]

    neg = -jnp.inf
    gi = lax.broadcasted_iota(I32, (GROUP_SIZE, tm), 0)
    grp_rows = []
    for g in range(N_GROUPS):
        blk = sel[g * GROUP_SIZE:(g + 1) * GROUP_SIZE, :]
        m1 = jnp.max(blk, axis=0, keepdims=True)
        f1 = _first_index(blk == m1, gi, GROUP_SIZE)
        m2 = jnp.max(jnp.where(gi == f1, neg, blk), axis=0, keepdims=True)
        grp_rows.append(m1 + m2)
    ni = lax.broadcasted_iota(I32, (N_GROUPS, tm), 0)
    gscore = jnp.zeros((N_GROUPS, tm), F32)
    for g in range(N_GROUPS):
        gscore = jnp.where(ni == g, grp_rows[g], gscore)
    gmask = jnp.zeros((N_GROUPS, tm), jnp.bool_)
    for _ in range(TOPK_GROUPS):
        m = jnp.max(gscore, axis=0, keepdims=True)
        f = _first_index(gscore == m, ni, N_GROUPS)
        hit = ni == f
        gmask = gmask | hit
        gscore = jnp.where(hit, neg, gscore)
    emask = jnp.concatenate(
        [jnp.broadcast_to(gmask[g:g + 1, :], (GROUP_SIZE, tm)) for g in range(N_GROUPS)], axis=0)
    masked = jnp.where(emask, sel, neg)

    ei = lax.broadcasted_iota(I32, (N_EXPERTS, tm), 0)
    chosen = jnp.zeros((N_EXPERTS, tm), jnp.bool_)
    picks, gates = [], []
    for _ in range(TOP_K):
        m = jnp.max(masked, axis=0, keepdims=True)
        f = _first_index(masked == m, ei, N_EXPERTS)
        hit = ei == f
        picks.append(f)
        gates.append(jnp.sum(jnp.where(hit, scores, 0.0), axis=0, keepdims=True))
        chosen = chosen | hit
        masked = jnp.where(hit, neg, masked)
    gsum = gates[0]
    for k in range(1, TOP_K):
        gsum = gsum + gates[k]

    chosen_f = jnp.where(chosen, 1.0, 0.0)
    incl = jnp.dot(chosen_f.astype(BF16), tri_ref[...], preferred_element_type=F32)
    rank_excl = base_ref[:, 0:1] + incl - chosen_f
    base_ref[...] = base_ref[...] + jnp.sum(chosen_f, axis=1, keepdims=True)
    cnt_ref[...] = base_ref[...]
    for k in range(TOP_K):
        idx_ref[k:k + 1, :] = picks[k]
        gate_ref[k:k + 1, :] = gates[k] / gsum * ROUTED_SCALE
        rk = jnp.sum(jnp.where(ei == picks[k], rank_excl, 0.0), axis=0, keepdims=True)
        rank_ref[k:k + 1, :] = rk.astype(I32)


def _mix_route(lru, att, x2, wo_bf, ln_g, ln_b, w_router, router_bias):
    n, d = x2.shape
    half = lru.shape[1]
    wr_t = w_router.T
    wr_hi = wr_t.astype(BF16)
    wr_lo = (wr_t - wr_hi.astype(F32)).astype(BF16)
    assert half == TILE_WORDS
    tri = jnp.triu(jnp.ones((MIX_TM, MIX_TM), BF16))
    const = lambda shape: pl.BlockSpec(shape, lambda i: (0,) * len(shape))
    tok = lambda dtype: jax.ShapeDtypeStruct((TOP_K, n), dtype)
    tok_spec = pl.BlockSpec((TOP_K, MIX_TM), lambda i: (0, i))
    return pl.pallas_call(
        _mix_kernel,
        grid=(n // MIX_TM,),
        in_specs=[pl.BlockSpec((MIX_TM, half), lambda i: (i, 0)),
                  pl.BlockSpec((MIX_TM, half), lambda i: (i, 0)),
                  pl.BlockSpec((MIX_TM, d), lambda i: (i, 0)),
                  const((d, d)), const((1, d)), const((1, d)),
                  const((N_EXPERTS, d)), const((N_EXPERTS, d)), const((N_EXPERTS, 1)),
                  const((MIX_TM, MIX_TM))],
        out_specs=[pl.BlockSpec((MIX_TM, d), lambda i: (i, 0)),
                   pl.BlockSpec((MIX_TM * SUBLANES, LANES), lambda i: (i, 0)),
                   tok_spec, tok_spec, tok_spec, const((N_EXPERTS, HEAD_DIM))],
        out_shape=[jax.ShapeDtypeStruct((n, d), F32), jax.ShapeDtypeStruct((n * SUBLANES, LANES), U32),
                   tok(I32), tok(F32), tok(I32),
                   jax.ShapeDtypeStruct((N_EXPERTS, HEAD_DIM), F32)],
        scratch_shapes=[pltpu.VMEM((N_EXPERTS, HEAD_DIM), F32)],
        compiler_params=_cparams(("arbitrary",)),
        name="mix_ln_route",
    )(lru, att, x2, wo_bf, ln_g.reshape(1, d), ln_b.reshape(1, d), wr_hi, wr_lo,
      router_bias.reshape(N_EXPERTS, 1), tri)


def _dispatch_kernel(start_ref, cnt_ref, pos_ref, x1_ref, zero_ref, xs_hbm, sem):
    i = pl.program_id(0)
    tt = pos_ref.shape[1]

    def row_copy(t, k):
        return pltpu.make_async_copy(x1_ref.at[t], xs_hbm.at[pos_ref[k, t]], sem)

    def issue(t, c):
        for k in range(TOP_K):
            row_copy(t, k).start(priority=k % DMA_QUEUES)
        return c

    def drain(t, c):
        for k in range(TOP_K):
            row_copy(t, k).wait()
        return c

    lax.fori_loop(0, tt, issue, 0)
    lax.fori_loop(0, tt, drain, 0)

    @pl.when(i == 0)
    def _():
        def pad_copy(r, rows):
            return pltpu.make_async_copy(zero_ref.at[pl.ds(0, rows)], xs_hbm.at[pl.ds(r, rows)], sem)

        def per_expert(e, c):
            lo = start_ref[e] + cnt_ref[e]
            hi = start_ref[e + 1]
            chunks = (hi - lo) // ZERO_ROWS
            mid = lo + chunks * ZERO_ROWS
            big = lambda j: pad_copy(lo + j * ZERO_ROWS, ZERO_ROWS)
            lax.fori_loop(0, chunks, lambda j, cc: (big(j).start(), cc)[1], 0)
            lax.fori_loop(mid, hi, lambda r, cc: (pad_copy(r, 1).start(), cc)[1], 0)
            lax.fori_loop(0, chunks, lambda j, cc: (big(j).wait(), cc)[1], 0)
            lax.fori_loop(mid, hi, lambda r, cc: (pad_copy(r, 1).wait(), cc)[1], 0)
            return c

        lax.fori_loop(0, N_EXPERTS, per_expert, 0)


def _dispatch(x1p, pos_t, start, counts, total_rows):
    n = x1p.shape[0]
    smem_tok = pl.BlockSpec((TOP_K, DISPATCH_TT), lambda i, *_: (0, i), memory_space=pltpu.SMEM)
    grid_spec = pltpu.PrefetchScalarGridSpec(
        num_scalar_prefetch=2,
        grid=(n // DISPATCH_TT,),
        in_specs=[smem_tok,
                  pl.BlockSpec((DISPATCH_TT, SUBLANES, LANES), lambda i, *_: (i, 0, 0)),
                  pl.BlockSpec((ZERO_ROWS, SUBLANES, LANES), lambda i, *_: (0, 0, 0))],
        out_specs=pl.BlockSpec(memory_space=pl.ANY),
        scratch_shapes=[pltpu.SemaphoreType.DMA(())],
    )
    return pl.pallas_call(
        _dispatch_kernel,
        grid_spec=grid_spec,
        out_shape=jax.ShapeDtypeStruct((total_rows, SUBLANES, LANES), x1p.dtype),
        compiler_params=_cparams(("arbitrary",)),
        name="dispatch",
    )(start, counts, pos_t, x1p, jnp.zeros((ZERO_ROWS, SUBLANES, LANES), x1p.dtype))


def _expert_kernel(start_ref, cnt_ref, xs_hbm, wg_ref, wu_ref, wd_ref, ys_hbm,
                   xbuf, ybuf, wg_bf, wu_bf, wd_bf, in_sems, out_sems):
    e = pl.program_id(0)
    row0 = start_ref[e]
    nb = (cnt_ref[e] + EXPERT_ROWS - 1) // EXPERT_ROWS
    block_words = EXPERT_ROWS * SUBLANES
    half = TILE_WORDS

    def rows_of(b):
        return pl.ds(pl.multiple_of((row0 + b * EXPERT_ROWS) * SUBLANES, block_words), block_words)

    def in_copy(b, s):
        return pltpu.make_async_copy(xs_hbm.at[rows_of(b)], xbuf.at[s], in_sems.at[s])

    def out_copy(b, s):
        return pltpu.make_async_copy(ybuf.at[s], ys_hbm.at[rows_of(b)], out_sems.at[s])

    def swiglu(s):
        lo, hi = _unpack_bf16_pair(_load_rows_from_tiles(xbuf.at[s], 0, EXPERT_ROWS))
        lo, hi = lo.astype(BF16), hi.astype(BF16)
        g = (jnp.dot(lo, wg_bf[0:half, :], preferred_element_type=F32)
             + jnp.dot(hi, wg_bf[half:, :], preferred_element_type=F32))
        u = (jnp.dot(lo, wu_bf[0:half, :], preferred_element_type=F32)
             + jnp.dot(hi, wu_bf[half:, :], preferred_element_type=F32))
        h = (g * jax.nn.sigmoid(g) * u).astype(BF16)
        y = jnp.dot(h, wd_bf[...], preferred_element_type=F32)
        _store_rows_as_tiles(ybuf.at[s], _pack_bf16_pair(y[:, 0:half], y[:, half:]))

    def one_block(b, s):
        in_copy(b, s).wait()

        @pl.when(b + 1 < nb)
        def _():
            in_copy(b + 1, 1 - s).start(priority=1)

        @pl.when(b >= 2)
        def _():
            out_copy(b - 2, s).wait()

        swiglu(s)
        out_copy(b, s).start(priority=1)

    def pair(p, c):
        one_block(2 * p, 0)

        @pl.when(2 * p + 1 < nb)
        def _():
            one_block(2 * p + 1, 1)

        return c

    @pl.when(nb > 0)
    def _():
        in_copy(0, 0).start(priority=1)
        wg_bf[...] = wg_ref[...].astype(BF16)
        wu_bf[...] = wu_ref[...].astype(BF16)
        wd_bf[...] = wd_ref[...].astype(BF16)
        lax.fori_loop(0, (nb + 1) // 2, pair, 0)

        @pl.when(nb >= 2)
        def _():
            out_copy(nb - 2, nb % 2).wait()

        out_copy(nb - 1, (nb - 1) % 2).wait()


def _experts(xs, start, counts, w_gate, w_up, w_down):
    d = 2 * TILE_WORDS
    de = w_gate.shape[2]
    block = pltpu.VMEM((2, EXPERT_ROWS * SUBLANES, LANES), U32)
    grid_spec = pltpu.PrefetchScalarGridSpec(
        num_scalar_prefetch=2,
        grid=(N_EXPERTS,),
        in_specs=[pl.BlockSpec(memory_space=pl.ANY),
                  pl.BlockSpec((None, d, de), lambda e, *_: (e, 0, 0)),
                  pl.BlockSpec((None, d, de), lambda e, *_: (e, 0, 0)),
                  pl.BlockSpec((None, de, d), lambda e, *_: (e, 0, 0))],
        out_specs=pl.BlockSpec(memory_space=pl.ANY),
        scratch_shapes=[block, block,
                        pltpu.VMEM((d, de), BF16), pltpu.VMEM((d, de), BF16),
                        pltpu.VMEM((de, d), BF16),
                        pltpu.SemaphoreType.DMA((2,)), pltpu.SemaphoreType.DMA((2,))],
    )
    return pl.pallas_call(
        _expert_kernel,
        grid_spec=grid_spec,
        out_shape=jax.ShapeDtypeStruct(xs.shape, xs.dtype),
        input_output_aliases={2: 0},
        compiler_params=_cparams(("arbitrary",)),
        name="experts",
    )(start, counts, xs, w_gate, w_up, w_down)


def _combine_kernel(pos_ref, gate_ref, x1_ref, ys_hbm, wsg_ref, wsu_ref,
                    wsd_ref, g_ref, b_ref, o_ref, buf, sem):
    tt = x1_ref.shape[0]

    def row_copy(t, k):
        dst = pl.multiple_of((k * tt + t) * SUBLANES, SUBLANES)
        return pltpu.make_async_copy(ys_hbm.at[pos_ref[k, t]], buf.at[pl.ds(dst, SUBLANES)], sem)

    def gather(start):
        def step(t, c):
            for k in range(TOP_K):
                copy = row_copy(t, k)
                copy.start(priority=k % DMA_QUEUES) if start else copy.wait()
            return c
        lax.fori_loop(0, tt, step, 0)

    gather(True)

    x1 = x1_ref[...]
    xb = x1.astype(BF16)
    g = jnp.dot(xb, wsg_ref[...], preferred_element_type=F32)
    u = jnp.dot(xb, wsu_ref[...], preferred_element_type=F32)
    h = (g * jax.nn.sigmoid(g) * u).astype(BF16)
    y = DN_ALPHA * x1 + jnp.dot(h, wsd_ref[...], preferred_element_type=F32)

    gather(False)
    gate = gate_ref[...]
    r_lo = jnp.zeros((tt, TILE_WORDS), F32)
    r_hi = jnp.zeros((tt, TILE_WORDS), F32)
    for k in range(TOP_K):
        lo, hi = _unpack_bf16_pair(_load_rows_from_tiles(buf, k * tt, tt))
        r_lo = r_lo + lo * gate[:, k:k + 1]
        r_hi = r_hi + hi * gate[:, k:k + 1]
    y = y + jnp.concatenate([r_lo, r_hi], axis=1)
    o_ref[...] = _layer_norm(y, g_ref[...], b_ref[...])


def _combine(x1, ys, pos_t, gate_nk, wsg_bf, wsu_bf, wsd_bf, ln_g, ln_b):
    n, d = x1.shape
    de = wsg_bf.shape[1]
    smem_tok = pl.BlockSpec((TOP_K, COMBINE_TT), lambda i, *_: (0, i), memory_space=pltpu.SMEM)
    const = lambda shape: pl.BlockSpec(shape, lambda i, *_: (0,) * len(shape))
    grid_spec = pltpu.PrefetchScalarGridSpec(
        num_scalar_prefetch=0,
        grid=(n // COMBINE_TT,),
        in_specs=[smem_tok,
                  pl.BlockSpec((COMBINE_TT, TOP_K), lambda i, *_: (i, 0)),
                  pl.BlockSpec((COMBINE_TT, d), lambda i, *_: (i, 0)),
                  pl.BlockSpec(memory_space=pl.ANY),
                  const((d, de)), const((d, de)), const((de, d)), const((1, d)), const((1, d))],
        out_specs=pl.BlockSpec((COMBINE_TT, d), lambda i, *_: (i, 0)),
        scratch_shapes=[pltpu.VMEM((TOP_K * COMBINE_TT * SUBLANES, LANES), U32),
                        pltpu.SemaphoreType.DMA(())],
    )
    return pl.pallas_call(
        _combine_kernel,
        grid_spec=grid_spec,
        out_shape=jax.ShapeDtypeStruct((n, d), F32),
        compiler_params=_cparams(("arbitrary",)),
        name="combine_ln",
    )(pos_t, gate_nk, x1, ys, wsg_bf, wsu_bf, wsd_bf, ln_g.reshape(1, d), ln_b.reshape(1, d))


def _block_layout(counts_f, n_assign):
    counts = counts_f.astype(I32)
    padded = (counts + EXPERT_ROWS - 1) // EXPERT_ROWS * EXPERT_ROWS
    end = jnp.cumsum(padded)
    total_rows = n_assign + N_EXPERTS * EXPERT_ROWS
    start = jnp.concatenate([end - padded, jnp.full((1,), total_rows, I32)])
    return start, counts, total_rows


def kernel(x, w_in, conv_w, conv_b, w_rg_a, b_rg_a, w_rg_x, b_rg_x, lru_lambda, w_out, ln1_g, ln1_b,
           w_router, router_bias, w_gate, w_up, w_down, ws_gate, ws_up, ws_down, ln2_g, ln2_b):
    b, s, d = x.shape
    n = b * s
    width = d // 2
    for l in range(DEPTH):
        x2 = x.reshape(n, d)
        proj = _proj(x2, w_in[l].astype(BF16), s, width)
        proj3 = proj.reshape(b, s, proj.shape[1])
        lru = _rglru(proj3, conv_w[l], conv_b[l], w_rg_a[l].astype(BF16), b_rg_a[l],
                     w_rg_x[l].astype(BF16), b_rg_x[l], lru_lambda[l], width)
        att = _attention(proj3, width)
        x1, x1p, idx_t, gate_t, rank_t, cnt = _mix_route(
            lru.reshape(n, width), att.reshape(n, width), x2, w_out[l].astype(BF16),
            ln1_g[l], ln1_b[l], w_router[l], router_bias[l])
        start, counts, total_rows = _block_layout(cnt[:, 0], n * TOP_K)
        onehot = idx_t[:, :, None] == jnp.arange(N_EXPERTS, dtype=I32)[None, None, :]
        pos_t = jnp.sum(jnp.where(onehot, start[None, None, :N_EXPERTS], 0), axis=2) + rank_t
        xs = _dispatch(x1p.reshape(n, SUBLANES, LANES), pos_t, start, counts, total_rows)
        ys = _experts(xs.reshape(total_rows * SUBLANES, LANES), start, counts,
                      w_gate[l], w_up[l], w_down[l])
        out = _combine(x1, ys.reshape(total_rows, SUBLANES, LANES), pos_t, gate_t.T,
                       ws_gate[l].astype(BF16),
                       ws_up[l].astype(BF16), ws_down[l].astype(BF16), ln2_g[l], ln2_b[l])
        x = out.reshape(b, s, d)
    return x
```

```python
import functools

import jax
import jax.numpy as jnp
from jax import lax
from jax.experimental import pallas as pl
from jax.experimental.pallas import tpu as pltpu

F32 = jnp.float32
BF16 = jnp.bfloat16
I32 = jnp.int32
U32 = jnp.uint32

LRU_BLOCKS = 8
CONV_W = 4
LRU_C = 8.0
HEAD_DIM = 128
ROT_DIM = HEAD_DIM // 4
ROPE_THETA = 500000.0
DILATED_PATTERNS = ((128, 1), (512, 4), (2048, 16))
ATTN_BLOCK = 128
N_EXPERTS = 64
N_GROUPS = 8
GROUP_SIZE = N_EXPERTS // N_GROUPS
TOPK_GROUPS = 4
TOP_K = 8
ROUTED_SCALE = 2.5
DEPTH = 1
DN_ALPHA = (2 * DEPTH) ** 0.25
LN_EPS = 1e-5

PROJ_TM = 1024
LRU_TC = 512
MIX_TM = 512
ATTN_UNROLL = 8
EXPERT_ROWS = 512
DISPATCH_TT = 512
COMBINE_TT = 256
ZERO_ROWS = 64
DMA_QUEUES = 2
SUBLANES = 8
LANES = 128
TILE_WORDS = SUBLANES * LANES
VMEM_LIMIT = 56 * 1024 * 1024


def _pack_bf16_pair(lo, hi):
    lo_bits = lax.bitcast_convert_type(lo.astype(BF16).astype(F32), U32)
    hi_bits = lax.bitcast_convert_type(hi.astype(BF16).astype(F32), U32)
    return (lo_bits >> 16) | (hi_bits & jnp.uint32(0xFFFF0000))


def _unpack_bf16_pair(packed):
    lo = lax.bitcast_convert_type(packed << 16, F32)
    hi = lax.bitcast_convert_type(packed & jnp.uint32(0xFFFF0000), F32)
    return lo, hi


def _store_rows_as_tiles(ref, words):
    rows = words.shape[0]
    for s in range(SUBLANES):
        ref[pl.ds(s, rows, stride=SUBLANES), :] = words[:, s * LANES:(s + 1) * LANES]


def _load_rows_from_tiles(ref, row0, rows):
    return jnp.concatenate(
        [ref[pl.ds(row0 * SUBLANES + s, rows, stride=SUBLANES), :] for s in range(SUBLANES)], axis=1)


def _cparams(sem):
    return pltpu.CompilerParams(dimension_semantics=sem, vmem_limit_bytes=VMEM_LIMIT)


def _proj_kernel(x_ref, w_ref, c_ref, sa_ref, sb_ref, o_ref, *, n_heads):
    j = pl.program_id(0)
    acc = jnp.dot(x_ref[...].astype(BF16), w_ref[...], preferred_element_type=F32)
    is_rope = (j == 2) | (j == 3)

    @pl.when(is_rope)
    def _():
        scale = jnp.where(j == 2, HEAD_DIM ** -0.5, 1.0).astype(F32)
        c = c_ref[...] * scale
        sa = sa_ref[...] * scale
        sb = sb_ref[...] * scale
        half = ROT_DIM // 2
        for h in range(n_heads):
            t = acc[:, h * HEAD_DIM:(h + 1) * HEAD_DIM]
            o_ref[:, h * HEAD_DIM:(h + 1) * HEAD_DIM] = (
                t * c + pltpu.roll(t, half, 1) * sa + pltpu.roll(t, HEAD_DIM - half, 1) * sb)

    @pl.when(jnp.logical_not(is_rope))
    def _():
        o_ref[...] = acc


def _rope_tables(seq):
    half = ROT_DIM // 2
    pos = jnp.arange(seq, dtype=F32)
    inv = ROPE_THETA ** (-jnp.arange(0, ROT_DIM, 2, dtype=F32) / ROT_DIM)
    ang = pos[:, None] * inv[None, :]
    cos, sin = jnp.cos(ang), jnp.sin(ang)
    ones = jnp.ones((seq, HEAD_DIM - ROT_DIM), F32)
    zeros = jnp.zeros((seq, HEAD_DIM - ROT_DIM), F32)
    zh = jnp.zeros((seq, half), F32)
    c = jnp.concatenate([cos, cos, ones], -1)
    sa = jnp.concatenate([zh, sin, zeros], -1)
    sb = jnp.concatenate([-sin, zh, zeros], -1)
    return c, sa, sb


def _proj(x2, w_in_bf, seq, width):
    n, d = x2.shape
    e = w_in_bf.shape[1]
    n_groups = e // width
    c, sa, sb = _rope_tables(seq)
    tiles_per_seq = seq // PROJ_TM
    tab_spec = pl.BlockSpec((PROJ_TM, HEAD_DIM), lambda j, i: (i % tiles_per_seq, 0))
    return pl.pallas_call(
        functools.partial(_proj_kernel, n_heads=width // HEAD_DIM),
        grid=(n_groups, n // PROJ_TM),
        in_specs=[pl.BlockSpec((PROJ_TM, d), lambda j, i: (i, 0)),
                  pl.BlockSpec((d, width), lambda j, i: (0, j)),
                  tab_spec, tab_spec, tab_spec],
        out_specs=pl.BlockSpec((PROJ_TM, width), lambda j, i: (i, j)),
        out_shape=jax.ShapeDtypeStruct((n, e), F32),
        compiler_params=_cparams(("arbitrary", "arbitrary")),
        name="proj_rope",
    )(x2, w_in_bf, c, sa, sb)


def _lru_kernel(xl_ref, gl_ref, cw_ref, cb_ref, wa_ref, ba_ref, wx_ref, bx_ref, lam_ref,
                o_ref, xext_ref, hprev_ref, a_ref, b_ref, h_ref, *, block_w):
    tc, width = xl_ref.shape
    groups = tc // SUBLANES

    @pl.when(pl.program_id(1) == 0)
    def _():
        xext_ref[0:SUBLANES, :] = jnp.zeros((SUBLANES, width), F32)
        hprev_ref[...] = jnp.zeros_like(hprev_ref)

    x = xl_ref[...]
    xext_ref[SUBLANES:SUBLANES + tc, :] = x
    xc = x * cw_ref[CONV_W - 1:CONV_W, :] + cb_ref[...]
    for j in range(CONV_W - 1):
        back = CONV_W - 1 - j
        xc = xc + xext_ref[pl.ds(SUBLANES - back, tc), :] * cw_ref[j:j + 1, :]
    xext_ref[0:SUBLANES, :] = x[tc - SUBLANES:, :]

    lam = lam_ref[...]
    softplus_neg_lam = jnp.maximum(-lam, 0.0) + jnp.log1p(jnp.exp(-jnp.abs(lam)))
    rs, is_ = [], []
    for nb in range(width // block_w):
        xb = xc[:, nb * block_w:(nb + 1) * block_w].astype(BF16)
        rs.append(jnp.dot(xb, wa_ref[nb], preferred_element_type=F32))
        is_.append(jnp.dot(xb, wx_ref[nb], preferred_element_type=F32))
    r = jax.nn.sigmoid(jnp.concatenate(rs, axis=1) + ba_ref[...])
    ig = jax.nn.sigmoid(jnp.concatenate(is_, axis=1) + bx_ref[...])
    a = jnp.exp(-LRU_C * r * softplus_neg_lam)
    b = jnp.sqrt(1.0 - a * a) * (ig * xc)

    row = lax.broadcasted_iota(I32, (tc, width), 0) & (SUBLANES - 1)
    k = 1
    while k < SUBLANES:
        keep = row >= k
        a_sh = pltpu.roll(a, k, 0)
        b_sh = pltpu.roll(b, k, 0)
        b = jnp.where(keep, a * b_sh + b, b)
        a = jnp.where(keep, a * a_sh, a)
        k *= 2
    a_ref[...] = a.reshape(groups, SUBLANES, width)
    b_ref[...] = b.reshape(groups, SUBLANES, width)

    def carry(g, hprev):
        h = a_ref[g] * hprev + b_ref[g]
        h_ref[g] = h
        return h[SUBLANES - 1:SUBLANES, :]

    hprev_ref[...] = lax.fori_loop(0, groups, carry, hprev_ref[...])
    h = h_ref[...].reshape(tc, width)
    o_ref[...] = (h * jax.nn.gelu(gl_ref[...], approximate=True)).astype(o_ref.dtype)


def _rglru(proj3, conv_w, conv_b, wa_bf, b_a, wx_bf, b_x, lam, width):
    b, s, _ = proj3.shape
    block_w = width // LRU_BLOCKS
    vec = pl.BlockSpec((1, width), lambda bi, ci: (0, 0))
    wspec = pl.BlockSpec((LRU_BLOCKS, block_w, block_w), lambda bi, ci: (0, 0, 0))
    groups = LRU_TC // SUBLANES
    return pl.pallas_call(
        functools.partial(_lru_kernel, block_w=block_w),
        grid=(b, s // LRU_TC),
        in_specs=[pl.BlockSpec((None, LRU_TC, width), lambda bi, ci: (bi, ci, 0)),
                  pl.BlockSpec((None, LRU_TC, width), lambda bi, ci: (bi, ci, 1)),
                  pl.BlockSpec((CONV_W, width), lambda bi, ci: (0, 0)),
                  vec, wspec, vec, wspec, vec, vec],
        out_specs=pl.BlockSpec((None, LRU_TC, width), lambda bi, ci: (bi, ci, 0)),
        out_shape=jax.ShapeDtypeStruct((b, s, width), BF16),
        scratch_shapes=[pltpu.VMEM((LRU_TC + SUBLANES, width), F32),
                        pltpu.VMEM((1, width), F32),
                        pltpu.VMEM((groups, SUBLANES, width), F32),
                        pltpu.VMEM((groups, SUBLANES, width), F32),
                        pltpu.VMEM((groups, SUBLANES, width), F32)],
        compiler_params=_cparams(("arbitrary", "arbitrary")),
        name="rglru",
    )(proj3, proj3, conv_w, conv_b.reshape(1, width), wa_bf, b_a.reshape(1, width),
      wx_bf, b_x.reshape(1, width), lam.reshape(1, width))


def _attn_kernel(q_ref, k_ref, v_ref, o_ref, qs, ks, vs, acc_s, m_s, l_s, acc_n, m_n, l_n):
    seq = q_ref.shape[0]
    blk = ATTN_BLOCK
    nblk = seq // blk
    n_pat = len(DILATED_PATTERNS)

    @pl.when((pl.program_id(0) == 0) & (pl.program_id(1) == 0))
    def _():
        for p in range(n_pat):
            vs[p, :, HEAD_DIM:] = jnp.ones((seq + blk, HEAD_DIM), BF16)

    for p, (_, d) in enumerate(DILATED_PATTERNS):
        cls_len = seq // d
        ks[p, 0:blk, :] = jnp.zeros((blk, HEAD_DIM), BF16)
        vs[p, 0:blk, 0:HEAD_DIM] = jnp.zeros((blk, HEAD_DIM), BF16)
        for r in range(d):
            src = pl.ds(r, cls_len, stride=d) if d > 1 else pl.ds(0, seq)
            qs[p, r * cls_len:(r + 1) * cls_len, :] = q_ref[src, :].astype(BF16)
            ks[p, blk + r * cls_len:blk + (r + 1) * cls_len, :] = k_ref[src, :].astype(BF16)
            vs[p, blk + r * cls_len:blk + (r + 1) * cls_len, 0:HEAD_DIM] = v_ref[src, :].astype(BF16)

    qi = lax.broadcasted_iota(I32, (blk, 2 * blk), 0)
    kj = lax.broadcasted_iota(I32, (blk, 2 * blk), 1)
    band2 = (kj >= qi) & (kj <= qi + blk)
    in_block = kj >= blk
    qi1 = lax.broadcasted_iota(I32, (blk, blk), 0)
    kj1 = lax.broadcasted_iota(I32, (blk, blk), 1)
    causal1 = kj1 <= qi1
    nt = (((1,), (1,)), ((), ()))

    def one_block(p, row0, first):
        q = qs[p, pl.ds(row0, blk), :]
        if first is True:
            kk = ks[p, pl.ds(row0 + blk, blk), :]
            vv = vs[p, pl.ds(row0 + blk, blk), :]
            mask = causal1
        else:
            kk = ks[p, pl.ds(row0, 2 * blk), :]
            vv = vs[p, pl.ds(row0, 2 * blk), :]
            mask = band2 if first is False else band2 & (in_block | jnp.logical_not(first))
        s = lax.dot_general(q, kk, nt, preferred_element_type=F32)
        s = jnp.where(mask, s, -jnp.inf)
        m = jnp.max(s, axis=1, keepdims=True)
        e = jnp.exp(s - m)
        acc_l = jnp.dot(e.astype(BF16), vv, preferred_element_type=F32)
        acc_s[p, pl.ds(row0, blk), :] = acc_l[:, 0:HEAD_DIM]
        m_s[p, pl.ds(row0, blk), :] = jnp.broadcast_to(m, (blk, HEAD_DIM))
        l_s[p, pl.ds(row0, blk), :] = acc_l[:, HEAD_DIM:]

    def body(it, carry):
        for p, (window, d) in enumerate(DILATED_PATTERNS):
            assert window // d == blk
            blocks_per_class = (seq // d) // blk
            for j in range(ATTN_UNROLL):
                row0 = pl.multiple_of((it * ATTN_UNROLL + j) * blk, blk)
                if ATTN_UNROLL % blocks_per_class == 0:
                    first = j % blocks_per_class == 0
                else:
                    assert blocks_per_class % ATTN_UNROLL == 0
                    first = (it % (blocks_per_class // ATTN_UNROLL) == 0) if j == 0 else False
                one_block(p, row0, first)
        return carry

    lax.fori_loop(0, nblk // ATTN_UNROLL, body, 0)

    for p, (_, d) in enumerate(DILATED_PATTERNS):
        cls_len = seq // d
        for r in range(d):
            dst = pl.ds(r, cls_len, stride=d) if d > 1 else pl.ds(0, seq)
            acc_n[p, dst, :] = acc_s[p, r * cls_len:(r + 1) * cls_len, :]
            m_n[p, dst, :] = m_s[p, r * cls_len:(r + 1) * cls_len, :]
            l_n[p, dst, :] = l_s[p, r * cls_len:(r + 1) * cls_len, :]

    mx = m_n[0]
    for p in range(1, n_pat):
        mx = jnp.maximum(mx, m_n[p])
    num = jnp.zeros((seq, HEAD_DIM), F32)
    den = jnp.zeros((seq, HEAD_DIM), F32)
    for p in range(n_pat):
        w = jnp.exp(m_n[p] - mx)
        num = num + w * acc_n[p]
        den = den + w * l_n[p]
    o_ref[...] = (num / den).astype(o_ref.dtype)


def _attention(proj3, width):
    b, s, _ = proj3.shape
    n_heads = width // HEAD_DIM
    n_pat = len(DILATED_PATTERNS)
    col0 = 2 * width // HEAD_DIM

    def spec(group):
        return pl.BlockSpec((None, s, HEAD_DIM),
                            lambda bi, hi: (bi, 0, col0 + group * n_heads + hi))

    return pl.pallas_call(
        _attn_kernel,
        grid=(b, n_heads),
        in_specs=[spec(0), spec(1), spec(2)],
        out_specs=pl.BlockSpec((None, s, HEAD_DIM), lambda bi, hi: (bi, 0, hi)),
        out_shape=jax.ShapeDtypeStruct((b, s, width), BF16),
        scratch_shapes=[pltpu.VMEM((n_pat, s, HEAD_DIM), BF16),
                        pltpu.VMEM((n_pat, s + ATTN_BLOCK, HEAD_DIM), BF16),
                        pltpu.VMEM((n_pat, s + ATTN_BLOCK, 2 * HEAD_DIM), BF16)]
                       + [pltpu.VMEM((n_pat, s, HEAD_DIM), F32)] * 6,
        compiler_params=_cparams(("arbitrary", "arbitrary")),
        name="dilated_attention",
    )(proj3, proj3, proj3)


def _layer_norm(y, g, b):
    mu = jnp.mean(y, axis=-1, keepdims=True)
    yc = y - mu
    var = jnp.mean(yc * yc, axis=-1, keepdims=True)
    return yc * lax.rsqrt(var + LN_EPS) * g + b


def _first_index(hit, iota, size):
    return jnp.min(jnp.where(hit, iota, size), axis=0, keepdims=True)


def _mix_kernel(lru_ref, att_ref, x_ref, wo_ref, g_ref, b_ref, wrh_ref, wrl_ref, rb_ref, tri_ref,
                x1_ref, x1p_ref, idx_ref, gate_ref, rank_ref, cnt_ref, base_ref):
    tm = x_ref.shape[0]
    half = lru_ref.shape[1]

    @pl.when(pl.program_id(0) == 0)
    def _():
        base_ref[...] = jnp.zeros_like(base_ref)

    mix = jnp.dot(lru_ref[...], wo_ref[0:half, :], preferred_element_type=F32)
    mix = mix + jnp.dot(att_ref[...], wo_ref[half:, :], preferred_element_type=F32)
    x1 = _layer_norm(DN_ALPHA * x_ref[...] + mix, g_ref[...], b_ref[...])
    x1_ref[...] = x1
    _store_rows_as_tiles(x1p_ref, _pack_bf16_pair(x1[:, 0:half], x1[:, half:]))

    x_hi = x1.astype(BF16)
    x_lo = (x1 - x_hi.astype(F32)).astype(BF16)
    nt = (((1,), (1,)), ((), ()))
    logits = lax.dot_general(wrh_ref[...], x_hi, nt, preferred_element_type=F32)
    logits = logits + lax.dot_general(wrh_ref[...], x_lo, nt, preferred_element_type=F32)
    logits = logits + lax.dot_general(wrl_ref[...], x_hi, nt, preferred_element_type=F32)
    scores = jax.nn.sigmoid(logits)
    sel = scores + rb_ref[...]

    neg = -jnp.inf
    gi = lax.broadcasted_iota(I32, (GROUP_SIZE, tm), 0)
    grp_rows = []
    for g in range(N_GROUPS):
        blk = sel[g * GROUP_SIZE:(g + 1) * GROUP_SIZE, :]
        m1 = jnp.max(blk, axis=0, keepdims=True)
        f1 = _first_index(blk == m1, gi, GROUP_SIZE)
        m2 = jnp.max(jnp.where(gi == f1, neg, blk), axis=0, keepdims=True)
        grp_rows.append(m1 + m2)
    ni = lax.broadcasted_iota(I32, (N_GROUPS, tm), 0)
    gscore = jnp.zeros((N_GROUPS, tm), F32)
    for g in range(N_GROUPS):
        gscore = jnp.where(ni == g, grp_rows[g], gscore)
    gmask = jnp.zeros((N_GROUPS, tm), jnp.bool_)
    for _ in range(TOPK_GROUPS):
        m = jnp.max(gscore, axis=0, keepdims=True)
        f = _first_index(gscore == m, ni, N_GROUPS)
        hit = ni == f
        gmask = gmask | hit
        gscore = jnp.where(hit, neg, gscore)
    emask = jnp.concatenate(
        [jnp.broadcast_to(gmask[g:g + 1, :], (GROUP_SIZE, tm)) for g in range(N_GROUPS)], axis=0)
    masked = jnp.where(emask, sel, neg)

    ei = lax.broadcasted_iota(I32, (N_EXPERTS, tm), 0)
    chosen = jnp.zeros((N_EXPERTS, tm), jnp.bool_)
    picks, gates = [], []
    for _ in range(TOP_K):
        m = jnp.max(masked, axis=0, keepdims=True)
        f = _first_index(masked == m, ei, N_EXPERTS)
        hit = ei == f
        picks.append(f)
        gates.append(jnp.sum(jnp.where(hit, scores, 0.0), axis=0, keepdims=True))
        chosen = chosen | hit
        masked = jnp.where(hit, neg, masked)
    gsum = gates[0]
    for k in range(1, TOP_K):
        gsum = gsum + gates[k]

    chosen_f = jnp.where(chosen, 1.0, 0.0)
    incl = jnp.dot(chosen_f.astype(BF16), tri_ref[...], preferred_element_type=F32)
    rank_excl = base_ref[:, 0:1] + incl - chosen_f
    base_ref[...] = base_ref[...] + jnp.sum(chosen_f, axis=1, keepdims=True)
    cnt_ref[...] = base_ref[...]
    for k in range(TOP_K):
        idx_ref[k:k + 1, :] = picks[k]
        gate_ref[k:k + 1, :] = gates[k] / gsum * ROUTED_SCALE
        rk = jnp.sum(jnp.where(ei == picks[k], rank_excl, 0.0), axis=0, keepdims=True)
        rank_ref[k:k + 1, :] = rk.astype(I32)


def _mix_route(lru, att, x2, wo_bf, ln_g, ln_b, w_router, router_bias):
    n, d = x2.shape
    half = lru.shape[1]
    wr_t = w_router.T
    wr_hi = wr_t.astype(BF16)
    wr_lo = (wr_t - wr_hi.astype(F32)).astype(BF16)
    assert half == TILE_WORDS
    tri = jnp.triu(jnp.ones((MIX_TM, MIX_TM), BF16))
    const = lambda shape: pl.BlockSpec(shape, lambda i: (0,) * len(shape))
    tok = lambda dtype: jax.ShapeDtypeStruct((TOP_K, n), dtype)
    tok_spec = pl.BlockSpec((TOP_K, MIX_TM), lambda i: (0, i))
    return pl.pallas_call(
        _mix_kernel,
        grid=(n // MIX_TM,),
        in_specs=[pl.BlockSpec((MIX_TM, half), lambda i: (i, 0)),
                  pl.BlockSpec((MIX_TM, half), lambda i: (i, 0)),
                  pl.BlockSpec((MIX_TM, d), lambda i: (i, 0)),
                  const((d, d)), const((1, d)), const((1, d)),
                  const((N_EXPERTS, d)), const((N_EXPERTS, d)), const((N_EXPERTS, 1)),
                  const((MIX_TM, MIX_TM))],
        out_specs=[pl.BlockSpec((MIX_TM, d), lambda i: (i, 0)),
                   pl.BlockSpec((MIX_TM * SUBLANES, LANES), lambda i: (i, 0)),
                   tok_spec, tok_spec, tok_spec, const((N_EXPERTS, HEAD_DIM))],
        out_shape=[jax.ShapeDtypeStruct((n, d), F32), jax.ShapeDtypeStruct((n * SUBLANES, LANES), U32),
                   tok(I32), tok(F32), tok(I32),
                   jax.ShapeDtypeStruct((N_EXPERTS, HEAD_DIM), F32)],
        scratch_shapes=[pltpu.VMEM((N_EXPERTS, HEAD_DIM), F32)],
        compiler_params=_cparams(("arbitrary",)),
        name="mix_ln_route",
    )(lru, att, x2, wo_bf, ln_g.reshape(1, d), ln_b.reshape(1, d), wr_hi, wr_lo,
      router_bias.reshape(N_EXPERTS, 1), tri)


def _dispatch_kernel(start_ref, cnt_ref, pos_ref, x1_ref, zero_ref, xs_hbm, sem):
    i = pl.program_id(0)
    tt = pos_ref.shape[1]

    def row_copy(t, k):
        return pltpu.make_async_copy(x1_ref.at[t], xs_hbm.at[pos_ref[k, t]], sem)

    def issue(t, c):
        for k in range(TOP_K):
            row_copy(t, k).start(priority=k % DMA_QUEUES)
        return c

    def drain(t, c):
        for k in range(TOP_K):
            row_copy(t, k).wait()
        return c

    lax.fori_loop(0, tt, issue, 0)
    lax.fori_loop(0, tt, drain, 0)

    @pl.when(i == 0)
    def _():
        def pad_copy(r, rows):
            return pltpu.make_async_copy(zero_ref.at[pl.ds(0, rows)], xs_hbm.at[pl.ds(r, rows)], sem)

        def per_expert(e, c):
            lo = start_ref[e] + cnt_ref[e]
            hi = start_ref[e + 1]
            chunks = (hi - lo) // ZERO_ROWS
            mid = lo + chunks * ZERO_ROWS
            big = lambda j: pad_copy(lo + j * ZERO_ROWS, ZERO_ROWS)
            lax.fori_loop(0, chunks, lambda j, cc: (big(j).start(), cc)[1], 0)
            lax.fori_loop(mid, hi, lambda r, cc: (pad_copy(r, 1).start(), cc)[1], 0)
            lax.fori_loop(0, chunks, lambda j, cc: (big(j).wait(), cc)[1], 0)
            lax.fori_loop(mid, hi, lambda r, cc: (pad_copy(r, 1).wait(), cc)[1], 0)
            return c

        lax.fori_loop(0, N_EXPERTS, per_expert, 0)


def _dispatch(x1p, pos_t, start, counts, total_rows):
    n = x1p.shape[0]
    smem_tok = pl.BlockSpec((TOP_K, DISPATCH_TT), lambda i, *_: (0, i), memory_space=pltpu.SMEM)
    grid_spec = pltpu.PrefetchScalarGridSpec(
        num_scalar_prefetch=2,
        grid=(n // DISPATCH_TT,),
        in_specs=[smem_tok,
                  pl.BlockSpec((DISPATCH_TT, SUBLANES, LANES), lambda i, *_: (i, 0, 0)),
                  pl.BlockSpec((ZERO_ROWS, SUBLANES, LANES), lambda i, *_: (0, 0, 0))],
        out_specs=pl.BlockSpec(memory_space=pl.ANY),
        scratch_shapes=[pltpu.SemaphoreType.DMA(())],
    )
    return pl.pallas_call(
        _dispatch_kernel,
        grid_spec=grid_spec,
        out_shape=jax.ShapeDtypeStruct((total_rows, SUBLANES, LANES), x1p.dtype),
        compiler_params=_cparams(("arbitrary",)),
        name="dispatch",
    )(start, counts, pos_t, x1p, jnp.zeros((ZERO_ROWS, SUBLANES, LANES), x1p.dtype))


def _expert_kernel(be_ref, used_ref, nxt_ref, slot_ref, xs_ref, wg_hbm, wu_hbm, wd_hbm, ys_ref,
                   wg_f32, wu_f32, wd_f32, wg_bf, wu_bf, wd_bf, sems):
    i = pl.program_id(0)
    e = be_ref[i]
    slot = slot_ref[i]
    changed = (i == 0) | (e != be_ref[jnp.maximum(i - 1, 0)])

    def weight_copies(expert, s):
        return (pltpu.make_async_copy(wg_hbm.at[expert], wg_f32.at[s], sems.at[0, s]),
                pltpu.make_async_copy(wu_hbm.at[expert], wu_f32.at[s], sems.at[1, s]),
                pltpu.make_async_copy(wd_hbm.at[expert], wd_f32.at[s], sems.at[2, s]))

    @pl.when(i == 0)
    def _():
        for c in weight_copies(e, slot):
            c.start(priority=1)

    @pl.when(changed)
    def _():
        for c in weight_copies(e, slot):
            c.wait()
        nxt = nxt_ref[i]

        @pl.when(nxt != e)
        def _():
            for c in weight_copies(nxt, 1 - slot):
                c.start(priority=1)

        wg_bf[...] = wg_f32[slot].astype(BF16)
        wu_bf[...] = wu_f32[slot].astype(BF16)
        wd_bf[...] = wd_f32[slot].astype(BF16)

    @pl.when(i < used_ref[0])
    def _():
        half = TILE_WORDS
        lo, hi = _unpack_bf16_pair(_load_rows_from_tiles(xs_ref, 0, EXPERT_ROWS))
        lo, hi = lo.astype(BF16), hi.astype(BF16)
        g = (jnp.dot(lo, wg_bf[0:half, :], preferred_element_type=F32)
             + jnp.dot(hi, wg_bf[half:, :], preferred_element_type=F32))
        u = (jnp.dot(lo, wu_bf[0:half, :], preferred_element_type=F32)
             + jnp.dot(hi, wu_bf[half:, :], preferred_element_type=F32))
        h = (g * jax.nn.sigmoid(g) * u).astype(BF16)
        y = jnp.dot(h, wd_bf[...], preferred_element_type=F32)
        _store_rows_as_tiles(ys_ref, _pack_bf16_pair(y[:, 0:half], y[:, half:]))

    @pl.when(i >= used_ref[0])
    def _():
        ys_ref[...] = jnp.zeros_like(ys_ref)


def _experts(xs, blk_expert, used, w_gate, w_up, w_down):
    rows = xs.shape[0] // SUBLANES
    d = 2 * TILE_WORDS
    de = w_gate.shape[2]
    nblk = rows // EXPERT_ROWS
    later = blk_expert[None, :] > blk_expert[:, None]
    nxt = jnp.min(jnp.where(later, blk_expert[None, :], N_EXPERTS), axis=1)
    nxt = jnp.where(nxt == N_EXPERTS, blk_expert, nxt)
    is_new = jnp.concatenate([jnp.zeros((1,), I32), (blk_expert[1:] != blk_expert[:-1]).astype(I32)])
    slot = jnp.cumsum(is_new) % 2
    grid_spec = pltpu.PrefetchScalarGridSpec(
        num_scalar_prefetch=4,
        grid=(nblk,),
        in_specs=[pl.BlockSpec((EXPERT_ROWS * SUBLANES, LANES),
                               lambda i, be, used, *_: (jnp.minimum(i, used[0] - 1), 0)),
                  pl.BlockSpec(memory_space=pl.ANY), pl.BlockSpec(memory_space=pl.ANY),
                  pl.BlockSpec(memory_space=pl.ANY)],
        out_specs=pl.BlockSpec((EXPERT_ROWS * SUBLANES, LANES), lambda i, *_: (i, 0)),
        scratch_shapes=[pltpu.VMEM((2, d, de), F32), pltpu.VMEM((2, d, de), F32),
                        pltpu.VMEM((2, de, d), F32),
                        pltpu.VMEM((d, de), BF16), pltpu.VMEM((d, de), BF16),
                        pltpu.VMEM((de, d), BF16), pltpu.SemaphoreType.DMA((3, 2))],
    )
    return pl.pallas_call(
        _expert_kernel,
        grid_spec=grid_spec,
        out_shape=jax.ShapeDtypeStruct((rows * SUBLANES, LANES), U32),
        compiler_params=_cparams(("arbitrary",)),
        name="experts",
    )(blk_expert, used, nxt.astype(I32), slot.astype(I32), xs, w_gate, w_up, w_down)


def _combine_kernel(pos_ref, pos_next_ref, gate_ref, x1_ref, ys_hbm, wsg_ref, wsu_ref,
                    wsd_ref, g_ref, b_ref, o_ref, buf0, buf1, sems):
    tt = x1_ref.shape[0]
    i = pl.program_id(0)
    even = i % 2 == 0

    def row_copy(positions, buf, s, t, k):
        dst = pl.multiple_of((k * tt + t) * SUBLANES, SUBLANES)
        return pltpu.make_async_copy(ys_hbm.at[positions[k, t]], buf.at[pl.ds(dst, SUBLANES)],
                                     sems.at[s])

    def gather(positions, buf, s, start):
        def step(t, c):
            for k in range(TOP_K):
                copy = row_copy(positions, buf, s, t, k)
                copy.start(priority=k % DMA_QUEUES) if start else copy.wait()
            return c
        lax.fori_loop(0, tt, step, 0)

    @pl.when(i == 0)
    def _():
        gather(pos_ref, buf0, 0, True)

    has_next = i + 1 < pl.num_programs(0)

    @pl.when(has_next & even)
    def _():
        gather(pos_next_ref, buf1, 1, True)

    @pl.when(has_next & jnp.logical_not(even))
    def _():
        gather(pos_next_ref, buf0, 0, True)

    x1 = x1_ref[...]
    xb = x1.astype(BF16)
    g = jnp.dot(xb, wsg_ref[...], preferred_element_type=F32)
    u = jnp.dot(xb, wsu_ref[...], preferred_element_type=F32)
    h = (g * jax.nn.sigmoid(g) * u).astype(BF16)
    y = DN_ALPHA * x1 + jnp.dot(h, wsd_ref[...], preferred_element_type=F32)

    gate = gate_ref[...]

    def finish(buf, s):
        gather(pos_ref, buf, s, False)
        r_lo = jnp.zeros((tt, TILE_WORDS), F32)
        r_hi = jnp.zeros((tt, TILE_WORDS), F32)
        for k in range(TOP_K):
            lo, hi = _unpack_bf16_pair(_load_rows_from_tiles(buf, k * tt, tt))
            r_lo = r_lo + lo * gate[:, k:k + 1]
            r_hi = r_hi + hi * gate[:, k:k + 1]
        out = y + jnp.concatenate([r_lo, r_hi], axis=1)
        o_ref[...] = _layer_norm(out, g_ref[...], b_ref[...])

    @pl.when(even)
    def _():
        finish(buf0, 0)

    @pl.when(jnp.logical_not(even))
    def _():
        finish(buf1, 1)


def _combine(x1, ys, pos_t, gate_nk, wsg_bf, wsu_bf, wsd_bf, ln_g, ln_b):
    n, d = x1.shape
    de = wsg_bf.shape[1]
    smem_tok = pl.BlockSpec((TOP_K, COMBINE_TT), lambda i, *_: (0, i), memory_space=pltpu.SMEM)
    steps = n // COMBINE_TT
    smem_next = pl.BlockSpec((TOP_K, COMBINE_TT), lambda i, *_: (0, jnp.minimum(i + 1, steps - 1)),
                             memory_space=pltpu.SMEM)
    const = lambda shape: pl.BlockSpec(shape, lambda i, *_: (0,) * len(shape))
    buf = pltpu.VMEM((TOP_K * COMBINE_TT * SUBLANES, LANES), U32)
    grid_spec = pltpu.PrefetchScalarGridSpec(
        num_scalar_prefetch=0,
        grid=(steps,),
        in_specs=[smem_tok, smem_next,
                  pl.BlockSpec((COMBINE_TT, TOP_K), lambda i, *_: (i, 0)),
                  pl.BlockSpec((COMBINE_TT, d), lambda i, *_: (i, 0)),
                  pl.BlockSpec(memory_space=pl.ANY),
                  const((d, de)), const((d, de)), const((de, d)), const((1, d)), const((1, d))],
        out_specs=pl.BlockSpec((COMBINE_TT, d), lambda i, *_: (i, 0)),
        scratch_shapes=[buf, buf, pltpu.SemaphoreType.DMA((2,))],
    )
    return pl.pallas_call(
        _combine_kernel,
        grid_spec=grid_spec,
        out_shape=jax.ShapeDtypeStruct((n, d), F32),
        compiler_params=_cparams(("arbitrary",)),
        name="combine_ln",
    )(pos_t, pos_t, gate_nk, x1, ys, wsg_bf, wsu_bf, wsd_bf, ln_g.reshape(1, d), ln_b.reshape(1, d))


def _block_layout(counts_f, n_assign):
    counts = counts_f.astype(I32)
    padded = (counts + EXPERT_ROWS - 1) // EXPERT_ROWS * EXPERT_ROWS
    end = jnp.cumsum(padded)
    total_rows = n_assign + N_EXPERTS * EXPERT_ROWS
    start = jnp.concatenate([end - padded, jnp.full((1,), total_rows, I32)])
    nblk = total_rows // EXPERT_ROWS
    used = end[-1:] // EXPERT_ROWS
    blk_start = jnp.minimum(jnp.arange(nblk, dtype=I32), used[0] - 1) * EXPERT_ROWS
    blk_expert = jnp.sum((end[None, :] <= blk_start[:, None]).astype(I32), axis=1)
    blk_expert = jnp.minimum(blk_expert, N_EXPERTS - 1)
    return start, counts, blk_expert, used.astype(I32), total_rows


def kernel(x, w_in, conv_w, conv_b, w_rg_a, b_rg_a, w_rg_x, b_rg_x, lru_lambda, w_out, ln1_g, ln1_b,
           w_router, router_bias, w_gate, w_up, w_down, ws_gate, ws_up, ws_down, ln2_g, ln2_b):
    b, s, d = x.shape
    n = b * s
    width = d // 2
    for l in range(DEPTH):
        x2 = x.reshape(n, d)
        proj = _proj(x2, w_in[l].astype(BF16), s, width)
        proj3 = proj.reshape(b, s, proj.shape[1])
        lru = _rglru(proj3, conv_w[l], conv_b[l], w_rg_a[l].astype(BF16), b_rg_a[l],
                     w_rg_x[l].astype(BF16), b_rg_x[l], lru_lambda[l], width)
        att = _attention(proj3, width)
        x1, x1p, idx_t, gate_t, rank_t, cnt = _mix_route(
            lru.reshape(n, width), att.reshape(n, width), x2, w_out[l].astype(BF16),
            ln1_g[l], ln1_b[l], w_router[l], router_bias[l])
        start, counts, blk_expert, used, total_rows = _block_layout(cnt[:, 0], n * TOP_K)
        onehot = idx_t[:, :, None] == jnp.arange(N_EXPERTS, dtype=I32)[None, None, :]
        pos_t = jnp.sum(jnp.where(onehot, start[None, None, :N_EXPERTS], 0), axis=2) + rank_t
        xs = _dispatch(x1p.reshape(n, SUBLANES, LANES), pos_t, start, counts, total_rows)
        ys = _experts(xs.reshape(total_rows * SUBLANES, LANES), blk_expert, used,
                      w_gate[l], w_up[l], w_down[l])
        out = _combine(x1, ys.reshape(total_rows, SUBLANES, LANES), pos_t, gate_t.T,
                       ws_gate[l].astype(BF16),
                       ws_up[l].astype(BF16), ws_down[l].astype(BF16), ln2_g[l], ln2_b[l])
        x = out.reshape(b, s, d)
    return x
```

```python
import functools

import jax
import jax.numpy as jnp
from jax import lax
from jax.experimental import pallas as pl
from jax.experimental.pallas import tpu as pltpu

F32 = jnp.float32
BF16 = jnp.bfloat16
I32 = jnp.int32
U32 = jnp.uint32

LRU_BLOCKS = 8
CONV_W = 4
LRU_C = 8.0
HEAD_DIM = 128
ROT_DIM = HEAD_DIM // 4
ROPE_THETA = 500000.0
DILATED_PATTERNS = ((128, 1), (512, 4), (2048, 16))
ATTN_BLOCK = 128
N_EXPERTS = 64
N_GROUPS = 8
GROUP_SIZE = N_EXPERTS // N_GROUPS
TOPK_GROUPS = 4
TOP_K = 8
ROUTED_SCALE = 2.5
DEPTH = 1
DN_ALPHA = (2 * DEPTH) ** 0.25
LN_EPS = 1e-5

PROJ_TM = 1024
LRU_TC = 512
MIX_TM = 512
EXPERT_ROWS = 512
DISPATCH_TT = 512
COMBINE_TT = 256
ZERO_ROWS = 64
DMA_QUEUES = 2
SUBLANES = 8
LANES = 128
TILE_WORDS = SUBLANES * LANES
VMEM_LIMIT = 56 * 1024 * 1024


def _pack_bf16_pair(lo, hi):
    lo_bits = lax.bitcast_convert_type(lo.astype(BF16).astype(F32), U32)
    hi_bits = lax.bitcast_convert_type(hi.astype(BF16).astype(F32), U32)
    return (lo_bits >> 16) | (hi_bits & jnp.uint32(0xFFFF0000))


def _unpack_bf16_pair(packed):
    lo = lax.bitcast_convert_type(packed << 16, F32)
    hi = lax.bitcast_convert_type(packed & jnp.uint32(0xFFFF0000), F32)
    return lo, hi


def _store_rows_as_tiles(ref, words):
    rows = words.shape[0]
    for s in range(SUBLANES):
        ref[pl.ds(s, rows, stride=SUBLANES), :] = words[:, s * LANES:(s + 1) * LANES]


def _load_rows_from_tiles(ref, row0, rows):
    return jnp.concatenate(
        [ref[pl.ds(row0 * SUBLANES + s, rows, stride=SUBLANES), :] for s in range(SUBLANES)], axis=1)


def _cparams(sem):
    return pltpu.CompilerParams(dimension_semantics=sem, vmem_limit_bytes=VMEM_LIMIT)


def _proj_kernel(x_ref, w_ref, c_ref, sa_ref, sb_ref, o_ref, *, n_heads):
    j = pl.program_id(0)
    acc = jnp.dot(x_ref[...].astype(BF16), w_ref[...], preferred_element_type=F32)
    is_rope = (j == 2) | (j == 3)

    @pl.when(is_rope)
    def _():
        scale = jnp.where(j == 2, HEAD_DIM ** -0.5, 1.0).astype(F32)
        c = c_ref[...] * scale
        sa = sa_ref[...] * scale
        sb = sb_ref[...] * scale
        half = ROT_DIM // 2
        for h in range(n_heads):
            t = acc[:, h * HEAD_DIM:(h + 1) * HEAD_DIM]
            o_ref[:, h * HEAD_DIM:(h + 1) * HEAD_DIM] = (
                t * c + pltpu.roll(t, half, 1) * sa + pltpu.roll(t, HEAD_DIM - half, 1) * sb)

    @pl.when(jnp.logical_not(is_rope))
    def _():
        o_ref[...] = acc


def _rope_tables(seq):
    half = ROT_DIM // 2
    pos = jnp.arange(seq, dtype=F32)
    inv = ROPE_THETA ** (-jnp.arange(0, ROT_DIM, 2, dtype=F32) / ROT_DIM)
    ang = pos[:, None] * inv[None, :]
    cos, sin = jnp.cos(ang), jnp.sin(ang)
    ones = jnp.ones((seq, HEAD_DIM - ROT_DIM), F32)
    zeros = jnp.zeros((seq, HEAD_DIM - ROT_DIM), F32)
    zh = jnp.zeros((seq, half), F32)
    c = jnp.concatenate([cos, cos, ones], -1)
    sa = jnp.concatenate([zh, sin, zeros], -1)
    sb = jnp.concatenate([-sin, zh, zeros], -1)
    return c, sa, sb


def _proj(x2, w_in_bf, seq, width):
    n, d = x2.shape
    e = w_in_bf.shape[1]
    n_groups = e // width
    c, sa, sb = _rope_tables(seq)
    tiles_per_seq = seq // PROJ_TM
    tab_spec = pl.BlockSpec((PROJ_TM, HEAD_DIM), lambda j, i: (i % tiles_per_seq, 0))
    return pl.pallas_call(
        functools.partial(_proj_kernel, n_heads=width // HEAD_DIM),
        grid=(n_groups, n // PROJ_TM),
        in_specs=[pl.BlockSpec((PROJ_TM, d), lambda j, i: (i, 0)),
                  pl.BlockSpec((d, width), lambda j, i: (0, j)),
                  tab_spec, tab_spec, tab_spec],
        out_specs=pl.BlockSpec((PROJ_TM, width), lambda j, i: (i, j)),
        out_shape=jax.ShapeDtypeStruct((n, e), F32),
        compiler_params=_cparams(("arbitrary", "arbitrary")),
        name="proj_rope",
    )(x2, w_in_bf, c, sa, sb)


def _lru_kernel(xl_ref, gl_ref, cw_ref, cb_ref, wa_ref, ba_ref, wx_ref, bx_ref, lam_ref,
                o_ref, xext_ref, hprev_ref, a_ref, b_ref, h_ref, *, block_w):
    tc, width = xl_ref.shape
    groups = tc // SUBLANES

    @pl.when(pl.program_id(1) == 0)
    def _():
        xext_ref[0:SUBLANES, :] = jnp.zeros((SUBLANES, width), F32)
        hprev_ref[...] = jnp.zeros_like(hprev_ref)

    x = xl_ref[...]
    xext_ref[SUBLANES:SUBLANES + tc, :] = x
    xc = x * cw_ref[CONV_W - 1:CONV_W, :] + cb_ref[...]
    for j in range(CONV_W - 1):
        back = CONV_W - 1 - j
        xc = xc + xext_ref[pl.ds(SUBLANES - back, tc), :] * cw_ref[j:j + 1, :]
    xext_ref[0:SUBLANES, :] = x[tc - SUBLANES:, :]

    lam = lam_ref[...]
    softplus_neg_lam = jnp.maximum(-lam, 0.0) + jnp.log1p(jnp.exp(-jnp.abs(lam)))
    rs, is_ = [], []
    for nb in range(width // block_w):
        xb = xc[:, nb * block_w:(nb + 1) * block_w].astype(BF16)
        rs.append(jnp.dot(xb, wa_ref[nb], preferred_element_type=F32))
        is_.append(jnp.dot(xb, wx_ref[nb], preferred_element_type=F32))
    r = jax.nn.sigmoid(jnp.concatenate(rs, axis=1) + ba_ref[...])
    ig = jax.nn.sigmoid(jnp.concatenate(is_, axis=1) + bx_ref[...])
    a = jnp.exp(-LRU_C * r * softplus_neg_lam)
    b = jnp.sqrt(1.0 - a * a) * (ig * xc)

    row = lax.broadcasted_iota(I32, (tc, width), 0) & (SUBLANES - 1)
    k = 1
    while k < SUBLANES:
        keep = row >= k
        a_sh = pltpu.roll(a, k, 0)
        b_sh = pltpu.roll(b, k, 0)
        b = jnp.where(keep, a * b_sh + b, b)
        a = jnp.where(keep, a * a_sh, a)
        k *= 2
    a_ref[...] = a.reshape(groups, SUBLANES, width)
    b_ref[...] = b.reshape(groups, SUBLANES, width)

    def carry(g, hprev):
        h = a_ref[g] * hprev + b_ref[g]
        h_ref[g] = h
        return h[SUBLANES - 1:SUBLANES, :]

    hprev_ref[...] = lax.fori_loop(0, groups, carry, hprev_ref[...])
    h = h_ref[...].reshape(tc, width)
    o_ref[...] = (h * jax.nn.gelu(gl_ref[...], approximate=True)).astype(o_ref.dtype)


def _rglru(proj3, conv_w, conv_b, wa_bf, b_a, wx_bf, b_x, lam, width):
    b, s, _ = proj3.shape
    block_w = width // LRU_BLOCKS
    vec = pl.BlockSpec((1, width), lambda bi, ci: (0, 0))
    wspec = pl.BlockSpec((LRU_BLOCKS, block_w, block_w), lambda bi, ci: (0, 0, 0))
    groups = LRU_TC // SUBLANES
    return pl.pallas_call(
        functools.partial(_lru_kernel, block_w=block_w),
        grid=(b, s // LRU_TC),
        in_specs=[pl.BlockSpec((None, LRU_TC, width), lambda bi, ci: (bi, ci, 0)),
                  pl.BlockSpec((None, LRU_TC, width), lambda bi, ci: (bi, ci, 1)),
                  pl.BlockSpec((CONV_W, width), lambda bi, ci: (0, 0)),
                  vec, wspec, vec, wspec, vec, vec],
        out_specs=pl.BlockSpec((None, LRU_TC, width), lambda bi, ci: (bi, ci, 0)),
        out_shape=jax.ShapeDtypeStruct((b, s, width), BF16),
        scratch_shapes=[pltpu.VMEM((LRU_TC + SUBLANES, width), F32),
                        pltpu.VMEM((1, width), F32),
                        pltpu.VMEM((groups, SUBLANES, width), F32),
                        pltpu.VMEM((groups, SUBLANES, width), F32),
                        pltpu.VMEM((groups, SUBLANES, width), F32)],
        compiler_params=_cparams(("arbitrary", "arbitrary")),
        name="rglru",
    )(proj3, proj3, conv_w, conv_b.reshape(1, width), wa_bf, b_a.reshape(1, width),
      wx_bf, b_x.reshape(1, width), lam.reshape(1, width))


def _attn_kernel(q_ref, k_ref, v_ref, o_ref, acc_n, m_n, l_n):
    seq = q_ref.shape[0]
    blk = ATTN_BLOCK
    n_pat = len(DILATED_PATTERNS)

    qi = lax.broadcasted_iota(I32, (blk, 2 * blk), 0)
    kj = lax.broadcasted_iota(I32, (blk, 2 * blk), 1)
    band2 = (kj >= qi) & (kj <= qi + blk)
    qi1 = lax.broadcasted_iota(I32, (blk, blk), 0)
    kj1 = lax.broadcasted_iota(I32, (blk, blk), 1)
    causal1 = kj1 <= qi1
    nt = (((1,), (1,)), ((), ()))

    ones = jnp.ones((2 * blk, HEAD_DIM), BF16)

    def rows(d, r, g0, nblocks):
        return pl.ds(r + d * blk * g0, nblocks * blk, stride=d) if d > 1 else pl.ds(blk * g0, nblocks * blk)

    for p, (window, d) in enumerate(DILATED_PATTERNS):
        assert window // d == blk
        for r in range(d):
            for g in range((seq // d) // blk):
                q = q_ref[rows(d, r, g, 1), :].astype(BF16)
                if g == 0:
                    kk = k_ref[rows(d, r, 0, 1), :].astype(BF16)
                    vv = v_ref[rows(d, r, 0, 1), :].astype(BF16)
                    mask, nk = causal1, blk
                else:
                    kk = k_ref[rows(d, r, g - 1, 2), :].astype(BF16)
                    vv = v_ref[rows(d, r, g - 1, 2), :].astype(BF16)
                    mask, nk = band2, 2 * blk
                s = lax.dot_general(q, kk, nt, preferred_element_type=F32)
                s = jnp.where(mask, s, -jnp.inf)
                m = jnp.max(s, axis=1, keepdims=True)
                e = jnp.exp(s - m)
                v_ones = jnp.concatenate([vv, ones[0:nk]], axis=1)
                acc_l = jnp.dot(e.astype(BF16), v_ones, preferred_element_type=F32)
                dst = rows(d, r, g, 1)
                acc_n[p, dst, :] = acc_l[:, 0:HEAD_DIM]
                m_n[p, dst, :] = jnp.broadcast_to(m, (blk, HEAD_DIM))
                l_n[p, dst, :] = acc_l[:, HEAD_DIM:]

    mx = m_n[0]
    for p in range(1, n_pat):
        mx = jnp.maximum(mx, m_n[p])
    num = jnp.zeros((seq, HEAD_DIM), F32)
    den = jnp.zeros((seq, HEAD_DIM), F32)
    for p in range(n_pat):
        w = jnp.exp(m_n[p] - mx)
        num = num + w * acc_n[p]
        den = den + w * l_n[p]
    o_ref[...] = (num / den).astype(o_ref.dtype)


def _attention(proj3, width):
    b, s, _ = proj3.shape
    n_heads = width // HEAD_DIM
    n_pat = len(DILATED_PATTERNS)
    col0 = 2 * width // HEAD_DIM

    def spec(group):
        return pl.BlockSpec((None, s, HEAD_DIM),
                            lambda bi, hi: (bi, 0, col0 + group * n_heads + hi))

    return pl.pallas_call(
        _attn_kernel,
        grid=(b, n_heads),
        in_specs=[spec(0), spec(1), spec(2)],
        out_specs=pl.BlockSpec((None, s, HEAD_DIM), lambda bi, hi: (bi, 0, hi)),
        out_shape=jax.ShapeDtypeStruct((b, s, width), BF16),
        scratch_shapes=[pltpu.VMEM((n_pat, s, HEAD_DIM), F32)] * 3,
        compiler_params=_cparams(("arbitrary", "arbitrary")),
        name="dilated_attention",
    )(proj3, proj3, proj3)


def _layer_norm(y, g, b):
    mu = jnp.mean(y, axis=-1, keepdims=True)
    yc = y - mu
    var = jnp.mean(yc * yc, axis=-1, keepdims=True)
    return yc * lax.rsqrt(var + LN_EPS) * g + b


def _first_index(hit, iota, size):
    return jnp.min(jnp.where(hit, iota, size), axis=0, keepdims=True)


def _mix_kernel(lru_ref, att_ref, x_ref, wo_ref, g_ref, b_ref, wrh_ref, wrl_ref, rb_ref, tri_ref,
                x1_ref, x1p_ref, idx_ref, gate_ref, rank_ref, cnt_ref, base_ref):
    tm = x_ref.shape[0]
    half = lru_ref.shape[1]

    @pl.when(pl.program_id(0) == 0)
    def _():
        base_ref[...] = jnp.zeros_like(base_ref)

    mix = jnp.dot(lru_ref[...], wo_ref[0:half, :], preferred_element_type=F32)
    mix = mix + jnp.dot(att_ref[...], wo_ref[half:, :], preferred_element_type=F32)
    x1 = _layer_norm(DN_ALPHA * x_ref[...] + mix, g_ref[...], b_ref[...])
    x1_ref[...] = x1
    _store_rows_as_tiles(x1p_ref, _pack_bf16_pair(x1[:, 0:half], x1[:, half:]))

    x_hi = x1.astype(BF16)
    x_lo = (x1 - x_hi.astype(F32)).astype(BF16)
    nt = (((1,), (1,)), ((), ()))
    logits = lax.dot_general(wrh_ref[...], x_hi, nt, preferred_element_type=F32)
    logits = logits + lax.dot_general(wrh_ref[...], x_lo, nt, preferred_element_type=F32)
    logits = logits + lax.dot_general(wrl_ref[...], x_hi, nt, preferred_element_type=F32)
    scores = jax.nn.sigmoid(logits)
    sel = scores + rb_ref[...]

    neg = -jnp.inf
    gi = lax.broadcasted_iota(I32, (GROUP_SIZE, tm), 0)
    grp_rows = []
    for g in range(N_GROUPS):
        blk = sel[g * GROUP_SIZE:(g + 1) * GROUP_SIZE, :]
        m1 = jnp.max(blk, axis=0, keepdims=True)
        f1 = _first_index(blk == m1, gi, GROUP_SIZE)
        m2 = jnp.max(jnp.where(gi == f1, neg, blk), axis=0, keepdims=True)
        grp_rows.append(m1 + m2)
    ni = lax.broadcasted_iota(I32, (N_GROUPS, tm), 0)
    gscore = jnp.zeros((N_GROUPS, tm), F32)
    for g in range(N_GROUPS):
        gscore = jnp.where(ni == g, grp_rows[g], gscore)
    gmask = jnp.zeros((N_GROUPS, tm), jnp.bool_)
    for _ in range(TOPK_GROUPS):
        m = jnp.max(gscore, axis=0, keepdims=True)
        f = _first_index(gscore == m, ni, N_GROUPS)
        hit = ni == f
        gmask = gmask | hit
        gscore = jnp.where(hit, neg, gscore)
    emask = jnp.concatenate(
        [jnp.broadcast_to(gmask[g:g + 1, :], (GROUP_SIZE, tm)) for g in range(N_GROUPS)], axis=0)
    masked = jnp.where(emask, sel, neg)

    ei = lax.broadcasted_iota(I32, (N_EXPERTS, tm), 0)
    chosen = jnp.zeros((N_EXPERTS, tm), jnp.bool_)
    picks, gates = [], []
    for _ in range(TOP_K):
        m = jnp.max(masked, axis=0, keepdims=True)
        f = _first_index(masked == m, ei, N_EXPERTS)
        hit = ei == f
        picks.append(f)
        gates.append(jnp.sum(jnp.where(hit, scores, 0.0), axis=0, keepdims=True))
        chosen = chosen | hit
        masked = jnp.where(hit, neg, masked)
    gsum = gates[0]
    for k in range(1, TOP_K):
        gsum = gsum + gates[k]

    chosen_f = jnp.where(chosen, 1.0, 0.0)
    incl = jnp.dot(chosen_f.astype(BF16), tri_ref[...], preferred_element_type=F32)
    rank_excl = base_ref[:, 0:1] + incl - chosen_f
    base_ref[...] = base_ref[...] + jnp.sum(chosen_f, axis=1, keepdims=True)
    cnt_ref[...] = base_ref[...]
    for k in range(TOP_K):
        idx_ref[k:k + 1, :] = picks[k]
        gate_ref[k:k + 1, :] = gates[k] / gsum * ROUTED_SCALE
        rk = jnp.sum(jnp.where(ei == picks[k], rank_excl, 0.0), axis=0, keepdims=True)
        rank_ref[k:k + 1, :] = rk.astype(I32)


def _mix_route(lru, att, x2, wo_bf, ln_g, ln_b, w_router, router_bias):
    n, d = x2.shape
    half = lru.shape[1]
    wr_t = w_router.T
    wr_hi = wr_t.astype(BF16)
    wr_lo = (wr_t - wr_hi.astype(F32)).astype(BF16)
    assert half == TILE_WORDS
    tri = jnp.triu(jnp.ones((MIX_TM, MIX_TM), BF16))
    const = lambda shape: pl.BlockSpec(shape, lambda i: (0,) * len(shape))
    tok = lambda dtype: jax.ShapeDtypeStruct((TOP_K, n), dtype)
    tok_spec = pl.BlockSpec((TOP_K, MIX_TM), lambda i: (0, i))
    return pl.pallas_call(
        _mix_kernel,
        grid=(n // MIX_TM,),
        in_specs=[pl.BlockSpec((MIX_TM, half), lambda i: (i, 0)),
                  pl.BlockSpec((MIX_TM, half), lambda i: (i, 0)),
                  pl.BlockSpec((MIX_TM, d), lambda i: (i, 0)),
                  const((d, d)), const((1, d)), const((1, d)),
                  const((N_EXPERTS, d)), const((N_EXPERTS, d)), const((N_EXPERTS, 1)),
                  const((MIX_TM, MIX_TM))],
        out_specs=[pl.BlockSpec((MIX_TM, d), lambda i: (i, 0)),
                   pl.BlockSpec((MIX_TM * SUBLANES, LANES), lambda i: (i, 0)),
                   tok_spec, tok_spec, tok_spec, const((N_EXPERTS, HEAD_DIM))],
        out_shape=[jax.ShapeDtypeStruct((n, d), F32), jax.ShapeDtypeStruct((n * SUBLANES, LANES), U32),
                   tok(I32), tok(F32), tok(I32),
                   jax.ShapeDtypeStruct((N_EXPERTS, HEAD_DIM), F32)],
        scratch_shapes=[pltpu.VMEM((N_EXPERTS, HEAD_DIM), F32)],
        compiler_params=_cparams(("arbitrary",)),
        name="mix_ln_route",
    )(lru, att, x2, wo_bf, ln_g.reshape(1, d), ln_b.reshape(1, d), wr_hi, wr_lo,
      router_bias.reshape(N_EXPERTS, 1), tri)


def _dispatch_kernel(start_ref, cnt_ref, pos_ref, x1_ref, zero_ref, xs_hbm, sem):
    i = pl.program_id(0)
    tt = pos_ref.shape[1]

    def row_copy(t, k):
        return pltpu.make_async_copy(x1_ref.at[t], xs_hbm.at[pos_ref[k, t]], sem)

    def issue(t, c):
        for k in range(TOP_K):
            row_copy(t, k).start(priority=k % DMA_QUEUES)
        return c

    def drain(t, c):
        for k in range(TOP_K):
            row_copy(t, k).wait()
        return c

    lax.fori_loop(0, tt, issue, 0)
    lax.fori_loop(0, tt, drain, 0)

    @pl.when(i == 0)
    def _():
        def pad_copy(r, rows):
            return pltpu.make_async_copy(zero_ref.at[pl.ds(0, rows)], xs_hbm.at[pl.ds(r, rows)], sem)

        def per_expert(e, c):
            lo = start_ref[e] + cnt_ref[e]
            hi = start_ref[e + 1]
            chunks = (hi - lo) // ZERO_ROWS
            mid = lo + chunks * ZERO_ROWS
            big = lambda j: pad_copy(lo + j * ZERO_ROWS, ZERO_ROWS)
            lax.fori_loop(0, chunks, lambda j, cc: (big(j).start(), cc)[1], 0)
            lax.fori_loop(mid, hi, lambda r, cc: (pad_copy(r, 1).start(), cc)[1], 0)
            lax.fori_loop(0, chunks, lambda j, cc: (big(j).wait(), cc)[1], 0)
            lax.fori_loop(mid, hi, lambda r, cc: (pad_copy(r, 1).wait(), cc)[1], 0)
            return c

        lax.fori_loop(0, N_EXPERTS, per_expert, 0)


def _dispatch(x1p, pos_t, start, counts, total_rows):
    n = x1p.shape[0]
    smem_tok = pl.BlockSpec((TOP_K, DISPATCH_TT), lambda i, *_: (0, i), memory_space=pltpu.SMEM)
    grid_spec = pltpu.PrefetchScalarGridSpec(
        num_scalar_prefetch=2,
        grid=(n // DISPATCH_TT,),
        in_specs=[smem_tok,
                  pl.BlockSpec((DISPATCH_TT, SUBLANES, LANES), lambda i, *_: (i, 0, 0)),
                  pl.BlockSpec((ZERO_ROWS, SUBLANES, LANES), lambda i, *_: (0, 0, 0))],
        out_specs=pl.BlockSpec(memory_space=pl.ANY),
        scratch_shapes=[pltpu.SemaphoreType.DMA(())],
    )
    return pl.pallas_call(
        _dispatch_kernel,
        grid_spec=grid_spec,
        out_shape=jax.ShapeDtypeStruct((total_rows, SUBLANES, LANES), x1p.dtype),
        compiler_params=_cparams(("arbitrary",)),
        name="dispatch",
    )(start, counts, pos_t, x1p, jnp.zeros((ZERO_ROWS, SUBLANES, LANES), x1p.dtype))


def _expert_kernel(be_ref, used_ref, nxt_ref, slot_ref, xs_ref, wg_hbm, wu_hbm, wd_hbm, ys_ref,
                   wg_f32, wu_f32, wd_f32, wg_bf, wu_bf, wd_bf, sems):
    i = pl.program_id(0)
    e = be_ref[i]
    slot = slot_ref[i]
    changed = (i == 0) | (e != be_ref[jnp.maximum(i - 1, 0)])

    def weight_copies(expert, s):
        return (pltpu.make_async_copy(wg_hbm.at[expert], wg_f32.at[s], sems.at[0, s]),
                pltpu.make_async_copy(wu_hbm.at[expert], wu_f32.at[s], sems.at[1, s]),
                pltpu.make_async_copy(wd_hbm.at[expert], wd_f32.at[s], sems.at[2, s]))

    @pl.when(i == 0)
    def _():
        for c in weight_copies(e, slot):
            c.start(priority=1)

    @pl.when(changed)
    def _():
        for c in weight_copies(e, slot):
            c.wait()
        nxt = nxt_ref[i]

        @pl.when(nxt != e)
        def _():
            for c in weight_copies(nxt, 1 - slot):
                c.start(priority=1)

        wg_bf[...] = wg_f32[slot].astype(BF16)
        wu_bf[...] = wu_f32[slot].astype(BF16)
        wd_bf[...] = wd_f32[slot].astype(BF16)

    @pl.when(i < used_ref[0])
    def _():
        half = TILE_WORDS
        lo, hi = _unpack_bf16_pair(_load_rows_from_tiles(xs_ref, 0, EXPERT_ROWS))
        lo, hi = lo.astype(BF16), hi.astype(BF16)
        g = (jnp.dot(lo, wg_bf[0:half, :], preferred_element_type=F32)
             + jnp.dot(hi, wg_bf[half:, :], preferred_element_type=F32))
        u = (jnp.dot(lo, wu_bf[0:half, :], preferred_element_type=F32)
             + jnp.dot(hi, wu_bf[half:, :], preferred_element_type=F32))
        h = (g * jax.nn.sigmoid(g) * u).astype(BF16)
        y = jnp.dot(h, wd_bf[...], preferred_element_type=F32)
        _store_rows_as_tiles(ys_ref, _pack_bf16_pair(y[:, 0:half], y[:, half:]))

    @pl.when(i >= used_ref[0])
    def _():
        ys_ref[...] = jnp.zeros_like(ys_ref)


def _experts(xs, blk_expert, used, w_gate, w_up, w_down):
    rows = xs.shape[0] // SUBLANES
    d = 2 * TILE_WORDS
    de = w_gate.shape[2]
    nblk = rows // EXPERT_ROWS
    later = blk_expert[None, :] > blk_expert[:, None]
    nxt = jnp.min(jnp.where(later, blk_expert[None, :], N_EXPERTS), axis=1)
    nxt = jnp.where(nxt == N_EXPERTS, blk_expert, nxt)
    is_new = jnp.concatenate([jnp.zeros((1,), I32), (blk_expert[1:] != blk_expert[:-1]).astype(I32)])
    slot = jnp.cumsum(is_new) % 2
    grid_spec = pltpu.PrefetchScalarGridSpec(
        num_scalar_prefetch=4,
        grid=(nblk,),
        in_specs=[pl.BlockSpec((EXPERT_ROWS * SUBLANES, LANES),
                               lambda i, be, used, *_: (jnp.minimum(i, used[0] - 1), 0)),
                  pl.BlockSpec(memory_space=pl.ANY), pl.BlockSpec(memory_space=pl.ANY),
                  pl.BlockSpec(memory_space=pl.ANY)],
        out_specs=pl.BlockSpec((EXPERT_ROWS * SUBLANES, LANES), lambda i, *_: (i, 0)),
        scratch_shapes=[pltpu.VMEM((2, d, de), F32), pltpu.VMEM((2, d, de), F32),
                        pltpu.VMEM((2, de, d), F32),
                        pltpu.VMEM((d, de), BF16), pltpu.VMEM((d, de), BF16),
                        pltpu.VMEM((de, d), BF16), pltpu.SemaphoreType.DMA((3, 2))],
    )
    return pl.pallas_call(
        _expert_kernel,
        grid_spec=grid_spec,
        out_shape=jax.ShapeDtypeStruct((rows * SUBLANES, LANES), U32),
        compiler_params=_cparams(("arbitrary",)),
        name="experts",
    )(blk_expert, used, nxt.astype(I32), slot.astype(I32), xs, w_gate, w_up, w_down)


def _combine_kernel(pos_ref, pos_next_ref, gate_ref, x1_ref, ys_hbm, wsg_ref, wsu_ref,
                    wsd_ref, g_ref, b_ref, o_ref, buf0, buf1, sems):
    tt = x1_ref.shape[0]
    i = pl.program_id(0)
    even = i % 2 == 0

    def row_copy(positions, buf, s, t, k):
        dst = pl.multiple_of((k * tt + t) * SUBLANES, SUBLANES)
        return pltpu.make_async_copy(ys_hbm.at[positions[k, t]], buf.at[pl.ds(dst, SUBLANES)],
                                     sems.at[s])

    def gather(positions, buf, s, start):
        def step(t, c):
            for k in range(TOP_K):
                copy = row_copy(positions, buf, s, t, k)
                copy.start(priority=k % DMA_QUEUES) if start else copy.wait()
            return c
        lax.fori_loop(0, tt, step, 0)

    @pl.when(i == 0)
    def _():
        gather(pos_ref, buf0, 0, True)

    has_next = i + 1 < pl.num_programs(0)

    @pl.when(has_next & even)
    def _():
        gather(pos_next_ref, buf1, 1, True)

    @pl.when(has_next & jnp.logical_not(even))
    def _():
        gather(pos_next_ref, buf0, 0, True)

    x1 = x1_ref[...]
    xb = x1.astype(BF16)
    g = jnp.dot(xb, wsg_ref[...], preferred_element_type=F32)
    u = jnp.dot(xb, wsu_ref[...], preferred_element_type=F32)
    h = (g * jax.nn.sigmoid(g) * u).astype(BF16)
    y = DN_ALPHA * x1 + jnp.dot(h, wsd_ref[...], preferred_element_type=F32)

    gate = gate_ref[...]

    def finish(buf, s):
        gather(pos_ref, buf, s, False)
        r_lo = jnp.zeros((tt, TILE_WORDS), F32)
        r_hi = jnp.zeros((tt, TILE_WORDS), F32)
        for k in range(TOP_K):
            lo, hi = _unpack_bf16_pair(_load_rows_from_tiles(buf, k * tt, tt))
            r_lo = r_lo + lo * gate[:, k:k + 1]
            r_hi = r_hi + hi * gate[:, k:k + 1]
        out = y + jnp.concatenate([r_lo, r_hi], axis=1)
        o_ref[...] = _layer_norm(out, g_ref[...], b_ref[...])

    @pl.when(even)
    def _():
        finish(buf0, 0)

    @pl.when(jnp.logical_not(even))
    def _():
        finish(buf1, 1)


def _combine(x1, ys, pos_t, gate_nk, wsg_bf, wsu_bf, wsd_bf, ln_g, ln_b):
    n, d = x1.shape
    de = wsg_bf.shape[1]
    smem_tok = pl.BlockSpec((TOP_K, COMBINE_TT), lambda i, *_: (0, i), memory_space=pltpu.SMEM)
    steps = n // COMBINE_TT
    smem_next = pl.BlockSpec((TOP_K, COMBINE_TT), lambda i, *_: (0, jnp.minimum(i + 1, steps - 1)),
                             memory_space=pltpu.SMEM)
    const = lambda shape: pl.BlockSpec(shape, lambda i, *_: (0,) * len(shape))
    buf = pltpu.VMEM((TOP_K * COMBINE_TT * SUBLANES, LANES), U32)
    grid_spec = pltpu.PrefetchScalarGridSpec(
        num_scalar_prefetch=0,
        grid=(steps,),
        in_specs=[smem_tok, smem_next,
                  pl.BlockSpec((COMBINE_TT, TOP_K), lambda i, *_: (i, 0)),
                  pl.BlockSpec((COMBINE_TT, d), lambda i, *_: (i, 0)),
                  pl.BlockSpec(memory_space=pl.ANY),
                  const((d, de)), const((d, de)), const((de, d)), const((1, d)), const((1, d))],
        out_specs=pl.BlockSpec((COMBINE_TT, d), lambda i, *_: (i, 0)),
        scratch_shapes=[buf, buf, pltpu.SemaphoreType.DMA((2,))],
    )
    return pl.pallas_call(
        _combine_kernel,
        grid_spec=grid_spec,
        out_shape=jax.ShapeDtypeStruct((n, d), F32),
        compiler_params=_cparams(("arbitrary",)),
        name="combine_ln",
    )(pos_t, pos_t, gate_nk, x1, ys, wsg_bf, wsu_bf, wsd_bf, ln_g.reshape(1, d), ln_b.reshape(1, d))


def _block_layout(counts_f, n_assign):
    counts = counts_f.astype(I32)
    padded = (counts + EXPERT_ROWS - 1) // EXPERT_ROWS * EXPERT_ROWS
    end = jnp.cumsum(padded)
    total_rows = n_assign + N_EXPERTS * EXPERT_ROWS
    start = jnp.concatenate([end - padded, jnp.full((1,), total_rows, I32)])
    nblk = total_rows // EXPERT_ROWS
    used = end[-1:] // EXPERT_ROWS
    blk_start = jnp.minimum(jnp.arange(nblk, dtype=I32), used[0] - 1) * EXPERT_ROWS
    blk_expert = jnp.sum((end[None, :] <= blk_start[:, None]).astype(I32), axis=1)
    blk_expert = jnp.minimum(blk_expert, N_EXPERTS - 1)
    return start, counts, blk_expert, used.astype(I32), total_rows


def kernel(x, w_in, conv_w, conv_b, w_rg_a, b_rg_a, w_rg_x, b_rg_x, lru_lambda, w_out, ln1_g, ln1_b,
           w_router, router_bias, w_gate, w_up, w_down, ws_gate, ws_up, ws_down, ln2_g, ln2_b):
    b, s, d = x.shape
    n = b * s
    width = d // 2
    for l in range(DEPTH):
        x2 = x.reshape(n, d)
        proj = _proj(x2, w_in[l].astype(BF16), s, width)
        proj3 = proj.reshape(b, s, proj.shape[1])
        lru = _rglru(proj3, conv_w[l], conv_b[l], w_rg_a[l].astype(BF16), b_rg_a[l],
                     w_rg_x[l].astype(BF16), b_rg_x[l], lru_lambda[l], width)
        att = _attention(proj3, width)
        x1, x1p, idx_t, gate_t, rank_t, cnt = _mix_route(
            lru.reshape(n, width), att.reshape(n, width), x2, w_out[l].astype(BF16),
            ln1_g[l], ln1_b[l], w_router[l], router_bias[l])
        start, counts, blk_expert, used, total_rows = _block_layout(cnt[:, 0], n * TOP_K)
        onehot = idx_t[:, :, None] == jnp.arange(N_EXPERTS, dtype=I32)[None, None, :]
        pos_t = jnp.sum(jnp.where(onehot, start[None, None, :N_EXPERTS], 0), axis=2) + rank_t
        xs = _dispatch(x1p.reshape(n, SUBLANES, LANES), pos_t, start, counts, total_rows)
        ys = _experts(xs.reshape(total_rows * SUBLANES, LANES), blk_expert, used,
                      w_gate[l], w_up[l], w_down[l])
        out = _combine(x1, ys.reshape(total_rows, SUBLANES, LANES), pos_t, gate_t.T,
                       ws_gate[l].astype(BF16),
                       ws_up[l].astype(BF16), ws_down[l].astype(BF16), ln2_g[l], ln2_b[l])
        x = out.reshape(b, s, d)
    return x
```

```python
import functools

import jax
import jax.numpy as jnp
from jax import lax
from jax.experimental import pallas as pl
from jax.experimental.pallas import tpu as pltpu

F32 = jnp.float32
BF16 = jnp.bfloat16
I32 = jnp.int32
U32 = jnp.uint32

LRU_BLOCKS = 8
CONV_W = 4
LRU_C = 8.0
HEAD_DIM = 128
ROT_DIM = HEAD_DIM // 4
ROPE_THETA = 500000.0
DILATED_PATTERNS = ((128, 1), (512, 4), (2048, 16))
ATTN_BLOCK = 128
N_EXPERTS = 64
N_GROUPS = 8
GROUP_SIZE = N_EXPERTS // N_GROUPS
TOPK_GROUPS = 4
TOP_K = 8
ROUTED_SCALE = 2.5
DEPTH = 1
DN_ALPHA = (2 * DEPTH) ** 0.25
LN_EPS = 1e-5

PROJ_TM = 1024
LRU_TC = 512
MIX_TM = 512
EXPERT_ROWS = 512
DISPATCH_TT = 512
COMBINE_TT = 256
ZERO_ROWS = 64
DMA_QUEUES = 2
SUBLANES = 8
LANES = 128
TILE_WORDS = SUBLANES * LANES
VMEM_LIMIT = 56 * 1024 * 1024


def _pack_bf16_pair(lo, hi):
    lo_bits = lax.bitcast_convert_type(lo.astype(BF16).astype(F32), U32)
    hi_bits = lax.bitcast_convert_type(hi.astype(BF16).astype(F32), U32)
    return (lo_bits >> 16) | (hi_bits & jnp.uint32(0xFFFF0000))


def _unpack_bf16_pair(packed):
    lo = lax.bitcast_convert_type(packed << 16, F32)
    hi = lax.bitcast_convert_type(packed & jnp.uint32(0xFFFF0000), F32)
    return lo, hi


def _store_rows_as_tiles(ref, words):
    rows = words.shape[0]
    for s in range(SUBLANES):
        ref[pl.ds(s, rows, stride=SUBLANES), :] = words[:, s * LANES:(s + 1) * LANES]


def _load_rows_from_tiles(ref, row0, rows):
    return jnp.concatenate(
        [ref[pl.ds(row0 * SUBLANES + s, rows, stride=SUBLANES), :] for s in range(SUBLANES)], axis=1)


def _cparams(sem):
    return pltpu.CompilerParams(dimension_semantics=sem, vmem_limit_bytes=VMEM_LIMIT)


def _proj_kernel(x_ref, w_ref, c_ref, sa_ref, sb_ref, o_ref, wb_ref, *, n_heads):
    j = pl.program_id(0)

    @pl.when(pl.program_id(1) == 0)
    def _():
        wb_ref[...] = w_ref[...].astype(BF16)

    acc = jnp.dot(x_ref[...].astype(BF16), wb_ref[...], preferred_element_type=F32)
    is_rope = (j == 2) | (j == 3)

    @pl.when(is_rope)
    def _():
        scale = jnp.where(j == 2, HEAD_DIM ** -0.5, 1.0).astype(F32)
        c = c_ref[...] * scale
        sa = sa_ref[...] * scale
        sb = sb_ref[...] * scale
        half = ROT_DIM // 2
        for h in range(n_heads):
            t = acc[:, h * HEAD_DIM:(h + 1) * HEAD_DIM]
            o_ref[:, h * HEAD_DIM:(h + 1) * HEAD_DIM] = (
                t * c + pltpu.roll(t, half, 1) * sa + pltpu.roll(t, HEAD_DIM - half, 1) * sb)

    @pl.when(jnp.logical_not(is_rope))
    def _():
        o_ref[...] = acc


def _rope_tables(seq):
    half = ROT_DIM // 2
    pos = jnp.arange(seq, dtype=F32)
    inv = ROPE_THETA ** (-jnp.arange(0, ROT_DIM, 2, dtype=F32) / ROT_DIM)
    ang = pos[:, None] * inv[None, :]
    cos, sin = jnp.cos(ang), jnp.sin(ang)
    ones = jnp.ones((seq, HEAD_DIM - ROT_DIM), F32)
    zeros = jnp.zeros((seq, HEAD_DIM - ROT_DIM), F32)
    zh = jnp.zeros((seq, half), F32)
    c = jnp.concatenate([cos, cos, ones], -1)
    sa = jnp.concatenate([zh, sin, zeros], -1)
    sb = jnp.concatenate([-sin, zh, zeros], -1)
    return c, sa, sb


def _proj(x2, w_in_bf, seq, width):
    n, d = x2.shape
    e = w_in_bf.shape[1]
    n_groups = e // width
    c, sa, sb = _rope_tables(seq)
    tiles_per_seq = seq // PROJ_TM
    tab_spec = pl.BlockSpec((PROJ_TM, HEAD_DIM), lambda j, i: (i % tiles_per_seq, 0))
    return pl.pallas_call(
        functools.partial(_proj_kernel, n_heads=width // HEAD_DIM),
        grid=(n_groups, n // PROJ_TM),
        in_specs=[pl.BlockSpec((PROJ_TM, d), lambda j, i: (i, 0)),
                  pl.BlockSpec((d, width), lambda j, i: (0, j)),
                  tab_spec, tab_spec, tab_spec],
        out_specs=pl.BlockSpec((PROJ_TM, width), lambda j, i: (i, j)),
        out_shape=jax.ShapeDtypeStruct((n, e), F32),
        scratch_shapes=[pltpu.VMEM((d, width), BF16)],
        compiler_params=_cparams(("arbitrary", "arbitrary")),
        name="proj_rope",
    )(x2, w_in_bf, c, sa, sb)


def _lru_kernel(xl_ref, gl_ref, cw_ref, cb_ref, wa_ref, ba_ref, wx_ref, bx_ref, lam_ref,
                o_ref, xext_ref, hprev_ref, a_ref, b_ref, h_ref, *, block_w):
    tc, width = xl_ref.shape
    groups = tc // SUBLANES

    @pl.when(pl.program_id(1) == 0)
    def _():
        xext_ref[0:SUBLANES, :] = jnp.zeros((SUBLANES, width), F32)
        hprev_ref[...] = jnp.zeros_like(hprev_ref)

    x = xl_ref[...]
    xext_ref[SUBLANES:SUBLANES + tc, :] = x
    xc = x * cw_ref[CONV_W - 1:CONV_W, :] + cb_ref[...]
    for j in range(CONV_W - 1):
        back = CONV_W - 1 - j
        xc = xc + xext_ref[pl.ds(SUBLANES - back, tc), :] * cw_ref[j:j + 1, :]
    xext_ref[0:SUBLANES, :] = x[tc - SUBLANES:, :]

    lam = lam_ref[...]
    softplus_neg_lam = jnp.maximum(-lam, 0.0) + jnp.log1p(jnp.exp(-jnp.abs(lam)))
    rs, is_ = [], []
    for nb in range(width // block_w):
        xb = xc[:, nb * block_w:(nb + 1) * block_w].astype(BF16)
        rs.append(jnp.dot(xb, wa_ref[nb], preferred_element_type=F32))
        is_.append(jnp.dot(xb, wx_ref[nb], preferred_element_type=F32))
    r = jax.nn.sigmoid(jnp.concatenate(rs, axis=1) + ba_ref[...])
    ig = jax.nn.sigmoid(jnp.concatenate(is_, axis=1) + bx_ref[...])
    a = jnp.exp(-LRU_C * r * softplus_neg_lam)
    b = jnp.sqrt(1.0 - a * a) * (ig * xc)

    row = lax.broadcasted_iota(I32, (tc, width), 0) & (SUBLANES - 1)
    k = 1
    while k < SUBLANES:
        keep = row >= k
        a_sh = pltpu.roll(a, k, 0)
        b_sh = pltpu.roll(b, k, 0)
        b = jnp.where(keep, a * b_sh + b, b)
        a = jnp.where(keep, a * a_sh, a)
        k *= 2
    a_ref[...] = a.reshape(groups, SUBLANES, width)
    b_ref[...] = b.reshape(groups, SUBLANES, width)

    def carry(g, hprev):
        h = a_ref[g] * hprev + b_ref[g]
        h_ref[g] = h
        return h[SUBLANES - 1:SUBLANES, :]

    hprev_ref[...] = lax.fori_loop(0, groups, carry, hprev_ref[...])
    h = h_ref[...].reshape(tc, width)
    o_ref[...] = (h * jax.nn.gelu(gl_ref[...], approximate=True)).astype(o_ref.dtype)


def _rglru(proj3, conv_w, conv_b, wa_bf, b_a, wx_bf, b_x, lam, width):
    b, s, _ = proj3.shape
    block_w = width // LRU_BLOCKS
    vec = pl.BlockSpec((1, width), lambda bi, ci: (0, 0))
    wspec = pl.BlockSpec((LRU_BLOCKS, block_w, block_w), lambda bi, ci: (0, 0, 0))
    groups = LRU_TC // SUBLANES
    return pl.pallas_call(
        functools.partial(_lru_kernel, block_w=block_w),
        grid=(b, s // LRU_TC),
        in_specs=[pl.BlockSpec((None, LRU_TC, width), lambda bi, ci: (bi, ci, 0)),
                  pl.BlockSpec((None, LRU_TC, width), lambda bi, ci: (bi, ci, 1)),
                  pl.BlockSpec((CONV_W, width), lambda bi, ci: (0, 0)),
                  vec, wspec, vec, wspec, vec, vec],
        out_specs=pl.BlockSpec((None, LRU_TC, width), lambda bi, ci: (bi, ci, 0)),
        out_shape=jax.ShapeDtypeStruct((b, s, width), BF16),
        scratch_shapes=[pltpu.VMEM((LRU_TC + SUBLANES, width), F32),
                        pltpu.VMEM((1, width), F32),
                        pltpu.VMEM((groups, SUBLANES, width), F32),
                        pltpu.VMEM((groups, SUBLANES, width), F32),
                        pltpu.VMEM((groups, SUBLANES, width), F32)],
        compiler_params=_cparams(("arbitrary", "arbitrary")),
        name="rglru",
    )(proj3, proj3, conv_w, conv_b.reshape(1, width), wa_bf, b_a.reshape(1, width),
      wx_bf, b_x.reshape(1, width), lam.reshape(1, width))


def _attn_kernel(q_ref, k_ref, v_ref, o_ref, acc_n, m_n, l_n):
    seq = q_ref.shape[0]
    blk = ATTN_BLOCK
    n_pat = len(DILATED_PATTERNS)

    qi = lax.broadcasted_iota(I32, (blk, 2 * blk), 0)
    kj = lax.broadcasted_iota(I32, (blk, 2 * blk), 1)
    band2 = (kj >= qi) & (kj <= qi + blk)
    qi1 = lax.broadcasted_iota(I32, (blk, blk), 0)
    kj1 = lax.broadcasted_iota(I32, (blk, blk), 1)
    causal1 = kj1 <= qi1
    nt = (((1,), (1,)), ((), ()))

    ones = jnp.ones((2 * blk, HEAD_DIM), BF16)

    def rows(d, r, g0, nblocks):
        return pl.ds(r + d * blk * g0, nblocks * blk, stride=d) if d > 1 else pl.ds(blk * g0, nblocks * blk)

    for p, (window, d) in enumerate(DILATED_PATTERNS):
        assert window // d == blk
        for r in range(d):
            for g in range((seq // d) // blk):
                q = q_ref[rows(d, r, g, 1), :].astype(BF16)
                if g == 0:
                    kk = k_ref[rows(d, r, 0, 1), :].astype(BF16)
                    vv = v_ref[rows(d, r, 0, 1), :].astype(BF16)
                    mask, nk = causal1, blk
                else:
                    kk = k_ref[rows(d, r, g - 1, 2), :].astype(BF16)
                    vv = v_ref[rows(d, r, g - 1, 2), :].astype(BF16)
                    mask, nk = band2, 2 * blk
                s = lax.dot_general(q, kk, nt, preferred_element_type=F32)
                s = jnp.where(mask, s, -jnp.inf)
                m = jnp.max(s, axis=1, keepdims=True)
                e = jnp.exp(s - m)
                v_ones = jnp.concatenate([vv, ones[0:nk]], axis=1)
                acc_l = jnp.dot(e.astype(BF16), v_ones, preferred_element_type=F32)
                dst = rows(d, r, g, 1)
                acc_n[p, dst, :] = acc_l[:, 0:HEAD_DIM]
                m_n[p, dst, :] = jnp.broadcast_to(m, (blk, HEAD_DIM))
                l_n[p, dst, :] = acc_l[:, HEAD_DIM:]

    mx = m_n[0]
    for p in range(1, n_pat):
        mx = jnp.maximum(mx, m_n[p])
    num = jnp.zeros((seq, HEAD_DIM), F32)
    den = jnp.zeros((seq, HEAD_DIM), F32)
    for p in range(n_pat):
        w = jnp.exp(m_n[p] - mx)
        num = num + w * acc_n[p]
        den = den + w * l_n[p]
    o_ref[...] = (num / den).astype(o_ref.dtype)


def _attention(proj3, width):
    b, s, _ = proj3.shape
    n_heads = width // HEAD_DIM
    n_pat = len(DILATED_PATTERNS)
    col0 = 2 * width // HEAD_DIM

    def spec(group):
        return pl.BlockSpec((None, s, HEAD_DIM),
                            lambda bi, hi: (bi, 0, col0 + group * n_heads + hi))

    return pl.pallas_call(
        _attn_kernel,
        grid=(b, n_heads),
        in_specs=[spec(0), spec(1), spec(2)],
        out_specs=pl.BlockSpec((None, s, HEAD_DIM), lambda bi, hi: (bi, 0, hi)),
        out_shape=jax.ShapeDtypeStruct((b, s, width), BF16),
        scratch_shapes=[pltpu.VMEM((n_pat, s, HEAD_DIM), F32)] * 3,
        compiler_params=_cparams(("arbitrary", "arbitrary")),
        name="dilated_attention",
    )(proj3, proj3, proj3)


def _layer_norm(y, g, b):
    mu = jnp.mean(y, axis=-1, keepdims=True)
    yc = y - mu
    var = jnp.mean(yc * yc, axis=-1, keepdims=True)
    return yc * lax.rsqrt(var + LN_EPS) * g + b


def _first_index(hit, iota, size):
    return jnp.min(jnp.where(hit, iota, size), axis=0, keepdims=True)


def _select_experts(scores, sel):
    tok = scores.shape[1]
    neg = -jnp.inf
    gi = lax.broadcasted_iota(I32, (GROUP_SIZE, tok), 0)
    ni = lax.broadcasted_iota(I32, (N_GROUPS, tok), 0)
    gscore = jnp.zeros((N_GROUPS, tok), F32)
    for g in range(N_GROUPS):
        blk = sel[g * GROUP_SIZE:(g + 1) * GROUP_SIZE, :]
        m1 = jnp.max(blk, axis=0, keepdims=True)
        f1 = _first_index(blk == m1, gi, GROUP_SIZE)
        m2 = jnp.max(jnp.where(gi == f1, neg, blk), axis=0, keepdims=True)
        gscore = jnp.where(ni == g, m1 + m2, gscore)
    gmask = jnp.zeros((N_GROUPS, tok), jnp.bool_)
    for _ in range(TOPK_GROUPS):
        m = jnp.max(gscore, axis=0, keepdims=True)
        hit = ni == _first_index(gscore == m, ni, N_GROUPS)
        gmask = gmask | hit
        gscore = jnp.where(hit, neg, gscore)
    emask = jnp.concatenate(
        [jnp.broadcast_to(gmask[g:g + 1, :], (GROUP_SIZE, tok)) for g in range(N_GROUPS)], axis=0)
    masked = jnp.where(emask, sel, neg)

    ei = lax.broadcasted_iota(I32, (N_EXPERTS, tok), 0)
    chosen = jnp.zeros((N_EXPERTS, tok), jnp.bool_)
    picks, gates = [], []
    for _ in range(TOP_K):
        m = jnp.max(masked, axis=0, keepdims=True)
        f = _first_index(masked == m, ei, N_EXPERTS)
        hit = ei == f
        picks.append(f)
        gates.append(jnp.sum(jnp.where(hit, scores, 0.0), axis=0, keepdims=True))
        chosen = chosen | hit
        masked = jnp.where(hit, neg, masked)
    gsum = gates[0]
    for k in range(1, TOP_K):
        gsum = gsum + gates[k]
    gates = [g / gsum * ROUTED_SCALE for g in gates]
    return picks, gates, jnp.where(chosen, 1.0, 0.0)


def _mix_kernel(lru_ref, att_ref, x_ref, wo_ref, g_ref, b_ref, wrh_ref, wrl_ref, rb_ref, tri_ref,
                x1_ref, x1p_ref, idx_ref, gate_ref, rank_ref, cnt_ref, base_ref):
    tm = x_ref.shape[0]
    half = lru_ref.shape[1]

    @pl.when(pl.program_id(0) == 0)
    def _():
        base_ref[...] = jnp.zeros_like(base_ref)

    mix = jnp.dot(lru_ref[...], wo_ref[0:half, :], preferred_element_type=F32)
    mix = mix + jnp.dot(att_ref[...], wo_ref[half:, :], preferred_element_type=F32)
    x1 = _layer_norm(DN_ALPHA * x_ref[...] + mix, g_ref[...], b_ref[...])
    x1_ref[...] = x1
    _store_rows_as_tiles(x1p_ref, _pack_bf16_pair(x1[:, 0:half], x1[:, half:]))

    x_hi = x1.astype(BF16)
    x_lo = (x1 - x_hi.astype(F32)).astype(BF16)
    nt = (((1,), (1,)), ((), ()))
    logits = lax.dot_general(wrh_ref[...], x_hi, nt, preferred_element_type=F32)
    logits = logits + lax.dot_general(wrh_ref[...], x_lo, nt, preferred_element_type=F32)
    logits = logits + lax.dot_general(wrl_ref[...], x_hi, nt, preferred_element_type=F32)
    scores = jax.nn.sigmoid(logits)
    sel = scores + rb_ref[...]

    chunks = [_select_experts(scores[:, c:c + LANES], sel[:, c:c + LANES])
              for c in range(0, tm, LANES)]

    chosen_f = jnp.concatenate([ch[2] for ch in chunks], axis=1)
    incl = jnp.dot(chosen_f.astype(BF16), tri_ref[...], preferred_element_type=F32)
    rank_excl = base_ref[:, 0:1] + incl - chosen_f
    base_ref[...] = base_ref[...] + jnp.sum(chosen_f, axis=1, keepdims=True)
    cnt_ref[...] = base_ref[...]
    ei = lax.broadcasted_iota(I32, (N_EXPERTS, LANES), 0)
    for ci, (picks, gates, _) in enumerate(chunks):
        cols = slice(ci * LANES, (ci + 1) * LANES)
        for k in range(TOP_K):
            idx_ref[k:k + 1, cols] = picks[k]
            gate_ref[k:k + 1, cols] = gates[k]
            rk = jnp.sum(jnp.where(ei == picks[k], rank_excl[:, cols], 0.0), axis=0, keepdims=True)
            rank_ref[k:k + 1, cols] = rk.astype(I32)


def _mix_route(lru, att, x2, wo_bf, ln_g, ln_b, w_router, router_bias):
    n, d = x2.shape
    half = lru.shape[1]
    wr_t = w_router.T
    wr_hi = wr_t.astype(BF16)
    wr_lo = (wr_t - wr_hi.astype(F32)).astype(BF16)
    assert half == TILE_WORDS
    tri = jnp.triu(jnp.ones((MIX_TM, MIX_TM), BF16))
    const = lambda shape: pl.BlockSpec(shape, lambda i: (0,) * len(shape))
    tok = lambda dtype: jax.ShapeDtypeStruct((TOP_K, n), dtype)
    tok_spec = pl.BlockSpec((TOP_K, MIX_TM), lambda i: (0, i))
    return pl.pallas_call(
        _mix_kernel,
        grid=(n // MIX_TM,),
        in_specs=[pl.BlockSpec((MIX_TM, half), lambda i: (i, 0)),
                  pl.BlockSpec((MIX_TM, half), lambda i: (i, 0)),
                  pl.BlockSpec((MIX_TM, d), lambda i: (i, 0)),
                  const((d, d)), const((1, d)), const((1, d)),
                  const((N_EXPERTS, d)), const((N_EXPERTS, d)), const((N_EXPERTS, 1)),
                  const((MIX_TM, MIX_TM))],
        out_specs=[pl.BlockSpec((MIX_TM, d), lambda i: (i, 0)),
                   pl.BlockSpec((MIX_TM * SUBLANES, LANES), lambda i: (i, 0)),
                   tok_spec, tok_spec, tok_spec, const((N_EXPERTS, HEAD_DIM))],
        out_shape=[jax.ShapeDtypeStruct((n, d), F32), jax.ShapeDtypeStruct((n * SUBLANES, LANES), U32),
                   tok(I32), tok(F32), tok(I32),
                   jax.ShapeDtypeStruct((N_EXPERTS, HEAD_DIM), F32)],
        scratch_shapes=[pltpu.VMEM((N_EXPERTS, HEAD_DIM), F32)],
        compiler_params=_cparams(("arbitrary",)),
        name="mix_ln_route",
    )(lru, att, x2, wo_bf, ln_g.reshape(1, d), ln_b.reshape(1, d), wr_hi, wr_lo,
      router_bias.reshape(N_EXPERTS, 1), tri)


def _dispatch_kernel(start_ref, cnt_ref, pos_ref, x1_ref, zero_ref, xs_hbm, sem):
    i = pl.program_id(0)
    tt = pos_ref.shape[1]

    def row_copy(t, k):
        return pltpu.make_async_copy(x1_ref.at[t], xs_hbm.at[pos_ref[k, t]], sem)

    def issue(t, c):
        for k in range(TOP_K):
            row_copy(t, k).start(priority=k % DMA_QUEUES)
        return c

    def drain(t, c):
        for k in range(TOP_K):
            row_copy(t, k).wait()
        return c

    lax.fori_loop(0, tt, issue, 0)
    lax.fori_loop(0, tt, drain, 0)

    @pl.when(i == 0)
    def _():
        def pad_copy(r, rows):
            return pltpu.make_async_copy(zero_ref.at[pl.ds(0, rows)], xs_hbm.at[pl.ds(r, rows)], sem)

        def per_expert(e, c):
            lo = start_ref[e] + cnt_ref[e]
            hi = start_ref[e + 1]
            chunks = (hi - lo) // ZERO_ROWS
            mid = lo + chunks * ZERO_ROWS
            big = lambda j: pad_copy(lo + j * ZERO_ROWS, ZERO_ROWS)
            lax.fori_loop(0, chunks, lambda j, cc: (big(j).start(), cc)[1], 0)
            lax.fori_loop(mid, hi, lambda r, cc: (pad_copy(r, 1).start(), cc)[1], 0)
            lax.fori_loop(0, chunks, lambda j, cc: (big(j).wait(), cc)[1], 0)
            lax.fori_loop(mid, hi, lambda r, cc: (pad_copy(r, 1).wait(), cc)[1], 0)
            return c

        lax.fori_loop(0, N_EXPERTS, per_expert, 0)


def _dispatch(x1p, pos_t, start, counts, total_rows):
    n = x1p.shape[0]
    smem_tok = pl.BlockSpec((TOP_K, DISPATCH_TT), lambda i, *_: (0, i), memory_space=pltpu.SMEM)
    grid_spec = pltpu.PrefetchScalarGridSpec(
        num_scalar_prefetch=2,
        grid=(n // DISPATCH_TT,),
        in_specs=[smem_tok,
                  pl.BlockSpec((DISPATCH_TT, SUBLANES, LANES), lambda i, *_: (i, 0, 0)),
                  pl.BlockSpec((ZERO_ROWS, SUBLANES, LANES), lambda i, *_: (0, 0, 0))],
        out_specs=pl.BlockSpec(memory_space=pl.ANY),
        scratch_shapes=[pltpu.SemaphoreType.DMA(())],
    )
    return pl.pallas_call(
        _dispatch_kernel,
        grid_spec=grid_spec,
        out_shape=jax.ShapeDtypeStruct((total_rows, SUBLANES, LANES), x1p.dtype),
        compiler_params=_cparams(("arbitrary",)),
        name="dispatch",
    )(start, counts, pos_t, x1p, jnp.zeros((ZERO_ROWS, SUBLANES, LANES), x1p.dtype))


def _expert_kernel(be_ref, used_ref, nxt_ref, slot_ref, valid_ref, xs_ref, wg_hbm, wu_hbm, wd_hbm,
                   ys_ref, wg_f32, wu_f32, wd_f32, wg_bf, wu_bf, wd_bf, sems):
    i = pl.program_id(0)
    e = be_ref[i]
    slot = slot_ref[i]
    changed = (i == 0) | (e != be_ref[jnp.maximum(i - 1, 0)])

    def weight_copies(expert, s):
        return (pltpu.make_async_copy(wg_hbm.at[expert], wg_f32.at[s], sems.at[0, s]),
                pltpu.make_async_copy(wu_hbm.at[expert], wu_f32.at[s], sems.at[1, s]),
                pltpu.make_async_copy(wd_hbm.at[expert], wd_f32.at[s], sems.at[2, s]))

    @pl.when(i == 0)
    def _():
        for c in weight_copies(e, slot):
            c.start(priority=1)

    @pl.when(changed)
    def _():
        for c in weight_copies(e, slot):
            c.wait()
        nxt = nxt_ref[i]

        @pl.when(nxt != e)
        def _():
            for c in weight_copies(nxt, 1 - slot):
                c.start(priority=1)

        wg_bf[...] = wg_f32[slot].astype(BF16)
        wu_bf[...] = wu_f32[slot].astype(BF16)
        wd_bf[...] = wd_f32[slot].astype(BF16)

    def swiglu(rows):
        half = TILE_WORDS
        lo, hi = _unpack_bf16_pair(_load_rows_from_tiles(xs_ref, 0, rows))
        lo, hi = lo.astype(BF16), hi.astype(BF16)
        g = (jnp.dot(lo, wg_bf[0:half, :], preferred_element_type=F32)
             + jnp.dot(hi, wg_bf[half:, :], preferred_element_type=F32))
        u = (jnp.dot(lo, wu_bf[0:half, :], preferred_element_type=F32)
             + jnp.dot(hi, wu_bf[half:, :], preferred_element_type=F32))
        h = (g * jax.nn.sigmoid(g) * u).astype(BF16)
        y = jnp.dot(h, wd_bf[...], preferred_element_type=F32)
        _store_rows_as_tiles(ys_ref, _pack_bf16_pair(y[:, 0:half], y[:, half:]))
        if rows < EXPERT_ROWS:
            ys_ref[rows * SUBLANES:, :] = jnp.zeros(((EXPERT_ROWS - rows) * SUBLANES, LANES), U32)

    in_use = i < used_ref[0]
    few = valid_ref[i] <= EXPERT_ROWS // 2

    @pl.when(in_use & jnp.logical_not(few))
    def _():
        swiglu(EXPERT_ROWS)

    @pl.when(in_use & few)
    def _():
        swiglu(EXPERT_ROWS // 2)

    @pl.when(jnp.logical_not(in_use))
    def _():
        ys_ref[...] = jnp.zeros_like(ys_ref)


def _experts(xs, blk_expert, used, blk_valid, w_gate, w_up, w_down):
    rows = xs.shape[0] // SUBLANES
    d = 2 * TILE_WORDS
    de = w_gate.shape[2]
    nblk = rows // EXPERT_ROWS
    later = blk_expert[None, :] > blk_expert[:, None]
    nxt = jnp.min(jnp.where(later, blk_expert[None, :], N_EXPERTS), axis=1)
    nxt = jnp.where(nxt == N_EXPERTS, blk_expert, nxt)
    is_new = jnp.concatenate([jnp.zeros((1,), I32), (blk_expert[1:] != blk_expert[:-1]).astype(I32)])
    slot = jnp.cumsum(is_new) % 2
    grid_spec = pltpu.PrefetchScalarGridSpec(
        num_scalar_prefetch=5,
        grid=(nblk,),
        in_specs=[pl.BlockSpec((EXPERT_ROWS * SUBLANES, LANES),
                               lambda i, be, used, *_: (jnp.minimum(i, used[0] - 1), 0)),
                  pl.BlockSpec(memory_space=pl.ANY), pl.BlockSpec(memory_space=pl.ANY),
                  pl.BlockSpec(memory_space=pl.ANY)],
        out_specs=pl.BlockSpec((EXPERT_ROWS * SUBLANES, LANES), lambda i, *_: (i, 0)),
        scratch_shapes=[pltpu.VMEM((2, d, de), F32), pltpu.VMEM((2, d, de), F32),
                        pltpu.VMEM((2, de, d), F32),
                        pltpu.VMEM((d, de), BF16), pltpu.VMEM((d, de), BF16),
                        pltpu.VMEM((de, d), BF16), pltpu.SemaphoreType.DMA((3, 2))],
    )
    return pl.pallas_call(
        _expert_kernel,
        grid_spec=grid_spec,
        out_shape=jax.ShapeDtypeStruct((rows * SUBLANES, LANES), U32),
        compiler_params=_cparams(("arbitrary",)),
        name="experts",
    )(blk_expert, used, nxt.astype(I32), slot.astype(I32), blk_valid, xs, w_gate, w_up, w_down)


def _combine_kernel(pos_ref, pos_next_ref, gate_ref, x1_ref, ys_hbm, wsg_ref, wsu_ref,
                    wsd_ref, g_ref, b_ref, o_ref, buf0, buf1, sems):
    tt = x1_ref.shape[0]
    i = pl.program_id(0)
    even = i % 2 == 0

    def row_copy(positions, buf, s, t, k):
        dst = pl.multiple_of((k * tt + t) * SUBLANES, SUBLANES)
        return pltpu.make_async_copy(ys_hbm.at[positions[k, t]], buf.at[pl.ds(dst, SUBLANES)],
                                     sems.at[s])

    def gather(positions, buf, s, start):
        def step(t, c):
            for k in range(TOP_K):
                copy = row_copy(positions, buf, s, t, k)
                copy.start(priority=k % DMA_QUEUES) if start else copy.wait()
            return c
        lax.fori_loop(0, tt, step, 0)

    @pl.when(i == 0)
    def _():
        gather(pos_ref, buf0, 0, True)

    has_next = i + 1 < pl.num_programs(0)

    @pl.when(has_next & even)
    def _():
        gather(pos_next_ref, buf1, 1, True)

    @pl.when(has_next & jnp.logical_not(even))
    def _():
        gather(pos_next_ref, buf0, 0, True)

    x1 = x1_ref[...]
    xb = x1.astype(BF16)
    g = jnp.dot(xb, wsg_ref[...], preferred_element_type=F32)
    u = jnp.dot(xb, wsu_ref[...], preferred_element_type=F32)
    h = (g * jax.nn.sigmoid(g) * u).astype(BF16)
    y = DN_ALPHA * x1 + jnp.dot(h, wsd_ref[...], preferred_element_type=F32)

    gate = gate_ref[...]

    def finish(buf, s):
        gather(pos_ref, buf, s, False)
        r_lo = jnp.zeros((tt, TILE_WORDS), F32)
        r_hi = jnp.zeros((tt, TILE_WORDS), F32)
        for k in range(TOP_K):
            lo, hi = _unpack_bf16_pair(_load_rows_from_tiles(buf, k * tt, tt))
            r_lo = r_lo + lo * gate[:, k:k + 1]
            r_hi = r_hi + hi * gate[:, k:k + 1]
        out = y + jnp.concatenate([r_lo, r_hi], axis=1)
        o_ref[...] = _layer_norm(out, g_ref[...], b_ref[...])

    @pl.when(even)
    def _():
        finish(buf0, 0)

    @pl.when(jnp.logical_not(even))
    def _():
        finish(buf1, 1)


def _combine(x1, ys, pos_t, gate_nk, wsg_bf, wsu_bf, wsd_bf, ln_g, ln_b):
    n, d = x1.shape
    de = wsg_bf.shape[1]
    smem_tok = pl.BlockSpec((TOP_K, COMBINE_TT), lambda i, *_: (0, i), memory_space=pltpu.SMEM)
    steps = n // COMBINE_TT
    smem_next = pl.BlockSpec((TOP_K, COMBINE_TT), lambda i, *_: (0, jnp.minimum(i + 1, steps - 1)),
                             memory_space=pltpu.SMEM)
    const = lambda shape: pl.BlockSpec(shape, lambda i, *_: (0,) * len(shape))
    buf = pltpu.VMEM((TOP_K * COMBINE_TT * SUBLANES, LANES), U32)
    grid_spec = pltpu.PrefetchScalarGridSpec(
        num_scalar_prefetch=0,
        grid=(steps,),
        in_specs=[smem_tok, smem_next,
                  pl.BlockSpec((COMBINE_TT, TOP_K), lambda i, *_: (i, 0)),
                  pl.BlockSpec((COMBINE_TT, d), lambda i, *_: (i, 0)),
                  pl.BlockSpec(memory_space=pl.ANY),
                  const((d, de)), const((d, de)), const((de, d)), const((1, d)), const((1, d))],
        out_specs=pl.BlockSpec((COMBINE_TT, d), lambda i, *_: (i, 0)),
        scratch_shapes=[buf, buf, pltpu.SemaphoreType.DMA((2,))],
    )
    return pl.pallas_call(
        _combine_kernel,
        grid_spec=grid_spec,
        out_shape=jax.ShapeDtypeStruct((n, d), F32),
        compiler_params=_cparams(("arbitrary",)),
        name="combine_ln",
    )(pos_t, pos_t, gate_nk, x1, ys, wsg_bf, wsu_bf, wsd_bf, ln_g.reshape(1, d), ln_b.reshape(1, d))


def _block_layout(counts_f, n_assign):
    counts = counts_f.astype(I32)
    padded = (counts + EXPERT_ROWS - 1) // EXPERT_ROWS * EXPERT_ROWS
    end = jnp.cumsum(padded)
    total_rows = n_assign + N_EXPERTS * EXPERT_ROWS
    start = jnp.concatenate([end - padded, jnp.full((1,), total_rows, I32)])
    nblk = total_rows // EXPERT_ROWS
    used = end[-1:] // EXPERT_ROWS
    blk_start = jnp.minimum(jnp.arange(nblk, dtype=I32), used[0] - 1) * EXPERT_ROWS
    blk_expert = jnp.sum((end[None, :] <= blk_start[:, None]).astype(I32), axis=1)
    blk_expert = jnp.minimum(blk_expert, N_EXPERTS - 1)
    mine = blk_expert[:, None] == jnp.arange(N_EXPERTS, dtype=I32)[None, :]
    row_end = jnp.sum(jnp.where(mine, (start[:N_EXPERTS] + counts)[None, :], 0), axis=1)
    blk_valid = jnp.clip(row_end - blk_start, 0, EXPERT_ROWS)
    return start, counts, blk_expert, used.astype(I32), blk_valid, total_rows


def kernel(x, w_in, conv_w, conv_b, w_rg_a, b_rg_a, w_rg_x, b_rg_x, lru_lambda, w_out, ln1_g, ln1_b,
           w_router, router_bias, w_gate, w_up, w_down, ws_gate, ws_up, ws_down, ln2_g, ln2_b):
    b, s, d = x.shape
    n = b * s
    width = d // 2
    for l in range(DEPTH):
        x2 = x.reshape(n, d)
        proj = _proj(x2, w_in[l], s, width)
        proj3 = proj.reshape(b, s, proj.shape[1])
        lru = _rglru(proj3, conv_w[l], conv_b[l], w_rg_a[l].astype(BF16), b_rg_a[l],
                     w_rg_x[l].astype(BF16), b_rg_x[l], lru_lambda[l], width)
        att = _attention(proj3, width)
        x1, x1p, idx_t, gate_t, rank_t, cnt = _mix_route(
            lru.reshape(n, width), att.reshape(n, width), x2, w_out[l].astype(BF16),
            ln1_g[l], ln1_b[l], w_router[l], router_bias[l])
        start, counts, blk_expert, used, blk_valid, total_rows = _block_layout(cnt[:, 0], n * TOP_K)
        onehot = idx_t[:, :, None] == jnp.arange(N_EXPERTS, dtype=I32)[None, None, :]
        pos_t = jnp.sum(jnp.where(onehot, start[None, None, :N_EXPERTS], 0), axis=2) + rank_t
        xs = _dispatch(x1p.reshape(n, SUBLANES, LANES), pos_t, start, counts, total_rows)
        ys = _experts(xs.reshape(total_rows * SUBLANES, LANES), blk_expert, used, blk_valid,
                      w_gate[l], w_up[l], w_down[l])
        out = _combine(x1, ys.reshape(total_rows, SUBLANES, LANES), pos_t, gate_t.T,
                       ws_gate[l].astype(BF16),
                       ws_up[l].astype(BF16), ws_down[l].astype(BF16), ln2_g[l], ln2_b[l])
        x = out.reshape(b, s, d)
    return x
```

```python
import functools

import jax
import jax.numpy as jnp
from jax import lax
from jax.experimental import pallas as pl
from jax.experimental.pallas import tpu as pltpu

F32 = jnp.float32
BF16 = jnp.bfloat16
I32 = jnp.int32
U32 = jnp.uint32

LRU_BLOCKS = 8
CONV_W = 4
LRU_C = 8.0
HEAD_DIM = 128
ROT_DIM = HEAD_DIM // 4
ROPE_THETA = 500000.0
DILATED_PATTERNS = ((128, 1), (512, 4), (2048, 16))
ATTN_BLOCK = 128
N_EXPERTS = 64
N_GROUPS = 8
GROUP_SIZE = N_EXPERTS // N_GROUPS
TOPK_GROUPS = 4
TOP_K = 8
ROUTED_SCALE = 2.5
DEPTH = 1
DN_ALPHA = (2 * DEPTH) ** 0.25
LN_EPS = 1e-5

PROJ_TM = 1024
LRU_TC = 512
MIX_TM = 512
EXPERT_ROWS = 512
DISPATCH_TT = 512
COMBINE_TT = 256
ZERO_ROWS = 64
DMA_QUEUES = 2
SUBLANES = 8
LANES = 128
TILE_WORDS = SUBLANES * LANES
VMEM_LIMIT = 56 * 1024 * 1024


def _pack_bf16_pair(lo, hi):
    lo_bits = lax.bitcast_convert_type(lo.astype(BF16).astype(F32), U32)
    hi_bits = lax.bitcast_convert_type(hi.astype(BF16).astype(F32), U32)
    return (lo_bits >> 16) | (hi_bits & jnp.uint32(0xFFFF0000))


def _unpack_bf16_pair(packed):
    lo = lax.bitcast_convert_type(packed << 16, F32)
    hi = lax.bitcast_convert_type(packed & jnp.uint32(0xFFFF0000), F32)
    return lo, hi


def _store_rows_as_tiles(ref, words):
    rows = words.shape[0]
    for s in range(SUBLANES):
        ref[pl.ds(s, rows, stride=SUBLANES), :] = words[:, s * LANES:(s + 1) * LANES]


def _load_rows_from_tiles(ref, row0, rows):
    return jnp.concatenate(
        [ref[pl.ds(row0 * SUBLANES + s, rows, stride=SUBLANES), :] for s in range(SUBLANES)], axis=1)


def _cparams(sem):
    return pltpu.CompilerParams(dimension_semantics=sem, vmem_limit_bytes=VMEM_LIMIT)


def _proj_kernel(x_ref, w_ref, c_ref, sa_ref, sb_ref, o_ref, wb_ref, *, n_heads):
    j = pl.program_id(0)

    @pl.when(pl.program_id(1) == 0)
    def _():
        wb_ref[...] = w_ref[...].astype(BF16)

    acc = jnp.dot(x_ref[...].astype(BF16), wb_ref[...], preferred_element_type=F32)
    is_rope = (j == 2) | (j == 3)

    @pl.when(is_rope)
    def _():
        scale = jnp.where(j == 2, HEAD_DIM ** -0.5, 1.0).astype(F32)
        c = c_ref[...] * scale
        sa = sa_ref[...] * scale
        sb = sb_ref[...] * scale
        half = ROT_DIM // 2
        for h in range(n_heads):
            t = acc[:, h * HEAD_DIM:(h + 1) * HEAD_DIM]
            o_ref[:, h * HEAD_DIM:(h + 1) * HEAD_DIM] = (
                t * c + pltpu.roll(t, half, 1) * sa + pltpu.roll(t, HEAD_DIM - half, 1) * sb)

    @pl.when(jnp.logical_not(is_rope))
    def _():
        o_ref[...] = acc


def _rope_tables(seq):
    half = ROT_DIM // 2
    pos = jnp.arange(seq, dtype=F32)
    inv = ROPE_THETA ** (-jnp.arange(0, ROT_DIM, 2, dtype=F32) / ROT_DIM)
    ang = pos[:, None] * inv[None, :]
    cos, sin = jnp.cos(ang), jnp.sin(ang)
    ones = jnp.ones((seq, HEAD_DIM - ROT_DIM), F32)
    zeros = jnp.zeros((seq, HEAD_DIM - ROT_DIM), F32)
    zh = jnp.zeros((seq, half), F32)
    c = jnp.concatenate([cos, cos, ones], -1)
    sa = jnp.concatenate([zh, sin, zeros], -1)
    sb = jnp.concatenate([-sin, zh, zeros], -1)
    return c, sa, sb


def _proj(x2, w_in_bf, seq, width):
    n, d = x2.shape
    e = w_in_bf.shape[1]
    n_groups = e // width
    c, sa, sb = _rope_tables(seq)
    tiles_per_seq = seq // PROJ_TM
    tab_spec = pl.BlockSpec((PROJ_TM, HEAD_DIM), lambda j, i: (i % tiles_per_seq, 0))
    return pl.pallas_call(
        functools.partial(_proj_kernel, n_heads=width // HEAD_DIM),
        grid=(n_groups, n // PROJ_TM),
        in_specs=[pl.BlockSpec((PROJ_TM, d), lambda j, i: (i, 0)),
                  pl.BlockSpec((d, width), lambda j, i: (0, j)),
                  tab_spec, tab_spec, tab_spec],
        out_specs=pl.BlockSpec((PROJ_TM, width), lambda j, i: (i, j)),
        out_shape=jax.ShapeDtypeStruct((n, e), F32),
        scratch_shapes=[pltpu.VMEM((d, width), BF16)],
        compiler_params=_cparams(("arbitrary", "arbitrary")),
        name="proj_rope",
    )(x2, w_in_bf, c, sa, sb)


def _lru_kernel(xl_ref, gl_ref, cw_ref, cb_ref, wa_ref, ba_ref, wx_ref, bx_ref, lam_ref,
                o_ref, xext_ref, hprev_ref, a_ref, b_ref, ga_ref, gb_ref, hin_ref, *, block_w):
    tc, width = xl_ref.shape
    groups = tc // SUBLANES

    @pl.when(pl.program_id(1) == 0)
    def _():
        xext_ref[0:SUBLANES, :] = jnp.zeros((SUBLANES, width), F32)
        hprev_ref[...] = jnp.zeros_like(hprev_ref)

    x = xl_ref[...]
    xext_ref[SUBLANES:SUBLANES + tc, :] = x
    xc = x * cw_ref[CONV_W - 1:CONV_W, :] + cb_ref[...]
    for j in range(CONV_W - 1):
        back = CONV_W - 1 - j
        xc = xc + xext_ref[pl.ds(SUBLANES - back, tc), :] * cw_ref[j:j + 1, :]
    xext_ref[0:SUBLANES, :] = x[tc - SUBLANES:, :]

    lam = lam_ref[...]
    softplus_neg_lam = jnp.maximum(-lam, 0.0) + jnp.log1p(jnp.exp(-jnp.abs(lam)))
    rs, is_ = [], []
    for nb in range(width // block_w):
        xb = xc[:, nb * block_w:(nb + 1) * block_w].astype(BF16)
        rs.append(jnp.dot(xb, wa_ref[nb], preferred_element_type=F32))
        is_.append(jnp.dot(xb, wx_ref[nb], preferred_element_type=F32))
    r = jax.nn.sigmoid(jnp.concatenate(rs, axis=1) + ba_ref[...])
    ig = jax.nn.sigmoid(jnp.concatenate(is_, axis=1) + bx_ref[...])
    a = jnp.exp(-LRU_C * r * softplus_neg_lam)
    b = jnp.sqrt(1.0 - a * a) * (ig * xc)
    ncol = width // LANES
    for c in range(ncol):
        a_ref[c] = a[:, c * LANES:(c + 1) * LANES]
        b_ref[c] = b[:, c * LANES:(c + 1) * LANES]

    def member(ref, j):
        return jnp.concatenate(
            [ref[c, pl.ds(j, groups, stride=SUBLANES), :] for c in range(ncol)], axis=1)

    big_a, big_b = [member(a_ref, 0)], [member(b_ref, 0)]
    for j in range(1, SUBLANES):
        aj = member(a_ref, j)
        big_b.append(aj * big_b[-1] + member(b_ref, j))
        big_a.append(aj * big_a[-1])
    ga_ref[...] = big_a[-1]
    gb_ref[...] = big_b[-1]

    def carry(g, hprev):
        hin_ref[pl.ds(g, 1), :] = hprev
        return ga_ref[pl.ds(g, 1), :] * hprev + gb_ref[pl.ds(g, 1), :]

    hprev_ref[...] = lax.fori_loop(0, groups, carry, hprev_ref[...])

    hin = hin_ref[...]
    for j in range(SUBLANES):
        hj = big_a[j] * hin + big_b[j]
        for c in range(ncol):
            a_ref[c, pl.ds(j, groups, stride=SUBLANES), :] = hj[:, c * LANES:(c + 1) * LANES]
    h = jnp.concatenate([a_ref[c] for c in range(ncol)], axis=1)
    o_ref[...] = (h * jax.nn.gelu(gl_ref[...], approximate=True)).astype(o_ref.dtype)


def _rglru(proj3, conv_w, conv_b, wa_bf, b_a, wx_bf, b_x, lam, width):
    b, s, _ = proj3.shape
    block_w = width // LRU_BLOCKS
    vec = pl.BlockSpec((1, width), lambda bi, ci: (0, 0))
    wspec = pl.BlockSpec((LRU_BLOCKS, block_w, block_w), lambda bi, ci: (0, 0, 0))
    groups = LRU_TC // SUBLANES
    return pl.pallas_call(
        functools.partial(_lru_kernel, block_w=block_w),
        grid=(b, s // LRU_TC),
        in_specs=[pl.BlockSpec((None, LRU_TC, width), lambda bi, ci: (bi, ci, 0)),
                  pl.BlockSpec((None, LRU_TC, width), lambda bi, ci: (bi, ci, 1)),
                  pl.BlockSpec((CONV_W, width), lambda bi, ci: (0, 0)),
                  vec, wspec, vec, wspec, vec, vec],
        out_specs=pl.BlockSpec((None, LRU_TC, width), lambda bi, ci: (bi, ci, 0)),
        out_shape=jax.ShapeDtypeStruct((b, s, width), BF16),
        scratch_shapes=[pltpu.VMEM((LRU_TC + SUBLANES, width), F32),
                        pltpu.VMEM((1, width), F32),
                        pltpu.VMEM((width // LANES, LRU_TC, LANES), F32),
                        pltpu.VMEM((width // LANES, LRU_TC, LANES), F32),
                        pltpu.VMEM((groups, width), F32),
                        pltpu.VMEM((groups, width), F32),
                        pltpu.VMEM((groups, width), F32)],
        compiler_params=_cparams(("arbitrary", "arbitrary")),
        name="rglru",
    )(proj3, proj3, conv_w, conv_b.reshape(1, width), wa_bf, b_a.reshape(1, width),
      wx_bf, b_x.reshape(1, width), lam.reshape(1, width))


def _attn_kernel(q_ref, k_ref, v_ref, o_ref, acc_n, m_n, l_n):
    seq = q_ref.shape[0]
    blk = ATTN_BLOCK
    n_pat = len(DILATED_PATTERNS)

    qi = lax.broadcasted_iota(I32, (blk, 2 * blk), 0)
    kj = lax.broadcasted_iota(I32, (blk, 2 * blk), 1)
    band2 = (kj >= qi) & (kj <= qi + blk)
    qi1 = lax.broadcasted_iota(I32, (blk, blk), 0)
    kj1 = lax.broadcasted_iota(I32, (blk, blk), 1)
    causal1 = kj1 <= qi1
    nt = (((1,), (1,)), ((), ()))

    ones = jnp.ones((2 * blk, HEAD_DIM), BF16)

    def rows(d, r, g0, nblocks):
        return pl.ds(r + d * blk * g0, nblocks * blk, stride=d) if d > 1 else pl.ds(blk * g0, nblocks * blk)

    for p, (window, d) in enumerate(DILATED_PATTERNS):
        assert window // d == blk
        for r in range(d):
            for g in range((seq // d) // blk):
                q = q_ref[rows(d, r, g, 1), :].astype(BF16)
                if g == 0:
                    kk = k_ref[rows(d, r, 0, 1), :].astype(BF16)
                    vv = v_ref[rows(d, r, 0, 1), :].astype(BF16)
                    mask, nk = causal1, blk
                else:
                    kk = k_ref[rows(d, r, g - 1, 2), :].astype(BF16)
                    vv = v_ref[rows(d, r, g - 1, 2), :].astype(BF16)
                    mask, nk = band2, 2 * blk
                s = lax.dot_general(q, kk, nt, preferred_element_type=F32)
                s = jnp.where(mask, s, -jnp.inf)
                m = jnp.max(s, axis=1, keepdims=True)
                e = jnp.exp(s - m)
                v_ones = jnp.concatenate([vv, ones[0:nk]], axis=1)
                acc_l = jnp.dot(e.astype(BF16), v_ones, preferred_element_type=F32)
                dst = rows(d, r, g, 1)
                acc_n[p, dst, :] = acc_l[:, 0:HEAD_DIM]
                m_n[p, dst, :] = jnp.broadcast_to(m, (blk, HEAD_DIM))
                l_n[p, dst, :] = acc_l[:, HEAD_DIM:]

    mx = m_n[0]
    for p in range(1, n_pat):
        mx = jnp.maximum(mx, m_n[p])
    num = jnp.zeros((seq, HEAD_DIM), F32)
    den = jnp.zeros((seq, HEAD_DIM), F32)
    for p in range(n_pat):
        w = jnp.exp(m_n[p] - mx)
        num = num + w * acc_n[p]
        den = den + w * l_n[p]
    o_ref[...] = (num / den).astype(o_ref.dtype)


def _attention(proj3, width):
    b, s, _ = proj3.shape
    n_heads = width // HEAD_DIM
    n_pat = len(DILATED_PATTERNS)
    col0 = 2 * width // HEAD_DIM

    def spec(group):
        return pl.BlockSpec((None, s, HEAD_DIM),
                            lambda bi, hi: (bi, 0, col0 + group * n_heads + hi))

    return pl.pallas_call(
        _attn_kernel,
        grid=(b, n_heads),
        in_specs=[spec(0), spec(1), spec(2)],
        out_specs=pl.BlockSpec((None, s, HEAD_DIM), lambda bi, hi: (bi, 0, hi)),
        out_shape=jax.ShapeDtypeStruct((b, s, width), BF16),
        scratch_shapes=[pltpu.VMEM((n_pat, s, HEAD_DIM), F32)] * 3,
        compiler_params=_cparams(("arbitrary", "arbitrary")),
        name="dilated_attention",
    )(proj3, proj3, proj3)


def _layer_norm(y, g, b):
    mu = jnp.mean(y, axis=-1, keepdims=True)
    yc = y - mu
    var = jnp.mean(yc * yc, axis=-1, keepdims=True)
    return yc * lax.rsqrt(var + LN_EPS) * g + b


def _first_index(hit, iota, size):
    return jnp.min(jnp.where(hit, iota, size), axis=0, keepdims=True)


def _select_experts(scores, sel):
    tok = scores.shape[1]
    neg = -jnp.inf
    gi = lax.broadcasted_iota(I32, (GROUP_SIZE, tok), 0)
    ni = lax.broadcasted_iota(I32, (N_GROUPS, tok), 0)
    gscore = jnp.zeros((N_GROUPS, tok), F32)
    for g in range(N_GROUPS):
        blk = sel[g * GROUP_SIZE:(g + 1) * GROUP_SIZE, :]
        m1 = jnp.max(blk, axis=0, keepdims=True)
        f1 = _first_index(blk == m1, gi, GROUP_SIZE)
        m2 = jnp.max(jnp.where(gi == f1, neg, blk), axis=0, keepdims=True)
        gscore = jnp.where(ni == g, m1 + m2, gscore)
    gmask = jnp.zeros((N_GROUPS, tok), jnp.bool_)
    for _ in range(TOPK_GROUPS):
        m = jnp.max(gscore, axis=0, keepdims=True)
        hit = ni == _first_index(gscore == m, ni, N_GROUPS)
        gmask = gmask | hit
        gscore = jnp.where(hit, neg, gscore)
    emask = jnp.concatenate(
        [jnp.broadcast_to(gmask[g:g + 1, :], (GROUP_SIZE, tok)) for g in range(N_GROUPS)], axis=0)
    masked = jnp.where(emask, sel, neg)

    ei = lax.broadcasted_iota(I32, (N_EXPERTS, tok), 0)
    chosen = jnp.zeros((N_EXPERTS, tok), jnp.bool_)
    picks, gates = [], []
    for _ in range(TOP_K):
        m = jnp.max(masked, axis=0, keepdims=True)
        f = _first_index(masked == m, ei, N_EXPERTS)
        hit = ei == f
        picks.append(f)
        gates.append(jnp.sum(jnp.where(hit, scores, 0.0), axis=0, keepdims=True))
        chosen = chosen | hit
        masked = jnp.where(hit, neg, masked)
    gsum = gates[0]
    for k in range(1, TOP_K):
        gsum = gsum + gates[k]
    gates = [g / gsum * ROUTED_SCALE for g in gates]
    return picks, gates, jnp.where(chosen, 1.0, 0.0)


def _mix_kernel(lru_ref, att_ref, x_ref, wo_ref, g_ref, b_ref, wrh_ref, wrl_ref, rb_ref, tri_ref,
                x1_ref, x1p_ref, idx_ref, gate_ref, rank_ref, cnt_ref, base_ref):
    tm = x_ref.shape[0]
    half = lru_ref.shape[1]

    @pl.when(pl.program_id(0) == 0)
    def _():
        base_ref[...] = jnp.zeros_like(base_ref)

    mix = jnp.dot(lru_ref[...], wo_ref[0:half, :], preferred_element_type=F32)
    mix = mix + jnp.dot(att_ref[...], wo_ref[half:, :], preferred_element_type=F32)
    x1 = _layer_norm(DN_ALPHA * x_ref[...] + mix, g_ref[...], b_ref[...])
    x1_ref[...] = x1
    _store_rows_as_tiles(x1p_ref, _pack_bf16_pair(x1[:, 0:half], x1[:, half:]))

    x_hi = x1.astype(BF16)
    x_lo = (x1 - x_hi.astype(F32)).astype(BF16)
    nt = (((1,), (1,)), ((), ()))
    logits = lax.dot_general(wrh_ref[...], x_hi, nt, preferred_element_type=F32)
    logits = logits + lax.dot_general(wrh_ref[...], x_lo, nt, preferred_element_type=F32)
    logits = logits + lax.dot_general(wrl_ref[...], x_hi, nt, preferred_element_type=F32)
    scores = jax.nn.sigmoid(logits)
    sel = scores + rb_ref[...]

    chunks = [_select_experts(scores[:, c:c + LANES], sel[:, c:c + LANES])
              for c in range(0, tm, LANES)]

    chosen_f = jnp.concatenate([ch[2] for ch in chunks], axis=1)
    incl = jnp.dot(chosen_f.astype(BF16), tri_ref[...], preferred_element_type=F32)
    rank_excl = base_ref[:, 0:1] + incl - chosen_f
    base_ref[...] = base_ref[...] + jnp.sum(chosen_f, axis=1, keepdims=True)
    cnt_ref[...] = base_ref[...]
    ei = lax.broadcasted_iota(I32, (N_EXPERTS, LANES), 0)
    for ci, (picks, gates, _) in enumerate(chunks):
        cols = slice(ci * LANES, (ci + 1) * LANES)
        for k in range(TOP_K):
            idx_ref[k:k + 1, cols] = picks[k]
            gate_ref[k:k + 1, cols] = gates[k]
            rk = jnp.sum(jnp.where(ei == picks[k], rank_excl[:, cols], 0.0), axis=0, keepdims=True)
            rank_ref[k:k + 1, cols] = rk.astype(I32)


def _mix_route(lru, att, x2, wo_bf, ln_g, ln_b, w_router, router_bias):
    n, d = x2.shape
    half = lru.shape[1]
    wr_t = w_router.T
    wr_hi = wr_t.astype(BF16)
    wr_lo = (wr_t - wr_hi.astype(F32)).astype(BF16)
    assert half == TILE_WORDS
    tri = jnp.triu(jnp.ones((MIX_TM, MIX_TM), BF16))
    const = lambda shape: pl.BlockSpec(shape, lambda i: (0,) * len(shape))
    tok = lambda dtype: jax.ShapeDtypeStruct((TOP_K, n), dtype)
    tok_spec = pl.BlockSpec((TOP_K, MIX_TM), lambda i: (0, i))
    return pl.pallas_call(
        _mix_kernel,
        grid=(n // MIX_TM,),
        in_specs=[pl.BlockSpec((MIX_TM, half), lambda i: (i, 0)),
                  pl.BlockSpec((MIX_TM, half), lambda i: (i, 0)),
                  pl.BlockSpec((MIX_TM, d), lambda i: (i, 0)),
                  const((d, d)), const((1, d)), const((1, d)),
                  const((N_EXPERTS, d)), const((N_EXPERTS, d)), const((N_EXPERTS, 1)),
                  const((MIX_TM, MIX_TM))],
        out_specs=[pl.BlockSpec((MIX_TM, d), lambda i: (i, 0)),
                   pl.BlockSpec((MIX_TM * SUBLANES, LANES), lambda i: (i, 0)),
                   tok_spec, tok_spec, tok_spec, const((N_EXPERTS, HEAD_DIM))],
        out_shape=[jax.ShapeDtypeStruct((n, d), F32), jax.ShapeDtypeStruct((n * SUBLANES, LANES), U32),
                   tok(I32), tok(F32), tok(I32),
                   jax.ShapeDtypeStruct((N_EXPERTS, HEAD_DIM), F32)],
        scratch_shapes=[pltpu.VMEM((N_EXPERTS, HEAD_DIM), F32)],
        compiler_params=_cparams(("arbitrary",)),
        name="mix_ln_route",
    )(lru, att, x2, wo_bf, ln_g.reshape(1, d), ln_b.reshape(1, d), wr_hi, wr_lo,
      router_bias.reshape(N_EXPERTS, 1), tri)


def _dispatch_kernel(start_ref, cnt_ref, pos_ref, x1_ref, zero_ref, xs_hbm, sem):
    i = pl.program_id(0)
    tt = pos_ref.shape[1]

    def row_copy(t, k):
        return pltpu.make_async_copy(x1_ref.at[t], xs_hbm.at[pos_ref[k, t]], sem)

    def issue(t, c):
        for k in range(TOP_K):
            row_copy(t, k).start(priority=k % DMA_QUEUES)
        return c

    def drain(t, c):
        for k in range(TOP_K):
            row_copy(t, k).wait()
        return c

    lax.fori_loop(0, tt, issue, 0)
    lax.fori_loop(0, tt, drain, 0)

    @pl.when(i == 0)
    def _():
        def pad_copy(r, rows):
            return pltpu.make_async_copy(zero_ref.at[pl.ds(0, rows)], xs_hbm.at[pl.ds(r, rows)], sem)

        def per_expert(e, c):
            lo = start_ref[e] + cnt_ref[e]
            hi = start_ref[e + 1]
            chunks = (hi - lo) // ZERO_ROWS
            mid = lo + chunks * ZERO_ROWS
            big = lambda j: pad_copy(lo + j * ZERO_ROWS, ZERO_ROWS)
            lax.fori_loop(0, chunks, lambda j, cc: (big(j).start(), cc)[1], 0)
            lax.fori_loop(mid, hi, lambda r, cc: (pad_copy(r, 1).start(), cc)[1], 0)
            lax.fori_loop(0, chunks, lambda j, cc: (big(j).wait(), cc)[1], 0)
            lax.fori_loop(mid, hi, lambda r, cc: (pad_copy(r, 1).wait(), cc)[1], 0)
            return c

        lax.fori_loop(0, N_EXPERTS, per_expert, 0)


def _dispatch(x1p, pos_t, start, counts, total_rows):
    n = x1p.shape[0]
    smem_tok = pl.BlockSpec((TOP_K, DISPATCH_TT), lambda i, *_: (0, i), memory_space=pltpu.SMEM)
    grid_spec = pltpu.PrefetchScalarGridSpec(
        num_scalar_prefetch=2,
        grid=(n // DISPATCH_TT,),
        in_specs=[smem_tok,
                  pl.BlockSpec((DISPATCH_TT, SUBLANES, LANES), lambda i, *_: (i, 0, 0)),
                  pl.BlockSpec((ZERO_ROWS, SUBLANES, LANES), lambda i, *_: (0, 0, 0))],
        out_specs=pl.BlockSpec(memory_space=pl.ANY),
        scratch_shapes=[pltpu.SemaphoreType.DMA(())],
    )
    return pl.pallas_call(
        _dispatch_kernel,
        grid_spec=grid_spec,
        out_shape=jax.ShapeDtypeStruct((total_rows, SUBLANES, LANES), x1p.dtype),
        compiler_params=_cparams(("arbitrary",)),
        name="dispatch",
    )(start, counts, pos_t, x1p, jnp.zeros((ZERO_ROWS, SUBLANES, LANES), x1p.dtype))


def _expert_kernel(be_ref, used_ref, nxt_ref, slot_ref, valid_ref, xs_ref, wg_hbm, wu_hbm, wd_hbm,
                   ys_ref, wg_f32, wu_f32, wd_f32, wg_bf, wu_bf, wd_bf, sems):
    i = pl.program_id(0)
    e = be_ref[i]
    slot = slot_ref[i]
    changed = (i == 0) | (e != be_ref[jnp.maximum(i - 1, 0)])

    def weight_copies(expert, s):
        return (pltpu.make_async_copy(wg_hbm.at[expert], wg_f32.at[s], sems.at[0, s]),
                pltpu.make_async_copy(wu_hbm.at[expert], wu_f32.at[s], sems.at[1, s]),
                pltpu.make_async_copy(wd_hbm.at[expert], wd_f32.at[s], sems.at[2, s]))

    @pl.when(i == 0)
    def _():
        for c in weight_copies(e, slot):
            c.start(priority=1)

    @pl.when(changed)
    def _():
        for c in weight_copies(e, slot):
            c.wait()
        nxt = nxt_ref[i]

        @pl.when(nxt != e)
        def _():
            for c in weight_copies(nxt, 1 - slot):
                c.start(priority=1)

        wg_bf[...] = wg_f32[slot].astype(BF16)
        wu_bf[...] = wu_f32[slot].astype(BF16)
        wd_bf[...] = wd_f32[slot].astype(BF16)

    def swiglu(rows):
        half = TILE_WORDS
        lo, hi = _unpack_bf16_pair(_load_rows_from_tiles(xs_ref, 0, rows))
        lo, hi = lo.astype(BF16), hi.astype(BF16)
        g = (jnp.dot(lo, wg_bf[0:half, :], preferred_element_type=F32)
             + jnp.dot(hi, wg_bf[half:, :], preferred_element_type=F32))
        u = (jnp.dot(lo, wu_bf[0:half, :], preferred_element_type=F32)
             + jnp.dot(hi, wu_bf[half:, :], preferred_element_type=F32))
        h = (g * jax.nn.sigmoid(g) * u).astype(BF16)
        y = jnp.dot(h, wd_bf[...], preferred_element_type=F32)
        _store_rows_as_tiles(ys_ref, _pack_bf16_pair(y[:, 0:half], y[:, half:]))
        if rows < EXPERT_ROWS:
            ys_ref[rows * SUBLANES:, :] = jnp.zeros(((EXPERT_ROWS - rows) * SUBLANES, LANES), U32)

    in_use = i < used_ref[0]
    few = valid_ref[i] <= EXPERT_ROWS // 2

    @pl.when(in_use & jnp.logical_not(few))
    def _():
        swiglu(EXPERT_ROWS)

    @pl.when(in_use & few)
    def _():
        swiglu(EXPERT_ROWS // 2)

    @pl.when(jnp.logical_not(in_use))
    def _():
        ys_ref[...] = jnp.zeros_like(ys_ref)


def _experts(xs, blk_expert, used, blk_valid, w_gate, w_up, w_down):
    rows = xs.shape[0] // SUBLANES
    d = 2 * TILE_WORDS
    de = w_gate.shape[2]
    nblk = rows // EXPERT_ROWS
    later = blk_expert[None, :] > blk_expert[:, None]
    nxt = jnp.min(jnp.where(later, blk_expert[None, :], N_EXPERTS), axis=1)
    nxt = jnp.where(nxt == N_EXPERTS, blk_expert, nxt)
    is_new = jnp.concatenate([jnp.zeros((1,), I32), (blk_expert[1:] != blk_expert[:-1]).astype(I32)])
    slot = jnp.cumsum(is_new) % 2
    grid_spec = pltpu.PrefetchScalarGridSpec(
        num_scalar_prefetch=5,
        grid=(nblk,),
        in_specs=[pl.BlockSpec((EXPERT_ROWS * SUBLANES, LANES),
                               lambda i, be, used, *_: (jnp.minimum(i, used[0] - 1), 0)),
                  pl.BlockSpec(memory_space=pl.ANY), pl.BlockSpec(memory_space=pl.ANY),
                  pl.BlockSpec(memory_space=pl.ANY)],
        out_specs=pl.BlockSpec((EXPERT_ROWS * SUBLANES, LANES), lambda i, *_: (i, 0)),
        scratch_shapes=[pltpu.VMEM((2, d, de), F32), pltpu.VMEM((2, d, de), F32),
                        pltpu.VMEM((2, de, d), F32),
                        pltpu.VMEM((d, de), BF16), pltpu.VMEM((d, de), BF16),
                        pltpu.VMEM((de, d), BF16), pltpu.SemaphoreType.DMA((3, 2))],
    )
    return pl.pallas_call(
        _expert_kernel,
        grid_spec=grid_spec,
        out_shape=jax.ShapeDtypeStruct((rows * SUBLANES, LANES), U32),
        compiler_params=_cparams(("arbitrary",)),
        name="experts",
    )(blk_expert, used, nxt.astype(I32), slot.astype(I32), blk_valid, xs, w_gate, w_up, w_down)


def _combine_kernel(pos_ref, pos_next_ref, gate_ref, x1_ref, ys_hbm, wsg_ref, wsu_ref,
                    wsd_ref, g_ref, b_ref, o_ref, buf0, buf1, sems):
    tt = x1_ref.shape[0]
    i = pl.program_id(0)
    even = i % 2 == 0

    def row_copy(positions, buf, s, t, k):
        dst = pl.multiple_of((k * tt + t) * SUBLANES, SUBLANES)
        return pltpu.make_async_copy(ys_hbm.at[positions[k, t]], buf.at[pl.ds(dst, SUBLANES)],
                                     sems.at[s])

    def gather(positions, buf, s, start):
        def step(t, c):
            for k in range(TOP_K):
                copy = row_copy(positions, buf, s, t, k)
                copy.start(priority=k % DMA_QUEUES) if start else copy.wait()
            return c
        lax.fori_loop(0, tt, step, 0)

    @pl.when(i == 0)
    def _():
        gather(pos_ref, buf0, 0, True)

    has_next = i + 1 < pl.num_programs(0)

    @pl.when(has_next & even)
    def _():
        gather(pos_next_ref, buf1, 1, True)

    @pl.when(has_next & jnp.logical_not(even))
    def _():
        gather(pos_next_ref, buf0, 0, True)

    x1 = x1_ref[...]
    xb = x1.astype(BF16)
    g = jnp.dot(xb, wsg_ref[...], preferred_element_type=F32)
    u = jnp.dot(xb, wsu_ref[...], preferred_element_type=F32)
    h = (g * jax.nn.sigmoid(g) * u).astype(BF16)
    y = DN_ALPHA * x1 + jnp.dot(h, wsd_ref[...], preferred_element_type=F32)

    gate = gate_ref[...]

    def finish(buf, s):
        gather(pos_ref, buf, s, False)
        r_lo = jnp.zeros((tt, TILE_WORDS), F32)
        r_hi = jnp.zeros((tt, TILE_WORDS), F32)
        for k in range(TOP_K):
            lo, hi = _unpack_bf16_pair(_load_rows_from_tiles(buf, k * tt, tt))
            r_lo = r_lo + lo * gate[:, k:k + 1]
            r_hi = r_hi + hi * gate[:, k:k + 1]
        out = y + jnp.concatenate([r_lo, r_hi], axis=1)
        o_ref[...] = _layer_norm(out, g_ref[...], b_ref[...])

    @pl.when(even)
    def _():
        finish(buf0, 0)

    @pl.when(jnp.logical_not(even))
    def _():
        finish(buf1, 1)


def _combine(x1, ys, pos_t, gate_nk, wsg_bf, wsu_bf, wsd_bf, ln_g, ln_b):
    n, d = x1.shape
    de = wsg_bf.shape[1]
    smem_tok = pl.BlockSpec((TOP_K, COMBINE_TT), lambda i, *_: (0, i), memory_space=pltpu.SMEM)
    steps = n // COMBINE_TT
    smem_next = pl.BlockSpec((TOP_K, COMBINE_TT), lambda i, *_: (0, jnp.minimum(i + 1, steps - 1)),
                             memory_space=pltpu.SMEM)
    const = lambda shape: pl.BlockSpec(shape, lambda i, *_: (0,) * len(shape))
    buf = pltpu.VMEM((TOP_K * COMBINE_TT * SUBLANES, LANES), U32)
    grid_spec = pltpu.PrefetchScalarGridSpec(
        num_scalar_prefetch=0,
        grid=(steps,),
        in_specs=[smem_tok, smem_next,
                  pl.BlockSpec((COMBINE_TT, TOP_K), lambda i, *_: (i, 0)),
                  pl.BlockSpec((COMBINE_TT, d), lambda i, *_: (i, 0)),
                  pl.BlockSpec(memory_space=pl.ANY),
                  const((d, de)), const((d, de)), const((de, d)), const((1, d)), const((1, d))],
        out_specs=pl.BlockSpec((COMBINE_TT, d), lambda i, *_: (i, 0)),
        scratch_shapes=[buf, buf, pltpu.SemaphoreType.DMA((2,))],
    )
    return pl.pallas_call(
        _combine_kernel,
        grid_spec=grid_spec,
        out_shape=jax.ShapeDtypeStruct((n, d), F32),
        compiler_params=_cparams(("arbitrary",)),
        name="combine_ln",
    )(pos_t, pos_t, gate_nk, x1, ys, wsg_bf, wsu_bf, wsd_bf, ln_g.reshape(1, d), ln_b.reshape(1, d))


def _block_layout(counts_f, n_assign):
    counts = counts_f.astype(I32)
    padded = (counts + EXPERT_ROWS - 1) // EXPERT_ROWS * EXPERT_ROWS
    end = jnp.cumsum(padded)
    total_rows = n_assign + N_EXPERTS * EXPERT_ROWS
    start = jnp.concatenate([end - padded, jnp.full((1,), total_rows, I32)])
    nblk = total_rows // EXPERT_ROWS
    used = end[-1:] // EXPERT_ROWS
    blk_start = jnp.minimum(jnp.arange(nblk, dtype=I32), used[0] - 1) * EXPERT_ROWS
    blk_expert = jnp.sum((end[None, :] <= blk_start[:, None]).astype(I32), axis=1)
    blk_expert = jnp.minimum(blk_expert, N_EXPERTS - 1)
    mine = blk_expert[:, None] == jnp.arange(N_EXPERTS, dtype=I32)[None, :]
    row_end = jnp.sum(jnp.where(mine, (start[:N_EXPERTS] + counts)[None, :], 0), axis=1)
    blk_valid = jnp.clip(row_end - blk_start, 0, EXPERT_ROWS)
    return start, counts, blk_expert, used.astype(I32), blk_valid, total_rows


def kernel(x, w_in, conv_w, conv_b, w_rg_a, b_rg_a, w_rg_x, b_rg_x, lru_lambda, w_out, ln1_g, ln1_b,
           w_router, router_bias, w_gate, w_up, w_down, ws_gate, ws_up, ws_down, ln2_g, ln2_b):
    b, s, d = x.shape
    n = b * s
    width = d // 2
    for l in range(DEPTH):
        x2 = x.reshape(n, d)
        proj = _proj(x2, w_in[l], s, width)
        proj3 = proj.reshape(b, s, proj.shape[1])
        lru = _rglru(proj3, conv_w[l], conv_b[l], w_rg_a[l].astype(BF16), b_rg_a[l],
                     w_rg_x[l].astype(BF16), b_rg_x[l], lru_lambda[l], width)
        att = _attention(proj3, width)
        x1, x1p, idx_t, gate_t, rank_t, cnt = _mix_route(
            lru.reshape(n, width), att.reshape(n, width), x2, w_out[l].astype(BF16),
            ln1_g[l], ln1_b[l], w_router[l], router_bias[l])
        start, counts, blk_expert, used, blk_valid, total_rows = _block_layout(cnt[:, 0], n * TOP_K)
        onehot = idx_t[:, :, None] == jnp.arange(N_EXPERTS, dtype=I32)[None, None, :]
        pos_t = jnp.sum(jnp.where(onehot, start[None, None, :N_EXPERTS], 0), axis=2) + rank_t
        xs = _dispatch(x1p.reshape(n, SUBLANES, LANES), pos_t, start, counts, total_rows)
        ys = _experts(xs.reshape(total_rows * SUBLANES, LANES), blk_expert, used, blk_valid,
                      w_gate[l], w_up[l], w_down[l])
        out = _combine(x1, ys.reshape(total_rows, SUBLANES, LANES), pos_t, gate_t.T,
                       ws_gate[l].astype(BF16),
                       ws_up[l].astype(BF16), ws_down[l].astype(BF16), ln2_g[l], ln2_b[l])
        x = out.reshape(b, s, d)
    return x
```

```python
import functools

import jax
import jax.numpy as jnp
from jax import lax
from jax.experimental import pallas as pl
from jax.experimental.pallas import tpu as pltpu

F32 = jnp.float32
BF16 = jnp.bfloat16
I32 = jnp.int32
U32 = jnp.uint32

LRU_BLOCKS = 8
CONV_W = 4
LRU_C = 8.0
HEAD_DIM = 128
ROT_DIM = HEAD_DIM // 4
ROPE_THETA = 500000.0
DILATED_PATTERNS = ((128, 1), (512, 4), (2048, 16))
ATTN_BLOCK = 128
N_EXPERTS = 64
N_GROUPS = 8
GROUP_SIZE = N_EXPERTS // N_GROUPS
TOPK_GROUPS = 4
TOP_K = 8
ROUTED_SCALE = 2.5
DEPTH = 1
DN_ALPHA = (2 * DEPTH) ** 0.25
LN_EPS = 1e-5
Q_GROUP, K_GROUP = 2, 3

PROJ_TM = 1024
LRU_TC = 1024
ATTN_HEADS_PER_STEP = 2
MIX_TM = 512
EXPERT_ROWS = 512
DISPATCH_TT = 512
COMBINE_TT = 256
ZERO_ROWS = 64
DMA_QUEUES = 2
SUBLANES = 8
LANES = 128
TILE_WORDS = SUBLANES * LANES
V7X_VMEM_BYTES = 64 * 1024 * 1024
VMEM_LIMIT = V7X_VMEM_BYTES * 7 // 8


def _pack_bf16_pair(lo, hi):
    lo_bits = lax.bitcast_convert_type(lo.astype(BF16).astype(F32), U32)
    hi_bits = lax.bitcast_convert_type(hi.astype(BF16).astype(F32), U32)
    return (lo_bits >> 16) | (hi_bits & jnp.uint32(0xFFFF0000))


def _unpack_bf16_pair(packed):
    lo = lax.bitcast_convert_type(packed << 16, F32)
    hi = lax.bitcast_convert_type(packed & jnp.uint32(0xFFFF0000), F32)
    return lo, hi


def _store_rows_as_tiles(ref, words):
    rows = words.shape[0]
    for s in range(SUBLANES):
        ref[pl.ds(s, rows, stride=SUBLANES), :] = words[:, s * LANES:(s + 1) * LANES]


def _load_rows_from_tiles(ref, row0, rows):
    return jnp.concatenate(
        [ref[pl.ds(row0 * SUBLANES + s, rows, stride=SUBLANES), :] for s in range(SUBLANES)], axis=1)


def _cparams(sem):
    return pltpu.CompilerParams(dimension_semantics=sem, vmem_limit_bytes=VMEM_LIMIT)


def _proj_kernel(x_ref, w_ref, c_ref, sa_ref, sb_ref, o_ref, wb_ref, *, n_heads):
    j = pl.program_id(0)

    @pl.when(pl.program_id(1) == 0)
    def _():
        wb_ref[...] = w_ref[...].astype(BF16)

    acc = jnp.dot(x_ref[...].astype(BF16), wb_ref[...], preferred_element_type=F32)
    is_rope = (j == Q_GROUP) | (j == K_GROUP)

    @pl.when(is_rope)
    def _():
        scale = jnp.where(j == Q_GROUP, HEAD_DIM ** -0.5, 1.0).astype(F32)
        c = c_ref[...] * scale
        sa = sa_ref[...] * scale
        sb = sb_ref[...] * scale
        half = ROT_DIM // 2
        for h in range(n_heads):
            t = acc[:, h * HEAD_DIM:(h + 1) * HEAD_DIM]
            o_ref[:, h * HEAD_DIM:(h + 1) * HEAD_DIM] = (
                t * c + pltpu.roll(t, half, 1) * sa + pltpu.roll(t, HEAD_DIM - half, 1) * sb)

    @pl.when(jnp.logical_not(is_rope))
    def _():
        o_ref[...] = acc


def _rope_tables(seq):
    half = ROT_DIM // 2
    pos = jnp.arange(seq, dtype=F32)
    inv = ROPE_THETA ** (-jnp.arange(0, ROT_DIM, 2, dtype=F32) / ROT_DIM)
    ang = pos[:, None] * inv[None, :]
    cos, sin = jnp.cos(ang), jnp.sin(ang)
    ones = jnp.ones((seq, HEAD_DIM - ROT_DIM), F32)
    zeros = jnp.zeros((seq, HEAD_DIM - ROT_DIM), F32)
    zh = jnp.zeros((seq, half), F32)
    c = jnp.concatenate([cos, cos, ones], -1)
    sa = jnp.concatenate([zh, sin, zeros], -1)
    sb = jnp.concatenate([-sin, zh, zeros], -1)
    return c, sa, sb


def _proj(x2, w_in, seq, width):
    n, d = x2.shape
    e = w_in.shape[1]
    n_groups = e // width
    c, sa, sb = _rope_tables(seq)
    tiles_per_seq = seq // PROJ_TM
    tab_spec = pl.BlockSpec((PROJ_TM, HEAD_DIM), lambda j, i: (i % tiles_per_seq, 0))
    return pl.pallas_call(
        functools.partial(_proj_kernel, n_heads=width // HEAD_DIM),
        grid=(n_groups, n // PROJ_TM),
        in_specs=[pl.BlockSpec((PROJ_TM, d), lambda j, i: (i, 0)),
                  pl.BlockSpec((d, width), lambda j, i: (0, j)),
                  tab_spec, tab_spec, tab_spec],
        out_specs=pl.BlockSpec((PROJ_TM, width), lambda j, i: (i, j)),
        out_shape=jax.ShapeDtypeStruct((n, e), F32),
        scratch_shapes=[pltpu.VMEM((d, width), BF16)],
        compiler_params=_cparams(("arbitrary", "arbitrary")),
        name="proj_rope",
    )(x2, w_in, c, sa, sb)


def _lru_kernel(xl_ref, gl_ref, cw_ref, cb_ref, wa_ref, ba_ref, wx_ref, bx_ref, lam_ref,
                o_ref, xext_ref, hprev_ref, a_ref, b_ref, ga_ref, gb_ref, hin_ref, *, block_w):
    tc, width = xl_ref.shape
    groups = tc // SUBLANES

    @pl.when(pl.program_id(1) == 0)
    def _():
        xext_ref[0:SUBLANES, :] = jnp.zeros((SUBLANES, width), F32)
        hprev_ref[...] = jnp.zeros_like(hprev_ref)

    x = xl_ref[...]
    xext_ref[SUBLANES:SUBLANES + tc, :] = x
    xc = x * cw_ref[CONV_W - 1:CONV_W, :] + cb_ref[...]
    for j in range(CONV_W - 1):
        back = CONV_W - 1 - j
        xc = xc + xext_ref[pl.ds(SUBLANES - back, tc), :] * cw_ref[j:j + 1, :]
    xext_ref[0:SUBLANES, :] = x[tc - SUBLANES:, :]

    lam = lam_ref[...]
    softplus_neg_lam = jnp.maximum(-lam, 0.0) + jnp.log1p(jnp.exp(-jnp.abs(lam)))
    rs, is_ = [], []
    for nb in range(width // block_w):
        xb = xc[:, nb * block_w:(nb + 1) * block_w].astype(BF16)
        rs.append(jnp.dot(xb, wa_ref[nb], preferred_element_type=F32))
        is_.append(jnp.dot(xb, wx_ref[nb], preferred_element_type=F32))
    r = jax.nn.sigmoid(jnp.concatenate(rs, axis=1) + ba_ref[...])
    ig = jax.nn.sigmoid(jnp.concatenate(is_, axis=1) + bx_ref[...])
    a = jnp.exp(-LRU_C * r * softplus_neg_lam)
    b = jnp.sqrt(1.0 - a * a) * (ig * xc)
    ncol = width // LANES
    for c in range(ncol):
        a_ref[c] = a[:, c * LANES:(c + 1) * LANES]
        b_ref[c] = b[:, c * LANES:(c + 1) * LANES]

    def member(ref, j):
        return jnp.concatenate(
            [ref[c, pl.ds(j, groups, stride=SUBLANES), :] for c in range(ncol)], axis=1)

    big_a, big_b = [member(a_ref, 0)], [member(b_ref, 0)]
    for j in range(1, SUBLANES):
        aj = member(a_ref, j)
        big_b.append(aj * big_b[-1] + member(b_ref, j))
        big_a.append(aj * big_a[-1])
    ga_ref[...] = big_a[-1]
    gb_ref[...] = big_b[-1]

    def carry(g, hprev):
        hin_ref[pl.ds(g, 1), :] = hprev
        return ga_ref[pl.ds(g, 1), :] * hprev + gb_ref[pl.ds(g, 1), :]

    hprev_ref[...] = lax.fori_loop(0, groups, carry, hprev_ref[...])

    hin = hin_ref[...]
    for j in range(SUBLANES):
        hj = big_a[j] * hin + big_b[j]
        for c in range(ncol):
            a_ref[c, pl.ds(j, groups, stride=SUBLANES), :] = hj[:, c * LANES:(c + 1) * LANES]
    h = jnp.concatenate([a_ref[c] for c in range(ncol)], axis=1)
    o_ref[...] = (h * jax.nn.gelu(gl_ref[...], approximate=True)).astype(o_ref.dtype)


def _rglru(proj3, conv_w, conv_b, wa_bf, b_a, wx_bf, b_x, lam, width):
    b, s, _ = proj3.shape
    block_w = width // LRU_BLOCKS
    vec = pl.BlockSpec((1, width), lambda bi, ci: (0, 0))
    wspec = pl.BlockSpec((LRU_BLOCKS, block_w, block_w), lambda bi, ci: (0, 0, 0))
    groups = LRU_TC // SUBLANES
    return pl.pallas_call(
        functools.partial(_lru_kernel, block_w=block_w),
        grid=(b, s // LRU_TC),
        in_specs=[pl.BlockSpec((None, LRU_TC, width), lambda bi, ci: (bi, ci, 0)),
                  pl.BlockSpec((None, LRU_TC, width), lambda bi, ci: (bi, ci, 1)),
                  pl.BlockSpec((CONV_W, width), lambda bi, ci: (0, 0)),
                  vec, wspec, vec, wspec, vec, vec],
        out_specs=pl.BlockSpec((None, LRU_TC, width), lambda bi, ci: (bi, ci, 0)),
        out_shape=jax.ShapeDtypeStruct((b, s, width), BF16),
        scratch_shapes=[pltpu.VMEM((LRU_TC + SUBLANES, width), F32),
                        pltpu.VMEM((1, width), F32),
                        pltpu.VMEM((width // LANES, LRU_TC, LANES), F32),
                        pltpu.VMEM((width // LANES, LRU_TC, LANES), F32),
                        pltpu.VMEM((groups, width), F32),
                        pltpu.VMEM((groups, width), F32),
                        pltpu.VMEM((groups, width), F32)],
        compiler_params=_cparams(("arbitrary", "arbitrary")),
        name="rglru",
    )(proj3, proj3, conv_w, conv_b.reshape(1, width), wa_bf, b_a.reshape(1, width),
      wx_bf, b_x.reshape(1, width), lam.reshape(1, width))


def _attend_head(q_ref, k_ref, v_ref, acc_n, m_n, l_n):
    seq = q_ref.shape[0]
    blk = ATTN_BLOCK
    n_pat = len(DILATED_PATTERNS)

    qi = lax.broadcasted_iota(I32, (blk, 2 * blk), 0)
    kj = lax.broadcasted_iota(I32, (blk, 2 * blk), 1)
    band2 = (kj >= qi) & (kj <= qi + blk)
    qi1 = lax.broadcasted_iota(I32, (blk, blk), 0)
    kj1 = lax.broadcasted_iota(I32, (blk, blk), 1)
    causal1 = kj1 <= qi1
    nt = (((1,), (1,)), ((), ()))

    ones = jnp.ones((2 * blk, HEAD_DIM), BF16)

    def rows(d, r, g0, nblocks):
        return pl.ds(r + d * blk * g0, nblocks * blk, stride=d) if d > 1 else pl.ds(blk * g0, nblocks * blk)

    for p, (window, d) in enumerate(DILATED_PATTERNS):
        assert window // d == blk
        for r in range(d):
            for g in range((seq // d) // blk):
                q = q_ref[rows(d, r, g, 1), :].astype(BF16)
                if g == 0:
                    kk = k_ref[rows(d, r, 0, 1), :].astype(BF16)
                    vv = v_ref[rows(d, r, 0, 1), :].astype(BF16)
                    mask, nk = causal1, blk
                else:
                    kk = k_ref[rows(d, r, g - 1, 2), :].astype(BF16)
                    vv = v_ref[rows(d, r, g - 1, 2), :].astype(BF16)
                    mask, nk = band2, 2 * blk
                s = lax.dot_general(q, kk, nt, preferred_element_type=F32)
                s = jnp.where(mask, s, -jnp.inf)
                m = jnp.max(s, axis=1, keepdims=True)
                e = jnp.exp(s - m)
                v_ones = jnp.concatenate([vv, ones[0:nk]], axis=1)
                acc_l = jnp.dot(e.astype(BF16), v_ones, preferred_element_type=F32)
                dst = rows(d, r, g, 1)
                acc_n[p, dst, :] = acc_l[:, 0:HEAD_DIM]
                m_n[p, dst, :] = jnp.broadcast_to(m, (blk, HEAD_DIM))
                l_n[p, dst, :] = acc_l[:, HEAD_DIM:]

    mx = m_n[0]
    for p in range(1, n_pat):
        mx = jnp.maximum(mx, m_n[p])
    num = jnp.zeros((seq, HEAD_DIM), F32)
    den = jnp.zeros((seq, HEAD_DIM), F32)
    for p in range(n_pat):
        w = jnp.exp(m_n[p] - mx)
        num = num + w * acc_n[p]
        den = den + w * l_n[p]
    return num / den


def _attn_kernel(*refs):
    nh = ATTN_HEADS_PER_STEP
    ins, o_ref, scratch = refs[:3 * nh], refs[3 * nh], refs[3 * nh + 1:]
    for h in range(nh):
        out = _attend_head(*ins[3 * h:3 * h + 3], *scratch[3 * h:3 * h + 3])
        o_ref[:, h * HEAD_DIM:(h + 1) * HEAD_DIM] = out.astype(o_ref.dtype)


def _attention(proj3, width):
    b, s, _ = proj3.shape
    n_heads = width // HEAD_DIM
    n_pat = len(DILATED_PATTERNS)
    col0 = 2 * width // HEAD_DIM

    nh = ATTN_HEADS_PER_STEP

    def spec(group, h):
        return pl.BlockSpec((None, s, HEAD_DIM),
                            lambda bi, hi: (bi, 0, col0 + group * n_heads + hi * nh + h))

    return pl.pallas_call(
        _attn_kernel,
        grid=(b, n_heads // nh),
        in_specs=[spec(group, h) for h in range(nh) for group in range(3)],
        out_specs=pl.BlockSpec((None, s, nh * HEAD_DIM), lambda bi, hi: (bi, 0, hi)),
        out_shape=jax.ShapeDtypeStruct((b, s, width), BF16),
        scratch_shapes=[pltpu.VMEM((n_pat, s, HEAD_DIM), F32)] * (3 * nh),
        compiler_params=_cparams(("arbitrary", "arbitrary")),
        name="dilated_attention",
    )(*([proj3] * (3 * nh)))


def _layer_norm(y, g, b):
    mu = jnp.mean(y, axis=-1, keepdims=True)
    yc = y - mu
    var = jnp.mean(yc * yc, axis=-1, keepdims=True)
    return yc * lax.rsqrt(var + LN_EPS) * g + b


def _first_index(hit, iota, size):
    return jnp.min(jnp.where(hit, iota, size), axis=0, keepdims=True)


def _select_experts(scores, sel):
    tok = scores.shape[1]
    neg = -jnp.inf
    gi = lax.broadcasted_iota(I32, (GROUP_SIZE, tok), 0)
    ni = lax.broadcasted_iota(I32, (N_GROUPS, tok), 0)
    gscore = jnp.zeros((N_GROUPS, tok), F32)
    for g in range(N_GROUPS):
        blk = sel[g * GROUP_SIZE:(g + 1) * GROUP_SIZE, :]
        m1 = jnp.max(blk, axis=0, keepdims=True)
        f1 = _first_index(blk == m1, gi, GROUP_SIZE)
        m2 = jnp.max(jnp.where(gi == f1, neg, blk), axis=0, keepdims=True)
        gscore = jnp.where(ni == g, m1 + m2, gscore)
    gmask = jnp.zeros((N_GROUPS, tok), jnp.bool_)
    for _ in range(TOPK_GROUPS):
        m = jnp.max(gscore, axis=0, keepdims=True)
        hit = ni == _first_index(gscore == m, ni, N_GROUPS)
        gmask = gmask | hit
        gscore = jnp.where(hit, neg, gscore)
    emask = jnp.concatenate(
        [jnp.broadcast_to(gmask[g:g + 1, :], (GROUP_SIZE, tok)) for g in range(N_GROUPS)], axis=0)
    masked = jnp.where(emask, sel, neg)

    ei = lax.broadcasted_iota(I32, (N_EXPERTS, tok), 0)
    chosen = jnp.zeros((N_EXPERTS, tok), jnp.bool_)
    picks, gates = [], []
    for _ in range(TOP_K):
        m = jnp.max(masked, axis=0, keepdims=True)
        f = _first_index(masked == m, ei, N_EXPERTS)
        hit = ei == f
        picks.append(f)
        gates.append(jnp.sum(jnp.where(hit, scores, 0.0), axis=0, keepdims=True))
        chosen = chosen | hit
        masked = jnp.where(hit, neg, masked)
    gsum = gates[0]
    for k in range(1, TOP_K):
        gsum = gsum + gates[k]
    gates = [g / gsum * ROUTED_SCALE for g in gates]
    return picks, gates, jnp.where(chosen, 1.0, 0.0)


def _mix_kernel(lru_ref, att_ref, x_ref, wo_ref, g_ref, b_ref, wrh_ref, wrl_ref, rb_ref, tri_ref,
                x1_ref, x1p_ref, idx_ref, gate_ref, rank_ref, cnt_ref, base_ref):
    tm = x_ref.shape[0]
    half = lru_ref.shape[1]

    @pl.when(pl.program_id(0) == 0)
    def _():
        base_ref[...] = jnp.zeros_like(base_ref)

    mix = jnp.dot(lru_ref[...], wo_ref[0:half, :], preferred_element_type=F32)
    mix = mix + jnp.dot(att_ref[...], wo_ref[half:, :], preferred_element_type=F32)
    x1 = _layer_norm(DN_ALPHA * x_ref[...] + mix, g_ref[...], b_ref[...])
    x1_ref[...] = x1
    _store_rows_as_tiles(x1p_ref, _pack_bf16_pair(x1[:, 0:half], x1[:, half:]))

    x_hi = x1.astype(BF16)
    x_lo = (x1 - x_hi.astype(F32)).astype(BF16)
    nt = (((1,), (1,)), ((), ()))
    logits = lax.dot_general(wrh_ref[...], x_hi, nt, preferred_element_type=F32)
    logits = logits + lax.dot_general(wrh_ref[...], x_lo, nt, preferred_element_type=F32)
    logits = logits + lax.dot_general(wrl_ref[...], x_hi, nt, preferred_element_type=F32)
    scores = jax.nn.sigmoid(logits)
    sel = scores + rb_ref[...]

    chunks = [_select_experts(scores[:, c:c + LANES], sel[:, c:c + LANES])
              for c in range(0, tm, LANES)]

    chosen_f = jnp.concatenate([ch[2] for ch in chunks], axis=1)
    incl = jnp.dot(chosen_f.astype(BF16), tri_ref[...], preferred_element_type=F32)
    rank_excl = base_ref[:, 0:1] + incl - chosen_f
    base_ref[...] = base_ref[...] + jnp.sum(chosen_f, axis=1, keepdims=True)
    cnt_ref[...] = base_ref[...]
    ei = lax.broadcasted_iota(I32, (N_EXPERTS, LANES), 0)
    for ci, (picks, gates, _) in enumerate(chunks):
        cols = slice(ci * LANES, (ci + 1) * LANES)
        for k in range(TOP_K):
            idx_ref[k:k + 1, cols] = picks[k]
            gate_ref[k:k + 1, cols] = gates[k]
            rk = jnp.sum(jnp.where(ei == picks[k], rank_excl[:, cols], 0.0), axis=0, keepdims=True)
            rank_ref[k:k + 1, cols] = rk.astype(I32)


def _mix_route(lru, att, x2, wo_bf, ln_g, ln_b, w_router, router_bias):
    n, d = x2.shape
    half = lru.shape[1]
    wr_t = w_router.T
    wr_hi = wr_t.astype(BF16)
    wr_lo = (wr_t - wr_hi.astype(F32)).astype(BF16)
    assert half == TILE_WORDS
    tri = jnp.triu(jnp.ones((MIX_TM, MIX_TM), BF16))
    const = lambda shape: pl.BlockSpec(shape, lambda i: (0,) * len(shape))
    tok = lambda dtype: jax.ShapeDtypeStruct((TOP_K, n), dtype)
    tok_spec = pl.BlockSpec((TOP_K, MIX_TM), lambda i: (0, i))
    return pl.pallas_call(
        _mix_kernel,
        grid=(n // MIX_TM,),
        in_specs=[pl.BlockSpec((MIX_TM, half), lambda i: (i, 0)),
                  pl.BlockSpec((MIX_TM, half), lambda i: (i, 0)),
                  pl.BlockSpec((MIX_TM, d), lambda i: (i, 0)),
                  const((d, d)), const((1, d)), const((1, d)),
                  const((N_EXPERTS, d)), const((N_EXPERTS, d)), const((N_EXPERTS, 1)),
                  const((MIX_TM, MIX_TM))],
        out_specs=[pl.BlockSpec((MIX_TM, d), lambda i: (i, 0)),
                   pl.BlockSpec((MIX_TM * SUBLANES, LANES), lambda i: (i, 0)),
                   tok_spec, tok_spec, tok_spec, const((N_EXPERTS, HEAD_DIM))],
        out_shape=[jax.ShapeDtypeStruct((n, d), F32), jax.ShapeDtypeStruct((n * SUBLANES, LANES), U32),
                   tok(I32), tok(F32), tok(I32),
                   jax.ShapeDtypeStruct((N_EXPERTS, HEAD_DIM), F32)],
        scratch_shapes=[pltpu.VMEM((N_EXPERTS, HEAD_DIM), F32)],
        compiler_params=_cparams(("arbitrary",)),
        name="mix_ln_route",
    )(lru, att, x2, wo_bf, ln_g.reshape(1, d), ln_b.reshape(1, d), wr_hi, wr_lo,
      router_bias.reshape(N_EXPERTS, 1), tri)


def _dispatch_kernel(start_ref, cnt_ref, pos_ref, x1_ref, zero_ref, xs_hbm, sem):
    i = pl.program_id(0)
    tt = pos_ref.shape[1]

    def row_copy(t, k):
        return pltpu.make_async_copy(x1_ref.at[t], xs_hbm.at[pos_ref[k, t]], sem)

    def issue(t, c):
        for k in range(TOP_K):
            row_copy(t, k).start(priority=k % DMA_QUEUES)
        return c

    def drain(t, c):
        for k in range(TOP_K):
            row_copy(t, k).wait()
        return c

    lax.fori_loop(0, tt, issue, 0)
    lax.fori_loop(0, tt, drain, 0)

    @pl.when(i == 0)
    def _():
        def pad_copy(r, rows):
            return pltpu.make_async_copy(zero_ref.at[pl.ds(0, rows)], xs_hbm.at[pl.ds(r, rows)], sem)

        def per_expert(e, c):
            lo = start_ref[e] + cnt_ref[e]
            hi = start_ref[e + 1]
            chunks = (hi - lo) // ZERO_ROWS
            mid = lo + chunks * ZERO_ROWS
            big = lambda j: pad_copy(lo + j * ZERO_ROWS, ZERO_ROWS)
            lax.fori_loop(0, chunks, lambda j, cc: (big(j).start(), cc)[1], 0)
            lax.fori_loop(mid, hi, lambda r, cc: (pad_copy(r, 1).start(), cc)[1], 0)
            lax.fori_loop(0, chunks, lambda j, cc: (big(j).wait(), cc)[1], 0)
            lax.fori_loop(mid, hi, lambda r, cc: (pad_copy(r, 1).wait(), cc)[1], 0)
            return c

        lax.fori_loop(0, N_EXPERTS, per_expert, 0)


def _dispatch(x1p, pos_t, start, counts, total_rows):
    n = x1p.shape[0]
    smem_tok = pl.BlockSpec((TOP_K, DISPATCH_TT), lambda i, *_: (0, i), memory_space=pltpu.SMEM)
    grid_spec = pltpu.PrefetchScalarGridSpec(
        num_scalar_prefetch=2,
        grid=(n // DISPATCH_TT,),
        in_specs=[smem_tok,
                  pl.BlockSpec((DISPATCH_TT, SUBLANES, LANES), lambda i, *_: (i, 0, 0)),
                  pl.BlockSpec((ZERO_ROWS, SUBLANES, LANES), lambda i, *_: (0, 0, 0))],
        out_specs=pl.BlockSpec(memory_space=pl.ANY),
        scratch_shapes=[pltpu.SemaphoreType.DMA(())],
    )
    return pl.pallas_call(
        _dispatch_kernel,
        grid_spec=grid_spec,
        out_shape=jax.ShapeDtypeStruct((total_rows, SUBLANES, LANES), x1p.dtype),
        compiler_params=_cparams(("arbitrary",)),
        name="dispatch",
    )(start, counts, pos_t, x1p, jnp.zeros((ZERO_ROWS, SUBLANES, LANES), x1p.dtype))


def _expert_kernel(be_ref, used_ref, nxt_ref, slot_ref, valid_ref, xs_ref, wg_hbm, wu_hbm, wd_hbm,
                   ys_ref, wg_f32, wu_f32, wd_f32, wg_bf, wu_bf, wd_bf, sems):
    i = pl.program_id(0)
    e = be_ref[i]
    slot = slot_ref[i]
    changed = (i == 0) | (e != be_ref[jnp.maximum(i - 1, 0)])

    def weight_copies(expert, s):
        return (pltpu.make_async_copy(wg_hbm.at[expert], wg_f32.at[s], sems.at[0, s]),
                pltpu.make_async_copy(wu_hbm.at[expert], wu_f32.at[s], sems.at[1, s]),
                pltpu.make_async_copy(wd_hbm.at[expert], wd_f32.at[s], sems.at[2, s]))

    @pl.when(i == 0)
    def _():
        for c in weight_copies(e, slot):
            c.start(priority=1)

    @pl.when(changed)
    def _():
        for c in weight_copies(e, slot):
            c.wait()
        nxt = nxt_ref[i]

        @pl.when(nxt != e)
        def _():
            for c in weight_copies(nxt, 1 - slot):
                c.start(priority=1)

        wg_bf[...] = wg_f32[slot].astype(BF16)
        wu_bf[...] = wu_f32[slot].astype(BF16)
        wd_bf[...] = wd_f32[slot].astype(BF16)

    def swiglu(rows):
        half = TILE_WORDS
        lo, hi = _unpack_bf16_pair(_load_rows_from_tiles(xs_ref, 0, rows))
        lo, hi = lo.astype(BF16), hi.astype(BF16)
        g = (jnp.dot(lo, wg_bf[0:half, :], preferred_element_type=F32)
             + jnp.dot(hi, wg_bf[half:, :], preferred_element_type=F32))
        u = (jnp.dot(lo, wu_bf[0:half, :], preferred_element_type=F32)
             + jnp.dot(hi, wu_bf[half:, :], preferred_element_type=F32))
        h = (g * jax.nn.sigmoid(g) * u).astype(BF16)
        y = jnp.dot(h, wd_bf[...], preferred_element_type=F32)
        _store_rows_as_tiles(ys_ref, _pack_bf16_pair(y[:, 0:half], y[:, half:]))
        if rows < EXPERT_ROWS:
            ys_ref[rows * SUBLANES:, :] = jnp.zeros(((EXPERT_ROWS - rows) * SUBLANES, LANES), U32)

    in_use = i < used_ref[0]
    few = valid_ref[i] <= EXPERT_ROWS // 2

    @pl.when(in_use & jnp.logical_not(few))
    def _():
        swiglu(EXPERT_ROWS)

    @pl.when(in_use & few)
    def _():
        swiglu(EXPERT_ROWS // 2)

    @pl.when(jnp.logical_not(in_use))
    def _():
        ys_ref[...] = jnp.zeros_like(ys_ref)


def _experts(xs, blk_expert, used, blk_valid, w_gate, w_up, w_down):
    rows = xs.shape[0] // SUBLANES
    d = 2 * TILE_WORDS
    de = w_gate.shape[2]
    nblk = rows // EXPERT_ROWS
    later = blk_expert[None, :] > blk_expert[:, None]
    nxt = jnp.min(jnp.where(later, blk_expert[None, :], N_EXPERTS), axis=1)
    nxt = jnp.where(nxt == N_EXPERTS, blk_expert, nxt)
    is_new = jnp.concatenate([jnp.zeros((1,), I32), (blk_expert[1:] != blk_expert[:-1]).astype(I32)])
    slot = jnp.cumsum(is_new) % 2
    grid_spec = pltpu.PrefetchScalarGridSpec(
        num_scalar_prefetch=5,
        grid=(nblk,),
        in_specs=[pl.BlockSpec((EXPERT_ROWS * SUBLANES, LANES),
                               lambda i, be, used, *_: (jnp.minimum(i, used[0] - 1), 0)),
                  pl.BlockSpec(memory_space=pl.ANY), pl.BlockSpec(memory_space=pl.ANY),
                  pl.BlockSpec(memory_space=pl.ANY)],
        out_specs=pl.BlockSpec((EXPERT_ROWS * SUBLANES, LANES), lambda i, *_: (i, 0)),
        scratch_shapes=[pltpu.VMEM((2, d, de), F32), pltpu.VMEM((2, d, de), F32),
                        pltpu.VMEM((2, de, d), F32),
                        pltpu.VMEM((d, de), BF16), pltpu.VMEM((d, de), BF16),
                        pltpu.VMEM((de, d), BF16), pltpu.SemaphoreType.DMA((3, 2))],
    )
    return pl.pallas_call(
        _expert_kernel,
        grid_spec=grid_spec,
        out_shape=jax.ShapeDtypeStruct((rows * SUBLANES, LANES), U32),
        compiler_params=_cparams(("arbitrary",)),
        name="experts",
    )(blk_expert, used, nxt.astype(I32), slot.astype(I32), blk_valid, xs, w_gate, w_up, w_down)


def _combine_kernel(pos_ref, pos_next_ref, gate_ref, x1_ref, ys_hbm, wsg_ref, wsu_ref,
                    wsd_ref, g_ref, b_ref, o_ref, buf, sems):
    tt = x1_ref.shape[0]
    i = pl.program_id(0)
    last = pl.num_programs(0) - 1
    even = i % 2 == 0
    slot = i % 2

    def row_copy(positions, s, t, k):
        dst = (k * tt + t) * SUBLANES
        if not isinstance(dst, int):
            dst = pl.multiple_of(dst, SUBLANES)
        return pltpu.make_async_copy(ys_hbm.at[positions[k, t]], buf.at[s, pl.ds(dst, SUBLANES)],
                                     sems.at[s])

    def gather(positions, s, start):
        def step(t, c):
            for k in range(TOP_K):
                copy = row_copy(positions, s, t, k)
                copy.start(priority=k % DMA_QUEUES) if start else copy.wait()
            return c
        lax.fori_loop(0, tt, step, 0)

    @pl.when(i == 0)
    def _():
        gather(pos_ref, 0, True)

    for t in range(tt):
        for k in range(TOP_K):
            row_copy(pos_next_ref, 1 - slot, t, k).start(priority=k % DMA_QUEUES)

    x1 = x1_ref[...]
    xb = x1.astype(BF16)
    g = jnp.dot(xb, wsg_ref[...], preferred_element_type=F32)
    u = jnp.dot(xb, wsu_ref[...], preferred_element_type=F32)
    h = (g * jax.nn.sigmoid(g) * u).astype(BF16)
    y = DN_ALPHA * x1 + jnp.dot(h, wsd_ref[...], preferred_element_type=F32)

    gate = gate_ref[...]

    def finish(s):
        gather(pos_ref, s, False)
        r_lo = jnp.zeros((tt, TILE_WORDS), F32)
        r_hi = jnp.zeros((tt, TILE_WORDS), F32)
        for k in range(TOP_K):
            lo, hi = _unpack_bf16_pair(_load_rows_from_tiles(buf.at[s], k * tt, tt))
            r_lo = r_lo + lo * gate[:, k:k + 1]
            r_hi = r_hi + hi * gate[:, k:k + 1]
        out = y + jnp.concatenate([r_lo, r_hi], axis=1)
        o_ref[...] = _layer_norm(out, g_ref[...], b_ref[...])

    @pl.when(even)
    def _():
        finish(0)

    @pl.when(jnp.logical_not(even))
    def _():
        finish(1)

    @pl.when(i == last)
    def _():
        gather(pos_next_ref, 1 - slot, False)


def _combine(x1, ys, pos_t, gate_nk, wsg_bf, wsu_bf, wsd_bf, ln_g, ln_b):
    n, d = x1.shape
    de = wsg_bf.shape[1]
    smem_tok = pl.BlockSpec((TOP_K, COMBINE_TT), lambda i, *_: (0, i), memory_space=pltpu.SMEM)
    steps = n // COMBINE_TT
    smem_next = pl.BlockSpec((TOP_K, COMBINE_TT), lambda i, *_: (0, jnp.minimum(i + 1, steps - 1)),
                             memory_space=pltpu.SMEM)
    const = lambda shape: pl.BlockSpec(shape, lambda i, *_: (0,) * len(shape))
    buf = pltpu.VMEM((2, TOP_K * COMBINE_TT * SUBLANES, LANES), U32)
    grid_spec = pltpu.PrefetchScalarGridSpec(
        num_scalar_prefetch=0,
        grid=(steps,),
        in_specs=[smem_tok, smem_next,
                  pl.BlockSpec((COMBINE_TT, TOP_K), lambda i, *_: (i, 0)),
                  pl.BlockSpec((COMBINE_TT, d), lambda i, *_: (i, 0)),
                  pl.BlockSpec(memory_space=pl.ANY),
                  const((d, de)), const((d, de)), const((de, d)), const((1, d)), const((1, d))],
        out_specs=pl.BlockSpec((COMBINE_TT, d), lambda i, *_: (i, 0)),
        scratch_shapes=[buf, pltpu.SemaphoreType.DMA((2,))],
    )
    return pl.pallas_call(
        _combine_kernel,
        grid_spec=grid_spec,
        out_shape=jax.ShapeDtypeStruct((n, d), F32),
        compiler_params=_cparams(("arbitrary",)),
        name="combine_ln",
    )(pos_t, pos_t, gate_nk, x1, ys, wsg_bf, wsu_bf, wsd_bf, ln_g.reshape(1, d), ln_b.reshape(1, d))


def _block_layout(counts_f, n_assign):
    counts = counts_f.astype(I32)
    padded = (counts + EXPERT_ROWS - 1) // EXPERT_ROWS * EXPERT_ROWS
    end = jnp.cumsum(padded)
    total_rows = n_assign + N_EXPERTS * EXPERT_ROWS
    start = jnp.concatenate([end - padded, jnp.full((1,), total_rows, I32)])
    nblk = total_rows // EXPERT_ROWS
    used = end[-1:] // EXPERT_ROWS
    blk_start = jnp.minimum(jnp.arange(nblk, dtype=I32), used[0] - 1) * EXPERT_ROWS
    blk_expert = jnp.sum((end[None, :] <= blk_start[:, None]).astype(I32), axis=1)
    blk_expert = jnp.minimum(blk_expert, N_EXPERTS - 1)
    mine = blk_expert[:, None] == jnp.arange(N_EXPERTS, dtype=I32)[None, :]
    row_end = jnp.sum(jnp.where(mine, (start[:N_EXPERTS] + counts)[None, :], 0), axis=1)
    blk_valid = jnp.clip(row_end - blk_start, 0, EXPERT_ROWS)
    return start, counts, blk_expert, used.astype(I32), blk_valid, total_rows


def kernel(x, w_in, conv_w, conv_b, w_rg_a, b_rg_a, w_rg_x, b_rg_x, lru_lambda, w_out, ln1_g, ln1_b,
           w_router, router_bias, w_gate, w_up, w_down, ws_gate, ws_up, ws_down, ln2_g, ln2_b):
    b, s, d = x.shape
    n = b * s
    width = d // 2
    for l in range(DEPTH):
        x2 = x.reshape(n, d)
        proj = _proj(x2, w_in[l], s, width)
        proj3 = proj.reshape(b, s, proj.shape[1])
        lru = _rglru(proj3, conv_w[l], conv_b[l], w_rg_a[l].astype(BF16), b_rg_a[l],
                     w_rg_x[l].astype(BF16), b_rg_x[l], lru_lambda[l], width)
        att = _attention(proj3, width)
        x1, x1p, idx_t, gate_t, rank_t, cnt = _mix_route(
            lru.reshape(n, width), att.reshape(n, width), x2, w_out[l].astype(BF16),
            ln1_g[l], ln1_b[l], w_router[l], router_bias[l])
        start, counts, blk_expert, used, blk_valid, total_rows = _block_layout(cnt[:, 0], n * TOP_K)
        onehot = idx_t[:, :, None] == jnp.arange(N_EXPERTS, dtype=I32)[None, None, :]
        pos_t = jnp.sum(jnp.where(onehot, start[None, None, :N_EXPERTS], 0), axis=2) + rank_t
        xs = _dispatch(x1p.reshape(n, SUBLANES, LANES), pos_t, start, counts, total_rows)
        ys = _experts(xs.reshape(total_rows * SUBLANES, LANES), blk_expert, used, blk_valid,
                      w_gate[l], w_up[l], w_down[l])
        out = _combine(x1, ys.reshape(total_rows, SUBLANES, LANES), pos_t, gate_t.T,
                       ws_gate[l].astype(BF16),
                       ws_up[l].astype(BF16), ws_down[l].astype(BF16), ln2_g[l], ln2_b[l])
        x = out.reshape(b, s, d)
    return x
```
